```python
import jax
import jax.numpy as jnp
from jax import lax
import numpy as np

D_MODEL = 2048
BATCH = 8
SEQ = 4096
DEPTH = 2

EPS = 1e-6
A_WIDTH = D_MODEL // 2
A_BLOCKS = 8
A_BLOCK = A_WIDTH // A_BLOCKS
A_CONV = 4
LRU_C = 8.0
B_WIDTH = D_MODEL // 2
B_HEAD = 64
B_HEADS = B_WIDTH // B_HEAD
B_DECAY_RANK = 64
B_AAA_RANK = 64
B_GATE_RANK = 160
B_PROJ = 3 * B_WIDTH + B_DECAY_RANK + B_AAA_RANK + B_GATE_RANK
B_SPLITS = (B_WIDTH, 2 * B_WIDTH, 3 * B_WIDTH, 3 * B_WIDTH + B_DECAY_RANK, 3 * B_WIDTH + B_DECAY_RANK + B_AAA_RANK)
B_LN_EPS = 64e-5
EVEN_IN = 2 * A_WIDTH + B_PROJ
C_WIDTH = 2 * D_MODEL
C_HEADS = 8
C_HEAD = C_WIDTH // C_HEADS
C_QKV_BLOCK = 4
C_CONV = 4
C_CHUNK = 64
D_FF = 5632
FFN_CONV = 3
N_EVEN = (DEPTH + 1) // 2
N_ODD = DEPTH // 2

kernel_name = 'hybrid_rglru_rwkv7_mlstm_convffn'


def rms_norm(x, g):
    x32 = x.astype(jnp.float32)
    y = x32 * lax.rsqrt(jnp.mean(x32 * x32, axis=-1, keepdims=True) + EPS)
    return (y * g).astype(x.dtype)


def head_layer_norm(x, eps):
    x32 = x.astype(jnp.float32)
    xc = x32 - jnp.mean(x32, axis=-1, keepdims=True)
    return xc * lax.rsqrt(jnp.mean(xc * xc, axis=-1, keepdims=True) + eps)


def causal_dwconv(x, w, b):
    k_width, t_len = w.shape[0], x.shape[1]
    xp = jnp.pad(x, ((0, 0), (k_width - 1, 0), (0, 0)))
    out = b + xp[:, 0:t_len] * w[0]
    for j in range(1, k_width):
        out = out + xp[:, j:j + t_len] * w[j]
    return out


def token_shift(x):
    return jnp.pad(x[:, :-1], ((0, 0), (1, 0), (0, 0)))


def block_diag(x, w):
    g, bs = w.shape[0], w.shape[1]
    y = jnp.einsum('btgi,gij->btgj', x.reshape(x.shape[0], x.shape[1], g, bs), w)
    return y.reshape(x.shape)


def rg_lru(x, w_r, b_r, w_i, b_i, lam):
    x32 = x.astype(jnp.float32)
    r = jax.nn.sigmoid(block_diag(x32, w_r) + b_r)
    i = jax.nn.sigmoid(block_diag(x32, w_i) + b_i)
    log_a = -LRU_C * r * jax.nn.softplus(-lam)
    a = jnp.exp(log_a)
    u = jnp.sqrt(-jnp.expm1(2.0 * log_a)) * (i * x32)

    def combine(lhs, rhs):
        a_l, u_l = lhs
        a_r, u_r = rhs
        return a_r * a_l, a_r * u_l + u_r

    _, h = lax.associative_scan(combine, (a, u), axis=1)
    return h.astype(x.dtype)


def rwkv7_time_mix(p, mu, w0, w_up, a0, a_up, g_up, k_k, k_a, r_k, ln_w, ln_b):
    bsz, t_len, _ = p.shape
    p = p + (token_shift(p) - p) * mu
    r, k, v, xw, xa, xg = jnp.split(p, B_SPLITS, axis=-1)
    log_w = -jnp.exp(-jax.nn.softplus(-(w0 + jnp.tanh(xw) @ w_up)) - 0.5)
    a = jax.nn.sigmoid(a0 + xa @ a_up)
    g = jax.nn.sigmoid(xg) @ g_up

    def heads(t):
        return t.astype(jnp.float32).reshape(bsz, t_len, B_HEADS, B_HEAD)

    kk = heads(k * k_k)
    kk = kk * lax.rsqrt(jnp.maximum(jnp.sum(kk * kk, axis=-1, keepdims=True), 1e-12))
    k = heads(k * (1.0 + (a - 1.0) * k_a))
    r, v, a, w = heads(r), heads(v), heads(a), jnp.exp(heads(log_w))

    def step(s, inp):
        r_t, w_t, k_t, v_t, a_t, b_t = inp
        sa = jnp.einsum('bhvk,bhk->bhv', s, a_t)
        s = s * w_t[:, :, None, :] + sa[..., None] * b_t[:, :, None, :] + v_t[..., None] * k_t[:, :, None, :]
        return s, jnp.einsum('bhvk,bhk->bhv', s, r_t)

    xs = tuple(jnp.moveaxis(t, 1, 0) for t in (r, w, k, v, -kk, kk * a))
    s0 = jnp.zeros((bsz, B_HEADS, B_HEAD, B_HEAD), jnp.float32)
    _, y = lax.scan(step, s0, xs)
    y = jnp.moveaxis(y, 0, 1)
    y = head_layer_norm(y, B_LN_EPS) * ln_w.reshape(B_HEADS, B_HEAD) + ln_b.reshape(B_HEADS, B_HEAD)
    y = y + jnp.sum(r * k * r_k, axis=-1, keepdims=True) * v
    return (y.reshape(bsz, t_len, B_WIDTH) * g).astype(p.dtype)


def mlstm_chunkwise(q, k, v, i_pre, f_pre):
    bsz, t_len = q.shape[0], q.shape[1]
    nc = t_len // C_CHUNK

    def chunks(t):
        t = t.reshape((bsz, nc, C_CHUNK) + t.shape[2:])
        return jnp.moveaxis(jnp.moveaxis(t, 1, 0), 2, 3)

    causal = jnp.tril(jnp.ones((C_CHUNK, C_CHUNK), dtype=bool))

    def step(carry, inp):
        c_mat, n_vec, m = carry
        q_c, k_c, v_c, lf, li = inp
        b = jnp.cumsum(lf, axis=-1)
        d_log = jnp.where(causal, b[..., :, None] - b[..., None, :] + li[..., None, :], -jnp.inf)
        inter = b + m[..., None]
        m_t = jnp.maximum(inter, jnp.max(d_log, axis=-1))
        s = jnp.einsum('bhtd,bhsd->bhts', q_c, k_c) * jnp.exp(d_log - m_t[..., None])
        sc = jnp.exp(inter - m_t)
        num = jnp.einsum('bhts,bhsd->bhtd', s, v_c) + sc[..., None] * jnp.einsum('bhvk,bhtk->bhtv', c_mat, q_c)
        den = jnp.sum(s, axis=-1) + sc * jnp.einsum('bhk,bhtk->bht', n_vec, q_c)
        h = num / jnp.maximum(jnp.abs(den), jnp.exp(-m_t))[..., None]
        b_last = b[..., -1]
        g_log = b_last[..., None] - b + li
        m_new = jnp.maximum(b_last + m, jnp.max(g_log, axis=-1))
        e = jnp.exp(g_log - m_new[..., None])
        decay = jnp.exp(b_last + m - m_new)
        c_mat = decay[..., None, None] * c_mat + jnp.einsum('bhsv,bhsk->bhvk', v_c * e[..., None], k_c)
        n_vec = decay[..., None] * n_vec + jnp.einsum('bhs,bhsk->bhk', e, k_c)
        return (c_mat, n_vec, m_new), h

    k = k * (C_HEAD ** -0.5)
    xs = (chunks(q), chunks(k), chunks(v), chunks(jax.nn.log_sigmoid(f_pre)), chunks(i_pre))
    init = (jnp.zeros((bsz, C_HEADS, C_HEAD, C_HEAD), jnp.float32),
            jnp.zeros((bsz, C_HEADS, C_HEAD), jnp.float32),
            jnp.zeros((bsz, C_HEADS), jnp.float32))
    _, h = lax.scan(step, init, xs)
    return jnp.moveaxis(jnp.moveaxis(h, 3, 2), 0, 1).reshape(bsz, t_len, C_HEADS, C_HEAD)


def rglru_rwkv7_layer(x, norm, w_in, a_conv_w, a_conv_b, a_w_r, a_b_r, a_w_i, a_b_i, a_lambda,
                      b_mu, b_w0, b_w_up, b_a0, b_a_up, b_g_up, b_k_k, b_k_a, b_r_k, b_ln_w, b_ln_b, w_out):
    h = rms_norm(x, norm)
    xa, ga, pb = jnp.split(h @ w_in, [A_WIDTH, 2 * A_WIDTH], axis=-1)
    ya = rg_lru(causal_dwconv(xa, a_conv_w, a_conv_b), a_w_r, a_b_r, a_w_i, a_b_i, a_lambda) * jax.nn.gelu(ga)
    yb = rwkv7_time_mix(pb, b_mu, b_w0, b_w_up, b_a0, b_a_up, b_g_up, b_k_k, b_k_a, b_r_k, b_ln_w, b_ln_b)
    return x + jnp.concatenate([ya, yb], axis=-1) @ w_out


def mlstm_layer(x, norm, w_in, conv_w, conv_b, w_q, w_k, w_v, w_if, b_if, ln_w, skip, w_out):
    bsz, t_len, _ = x.shape
    h = rms_norm(x, norm)
    xm, z = jnp.split(h @ w_in, [C_WIDTH], axis=-1)
    xc = jax.nn.silu(causal_dwconv(xm, conv_w, conv_b))
    q, k, v = block_diag(xc, w_q), block_diag(xc, w_k), block_diag(xm, w_v)
    gates = (jnp.einsum('btc,cg->btg', q, w_if[0]) + jnp.einsum('btc,cg->btg', k, w_if[1])
             + jnp.einsum('btc,cg->btg', v, w_if[2]) + b_if)
    i_pre, f_pre = jnp.split(gates.astype(jnp.float32), [C_HEADS], axis=-1)

    def heads(t):
        return t.astype(jnp.float32).reshape(bsz, t_len, C_HEADS, C_HEAD)

    hc = mlstm_chunkwise(heads(q), heads(k), heads(v), i_pre, f_pre)
    hn = head_layer_norm(hc, EPS) * ln_w.reshape(C_HEADS, C_HEAD)
    hs = hn.reshape(bsz, t_len, C_WIDTH).astype(x.dtype) + skip * xc
    return x + (hs * jax.nn.silu(z)) @ w_out


def conv_ffn(x, norm, w_gate, w_up, conv_w, conv_b, w_down):
    h = rms_norm(x, norm)
    u = jax.nn.silu(causal_dwconv(h @ w_gate, conv_w, conv_b)) * (h @ w_up)
    return x + u @ w_down


def setup_inputs(seed: int = 0) -> dict:
    key = jax.random.key(seed)
    ks = list(jax.random.split(key, 48))
    f32 = jnp.float32

    def normal(i, shape, scale):
        return jax.random.normal(ks[i], shape, f32) * scale

    def uniform(i, shape, lo, hi):
        return jax.random.uniform(ks[i], shape, f32, lo, hi)

    def gain(i, shape):
        return 1.0 + normal(i, shape, 0.02)

    ne, no, nl = N_EVEN, N_ODD, DEPTH
    a_decay = uniform(9, (ne, A_WIDTH), 0.9, 0.999)
    b_if = jnp.concatenate([normal(33, (no, C_HEADS), 0.1), uniform(34, (no, C_HEADS), 3.0, 6.0)], axis=-1)
    return {
        'x': normal(0, (BATCH, SEQ, D_MODEL), 1.0),
        'even_norm': gain(1, (ne, D_MODEL)),
        'even_w_in': normal(2, (ne, D_MODEL, EVEN_IN), D_MODEL ** -0.5),
        'a_conv_w': normal(3, (ne, A_CONV, A_WIDTH), A_CONV ** -0.5),
        'a_conv_b': normal(4, (ne, A_WIDTH), 0.01),
        'a_w_r': normal(5, (ne, A_BLOCKS, A_BLOCK, A_BLOCK), A_BLOCK ** -0.5),
        'a_b_r': normal(6, (ne, A_WIDTH), 0.1),
        'a_w_i': normal(7, (ne, A_BLOCKS, A_BLOCK, A_BLOCK), A_BLOCK ** -0.5),
        'a_b_i': normal(8, (ne, A_WIDTH), 0.1),
        'a_lambda': jnp.log(a_decay) - jnp.log1p(-a_decay),
        'b_mu': uniform(10, (ne, B_PROJ), 0.0, 1.0),
        'b_w0': uniform(11, (ne, B_WIDTH), -6.0, 0.0),
        'b_w_up': normal(12, (ne, B_DECAY_RANK, B_WIDTH), 0.5 * B_DECAY_RANK ** -0.5),
        'b_a0': normal(13, (ne, B_WIDTH), 0.1),
        'b_a_up': normal(14, (ne, B_AAA_RANK, B_WIDTH), 0.5 * B_AAA_RANK ** -0.5),
        'b_g_up': normal(15, (ne, B_GATE_RANK, B_WIDTH), B_GATE_RANK ** -0.5),
        'b_k_k': uniform(16, (ne, B_WIDTH), 0.7, 1.0),
        'b_k_a': uniform(17, (ne, B_WIDTH), 0.8, 1.2),
        'b_r_k': normal(18, (ne, B_HEADS, B_HEAD), 0.1),
        'b_ln_w': gain(19, (ne, B_WIDTH)),
        'b_ln_b': normal(20, (ne, B_WIDTH), 0.01),
        'even_w_out': normal(21, (ne, D_MODEL, D_MODEL), D_MODEL ** -0.5),
        'odd_norm': gain(22, (no, D_MODEL)),
        'odd_w_in': normal(23, (no, D_MODEL, 2 * C_WIDTH), D_MODEL ** -0.5),
        'c_conv_w': normal(24, (no, C_CONV, C_WIDTH), C_CONV ** -0.5),
        'c_conv_b': normal(25, (no, C_WIDTH), 0.01),
        'c_w_q': normal(26, (no, C_WIDTH // C_QKV_BLOCK, C_QKV_BLOCK, C_QKV_BLOCK), C_QKV_BLOCK ** -0.5),
        'c_w_k': normal(27, (no, C_WIDTH // C_QKV_BLOCK, C_QKV_BLOCK, C_QKV_BLOCK), C_QKV_BLOCK ** -0.5),
        'c_w_v': normal(28, (no, C_WIDTH // C_QKV_BLOCK, C_QKV_BLOCK, C_QKV_BLOCK), C_QKV_BLOCK ** -0.5),
        'c_w_if': normal(29, (no, 3, C_WIDTH, 2 * C_HEADS), (3 * C_WIDTH) ** -0.5),
        'c_b_if': b_if,
        'c_ln_w': gain(30, (no, C_WIDTH)),
        'c_skip': gain(31, (no, C_WIDTH)),
        'odd_w_out': normal(32, (no, C_WIDTH, D_MODEL), C_WIDTH ** -0.5),
        'ffn_norm': gain(35, (nl, D_MODEL)),
        'ffn_w_gate': normal(36, (nl, D_MODEL, D_FF), D_MODEL ** -0.5),
        'ffn_w_up': normal(37, (nl, D_MODEL, D_FF), D_MODEL ** -0.5),
        'ffn_conv_w': normal(38, (nl, FFN_CONV, D_FF), FFN_CONV ** -0.5),
        'ffn_conv_b': normal(39, (nl, D_FF), 0.01),
        'ffn_w_down': normal(40, (nl, D_FF, D_MODEL), D_FF ** -0.5),
        'final_norm': gain(41, (D_MODEL,)),
    }


def reference(x, even_norm, even_w_in, a_conv_w, a_conv_b, a_w_r, a_b_r, a_w_i, a_b_i, a_lambda,
              b_mu, b_w0, b_w_up, b_a0, b_a_up, b_g_up, b_k_k, b_k_a, b_r_k, b_ln_w, b_ln_b, even_w_out,
              odd_norm, odd_w_in, c_conv_w, c_conv_b, c_w_q, c_w_k, c_w_v, c_w_if, c_b_if, c_ln_w, c_skip, odd_w_out,
              ffn_norm, ffn_w_gate, ffn_w_up, ffn_conv_w, ffn_conv_b, ffn_w_down, final_norm):
    for layer in range(DEPTH):
        if layer % 2 == 0:
            e = layer // 2
            x = rglru_rwkv7_layer(x, even_norm[e], even_w_in[e], a_conv_w[e], a_conv_b[e], a_w_r[e], a_b_r[e],
                                  a_w_i[e], a_b_i[e], a_lambda[e], b_mu[e], b_w0[e], b_w_up[e], b_a0[e], b_a_up[e],
                                  b_g_up[e], b_k_k[e], b_k_a[e], b_r_k[e], b_ln_w[e], b_ln_b[e], even_w_out[e])
        else:
            o = layer // 2
            x = mlstm_layer(x, odd_norm[o], odd_w_in[o], c_conv_w[o], c_conv_b[o], c_w_q[o], c_w_k[o], c_w_v[o],
                            c_w_if[o], c_b_if[o], c_ln_w[o], c_skip[o], odd_w_out[o])
        x = conv_ffn(x, ffn_norm[layer], ffn_w_gate[layer], ffn_w_up[layer], ffn_conv_w[layer],
                     ffn_conv_b[layer], ffn_w_down[layer])
    return rms_norm(x, final_norm)
```

```python
import functools
import math

import jax
import jax.numpy as jnp
from jax import lax
from jax.experimental import pallas as pl
from jax.experimental.pallas import tpu as pltpu

F32 = jnp.float32
BF16 = jnp.bfloat16

EPS = 1e-6
D_MODEL = 2048
A_WIDTH = 1024
A_BLOCKS = 8
A_CONV = 4
LRU_C = 8.0
B_WIDTH = 1024
B_HEAD = 64
B_DECAY_RANK = 64
B_AAA_RANK = 64
B_GATE_RANK = 160
B_SMALL = 384
B_LN_EPS = 64e-5
C_WIDTH = 4096
C_HEADS = 8
C_HEAD = 512
C_QKV_BLOCK = 4
C_CONV = 4
D_FF = 5632
FFN_CONV = 3

SUBLANES = 8
MXU_DIM = 256
VMEM_LIMIT = 56 * 1024 * 1024

RWKV_CHUNK = 64
RWKV_GROUP = 4 * B_HEAD
MLSTM_CHUNK = 256


def _params(sem):
    return pltpu.CompilerParams(dimension_semantics=sem, vmem_limit_bytes=VMEM_LIMIT)


def _dot(a, b):
    return jnp.dot(a, b, preferred_element_type=F32)


def _dot_nt(a, b):
    return lax.dot_general(a, b, (((1,), (1,)), ((), ())), preferred_element_type=F32)


def _dot_tn(a, b):
    return lax.dot_general(a, b, (((0,), (0,)), ((), ())), preferred_element_type=F32)


def _split_bf16(x, terms):
    parts = []
    for _ in range(terms):
        p = x.astype(BF16)
        parts.append(p)
        x = x - p.astype(F32)
    return parts


def _sigmoid(x):
    return 1.0 / (1.0 + jnp.exp(-x))


def _silu(x):
    return x * _sigmoid(x)


def _rms(x, g):
    return x * lax.rsqrt(jnp.mean(x * x, axis=-1, keepdims=True) + EPS) * g


def _shift_rows(x, k, prev8):
    r = pltpu.roll(x, k, 0)
    fix = pltpu.roll(prev8, k, 0)
    row = lax.broadcasted_iota(jnp.int32, (SUBLANES, x.shape[1]), 0)
    head = jnp.where(row < k, fix, r[:SUBLANES])
    return jnp.concatenate([head, r[SUBLANES:]], axis=0)


def _norm_matmul_kernel(x_ref, g_ref, w_ref, o_ref, hn_ref):
    @pl.when(pl.program_id(1) == 0)
    def _():
        hn_ref[...] = _rms(x_ref[...], g_ref[...]).astype(BF16)

    o_ref[...] = _dot(hn_ref[...], w_ref[...]).astype(o_ref.dtype)


def _norm_matmul(x, g, w, tm, tn, name):
    n, d = x.shape
    nout = w.shape[1]
    return pl.pallas_call(
        _norm_matmul_kernel,
        grid=(n // tm, nout // tn),
        in_specs=[pl.BlockSpec((tm, d), lambda i, j: (i, 0)),
                  pl.BlockSpec((1, d), lambda i, j: (0, 0)),
                  pl.BlockSpec((d, tn), lambda i, j: (0, j))],
        out_specs=pl.BlockSpec((tm, tn), lambda i, j: (i, j)),
        out_shape=jax.ShapeDtypeStruct((n, nout), F32),
        scratch_shapes=[pltpu.VMEM((tm, d), BF16)],
        compiler_params=_params(("arbitrary", "arbitrary")),
        name=name,
    )(x, g, w)


def _resid_matmul_kernel(*refs, n_in):
    x_ref = refs[0]
    a_refs = refs[1:1 + n_in]
    w_refs = refs[1 + n_in:1 + 2 * n_in]
    o_ref = refs[1 + 2 * n_in]
    acc = x_ref[...]
    for a_ref, w_ref in zip(a_refs, w_refs):
        acc = acc + _dot(a_ref[...], w_ref[...])
    o_ref[...] = acc


def _resid_matmul(x, acts, ws, tm, tn, name):
    n, d = x.shape
    n_in = len(acts)
    in_specs = [pl.BlockSpec((tm, tn), lambda i, j: (i, j))]
    in_specs += [pl.BlockSpec((tm, a.shape[1]), lambda i, j: (i, 0)) for a in acts]
    in_specs += [pl.BlockSpec((w.shape[0], tn), lambda i, j: (0, j)) for w in ws]
    return pl.pallas_call(
        functools.partial(_resid_matmul_kernel, n_in=n_in),
        grid=(n // tm, d // tn),
        in_specs=in_specs,
        out_specs=pl.BlockSpec((tm, tn), lambda i, j: (i, j)),
        out_shape=jax.ShapeDtypeStruct((n, d), F32),
        compiler_params=_params(("arbitrary", "arbitrary")),
        name=name,
    )(x, *acts, *ws)


def _ffn_kernel(x_ref, g_ref, wg_ref, wu_ref, cw_ref, cb_ref, wd_ref, fg_ref, o_ref, hn_ref, carry_ref,
                *, tiles_per_seq, final_norm):
    i = pl.program_id(0)
    j = pl.program_id(1)
    tm = x_ref.shape[0]

    @pl.when(j == 0)
    def _():
        x = x_ref[...]
        hn_ref[...] = _rms(x, g_ref[...]).astype(BF16)
        o_ref[...] = x

    hn = hn_ref[...]
    gate = _dot(hn, wg_ref[...])
    up = _dot(hn, wu_ref[...])
    seq_start = (i % tiles_per_seq) == 0
    prev8 = jnp.where(seq_start, 0.0, carry_ref[j])
    carry_ref[j] = gate[tm - SUBLANES:]
    cw = cw_ref[...]
    conv = (cb_ref[...] + gate * cw[2:3] + _shift_rows(gate, 1, prev8) * cw[1:2]
            + _shift_rows(gate, 2, prev8) * cw[0:1])
    u = (_silu(conv) * up).astype(BF16)
    o_ref[...] += _dot(u, wd_ref[...])

    if final_norm:
        @pl.when(j == pl.num_programs(1) - 1)
        def _():
            o_ref[...] = _rms(o_ref[...], fg_ref[...])


def _ffn(x, g, wg, wu, cw, cb, wd, fg, seq, tm, tf, final_norm, name):
    n, d = x.shape
    f = wg.shape[1]
    nf = f // tf
    kern = functools.partial(_ffn_kernel, tiles_per_seq=seq // tm, final_norm=final_norm)
    return pl.pallas_call(
        kern,
        grid=(n // tm, nf),
        in_specs=[pl.BlockSpec((tm, d), lambda i, j: (i, 0)),
                  pl.BlockSpec((1, d), lambda i, j: (0, 0)),
                  pl.BlockSpec((d, tf), lambda i, j: (0, j)),
                  pl.BlockSpec((d, tf), lambda i, j: (0, j)),
                  pl.BlockSpec((FFN_CONV, tf), lambda i, j: (0, j)),
                  pl.BlockSpec((1, tf), lambda i, j: (0, j)),
                  pl.BlockSpec((tf, d), lambda i, j: (j, 0)),
                  pl.BlockSpec((1, d), lambda i, j: (0, 0))],
        out_specs=pl.BlockSpec((tm, d), lambda i, j: (i, 0)),
        out_shape=jax.ShapeDtypeStruct((n, d), F32),
        scratch_shapes=[pltpu.VMEM((tm, d), BF16), pltpu.VMEM((nf, SUBLANES, tf), F32)],
        compiler_params=_params(("arbitrary", "arbitrary")),
        name=name,
    )(x, g, wg, wu, cw, cb, wd, fg)


def _rglru_kernel(xa_ref, ga_ref, cw_ref, cb_ref, wr_ref, br_ref, wi_ref, bi_ref, lam_ref, o_ref,
                  prev_ref, h_ref):
    t = pl.program_id(1)
    tt, width = xa_ref.shape

    @pl.when(t == 0)
    def _():
        prev_ref[...] = jnp.zeros_like(prev_ref)
        h_ref[...] = jnp.zeros_like(h_ref)

    x = xa_ref[...]
    prev8 = prev_ref[...]
    prev_ref[...] = x[tt - SUBLANES:]
    cw = cw_ref[...]
    xc = cb_ref[...] + x * cw[3:4]
    for k in range(1, A_CONV):
        xc = xc + _shift_rows(x, k, prev8) * cw[A_CONV - 1 - k:A_CONV - k]

    xb = xc.astype(BF16)
    n_grp = width // MXU_DIM
    r_pre = jnp.concatenate(
        [_dot(xb[:, g * MXU_DIM:(g + 1) * MXU_DIM], wr_ref[g]) for g in range(n_grp)], axis=1)
    i_pre = jnp.concatenate(
        [_dot(xb[:, g * MXU_DIM:(g + 1) * MXU_DIM], wi_ref[g]) for g in range(n_grp)], axis=1)
    r = _sigmoid(r_pre + br_ref[...])
    ig = _sigmoid(i_pre + bi_ref[...])
    neg_lam = -lam_ref[...]
    softplus = jnp.maximum(neg_lam, 0.0) + jnp.log1p(jnp.exp(-jnp.abs(neg_lam)))
    log_a = (-LRU_C) * r * softplus
    a = jnp.exp(log_a)
    u = jnp.sqrt(1.0 - jnp.exp(2.0 * log_a)) * (ig * xc)

    row = lax.broadcasted_iota(jnp.int32, (tt, width), 0)
    s = 1
    while s < tt:
        keep = row >= s
        a_prev = pltpu.roll(a, s, 0)
        u_prev = pltpu.roll(u, s, 0)
        u = jnp.where(keep, a * u_prev + u, u)
        a = jnp.where(keep, a * a_prev, a)
        s *= 2
    h = u + a * h_ref[...]
    h_ref[...] = h[tt - 1:tt]

    ga = ga_ref[...]
    gelu = 0.5 * ga * (1.0 + jnp.tanh(math.sqrt(2.0 / math.pi) * (ga + 0.044715 * (ga * ga * ga))))
    o_ref[...] = (h * gelu).astype(o_ref.dtype)


def _rglru(p, cw, cb, wr, br, wi, bi, lam, bsz, seq, tt):
    n = p.shape[0]
    nt = seq // tt
    w = A_WIDTH
    vec = lambda: pl.BlockSpec((1, w), lambda b, t: (0, 0))
    return pl.pallas_call(
        _rglru_kernel,
        grid=(bsz, nt),
        in_specs=[pl.BlockSpec((tt, w), lambda b, t: (b * nt + t, 0)),
                  pl.BlockSpec((tt, w), lambda b, t: (b * nt + t, 1)),
                  pl.BlockSpec((A_CONV, w), lambda b, t: (0, 0)), vec(),
                  pl.BlockSpec(wr.shape, lambda b, t: (0, 0, 0)), vec(),
                  pl.BlockSpec(wi.shape, lambda b, t: (0, 0, 0)), vec(), vec()],
        out_specs=pl.BlockSpec((tt, w), lambda b, t: (b * nt + t, 0)),
        out_shape=jax.ShapeDtypeStruct((n, w), BF16),
        scratch_shapes=[pltpu.VMEM((SUBLANES, w), F32), pltpu.VMEM((1, w), F32)],
        compiler_params=_params(("arbitrary", "arbitrary")),
        name="rglru",
    )(p, p, cw, cb, wr, br, wi, bi, lam)


def _block_diag_tile(x, mask):
    return jnp.where(mask, jnp.concatenate([x] * (RWKV_GROUP // B_HEAD), axis=0), jnp.zeros((), x.dtype))


def _rwkv_kernel(r_ref, k_ref, v_ref, sm_ref, mur_ref, muk_ref, muv_ref, mus_ref, w0_ref, wup_ref, a0_ref,
                 aup_ref, gup_ref, kkw_ref, kaw_ref, rkw_ref, lnw_ref, lnb_ref, o_ref,
                 s_ref, pr_ref, pk_ref, pv_ref, ps_ref):
    c = pl.program_id(1)
    L = RWKV_CHUNK
    gw = RWKV_GROUP
    n_grp = B_WIDTH // gw

    @pl.when(c == 0)
    def _():
        s_ref[...] = jnp.zeros_like(s_ref)
        pr_ref[...] = jnp.zeros_like(pr_ref)
        pk_ref[...] = jnp.zeros_like(pk_ref)
        pv_ref[...] = jnp.zeros_like(pv_ref)
        ps_ref[...] = jnp.zeros_like(ps_ref)

    def lerp(x_ref, p_ref, mu_ref):
        x = x_ref[...]
        xs = _shift_rows(x, 1, p_ref[...])
        p_ref[...] = x[L - SUBLANES:]
        return x + (xs - x) * mu_ref[...]

    r = lerp(r_ref, pr_ref, mur_ref)
    k = lerp(k_ref, pk_ref, muk_ref)
    v = lerp(v_ref, pv_ref, muv_ref)
    sm = lerp(sm_ref, ps_ref, mus_ref)

    z = w0_ref[...] + _dot(jnp.tanh(sm).astype(BF16), wup_ref[...])
    log_w = (-math.exp(-0.5)) * _sigmoid(z)
    a = _sigmoid(a0_ref[...] + _dot(sm.astype(BF16), aup_ref[...]))
    g = _dot(_sigmoid(sm).astype(BF16), gup_ref[...])

    row_g = lax.broadcasted_iota(jnp.int32, (gw, gw), 0)
    col_g = lax.broadcasted_iota(jnp.int32, (gw, gw), 1)
    head_mask = (row_g // B_HEAD) == (col_g // B_HEAD)
    ones_bd = jnp.where(head_mask, 1.0, 0.0).astype(BF16)

    def head_sum(x):
        outs = []
        for gi in range(n_grp):
            hi, lo = _split_bf16(x[:, gi * gw:(gi + 1) * gw], 2)
            outs.append(_dot(hi, ones_bd) + _dot(lo, ones_bd))
        return jnp.concatenate(outs, axis=1)

    kk = k * kkw_ref[...]
    kk = kk * lax.rsqrt(jnp.maximum(head_sum(kk * kk), 1e-12))
    k2 = k * (1.0 + (a - 1.0) * kaw_ref[...])

    row_l = lax.broadcasted_iota(jnp.int32, (L, L), 0)
    col_l = lax.broadcasted_iota(jnp.int32, (L, L), 1)
    tril = jnp.where(row_l >= col_l, 1.0, 0.0).astype(BF16)
    cum = sum(_dot(tril, part) for part in _split_bf16(log_w, 3))
    p_in = jnp.exp(cum)
    p_prev = jnp.exp(cum - log_w)
    p_inv = jnp.exp(-cum)
    p_last = p_in[L - 1:L]

    a_bar = (-kk) * p_prev
    r_bar = r * p_in
    b_bar = kk * a * p_inv
    k_bar = k2 * p_inv

    row_p = lax.broadcasted_iota(jnp.int32, (L, gw), 0)
    src_p = lax.broadcasted_iota(jnp.int32, (L, gw), 1) % B_HEAD
    strict_lower = row_p > src_p
    lower = row_p >= src_p

    ys = []
    for gi in range(n_grp):
        sl = slice(gi * gw, (gi + 1) * gw)
        ar = jnp.concatenate([a_bar[:, sl], r_bar[:, sl]], axis=0).astype(BF16)
        bd = lambda x: _block_diag_tile(x.astype(BF16), head_mask)
        m_b = _dot_nt(ar, bd(b_bar[:, sl]))
        m_k = _dot_nt(ar, bd(k_bar[:, sl]))
        a_ab = jnp.where(strict_lower, m_b[:L], 0.0)
        a_rb = jnp.where(lower, m_b[L:], 0.0)
        a_ak = jnp.where(strict_lower, m_k[:L], 0.0)
        a_rk = jnp.where(lower, m_k[L:], 0.0)
        s_bd = s_ref[gi]
        ah = _dot_nt(ar, s_bd.astype(BF16))
        v_bd = bd(v[:, sl])
        u = ah[:L] + _dot(a_ak.astype(BF16), v_bd)
        x_pow = a_ab
        n_steps = int(math.log2(L))
        for step in range(n_steps):
            xb = x_pow.astype(BF16)
            u = u + _dot(xb, bd(u))
            if step + 1 < n_steps:
                x_pow = _dot(xb, bd(x_pow))
        u_bd = bd(u)
        ys.append(ah[L:] + _dot(a_rb.astype(BF16), u_bd) + _dot(a_rk.astype(BF16), v_bd))
        pl_g = p_last[:, sl]
        uv = jnp.concatenate([u, v[:, sl]], axis=0).astype(BF16)
        bk = jnp.concatenate([b_bar[:, sl] * pl_g, k_bar[:, sl] * pl_g], axis=0).astype(BF16)
        ds = _dot_tn(uv, bk)
        s_ref[gi] = s_bd * pl_g + jnp.where(head_mask, ds, 0.0)

    y = jnp.concatenate(ys, axis=1)
    inv_n = 1.0 / B_HEAD
    yc = y - head_sum(y) * inv_n
    var = head_sum(yc * yc) * inv_n
    yn = yc * lax.rsqrt(var + B_LN_EPS) * lnw_ref[...] + lnb_ref[...]
    bonus = head_sum(r * k2 * rkw_ref[...]) * v
    o_ref[...] = ((yn + bonus) * g).astype(o_ref.dtype)


def _rwkv(p, sm, mur, muk, muv, mus, w0, wup, a0, aup, gup, kkw, kaw, rkw, lnw, lnb, bsz, seq):
    n = p.shape[0]
    L = RWKV_CHUNK
    nc = seq // L
    w = B_WIDTH
    vec = lambda: pl.BlockSpec((1, w), lambda b, c: (0, 0))
    mat = lambda: pl.BlockSpec((B_SMALL, w), lambda b, c: (0, 0))
    return pl.pallas_call(
        _rwkv_kernel,
        grid=(bsz, nc),
        in_specs=[pl.BlockSpec((L, w), lambda b, c: (b * nc + c, 2)),
                  pl.BlockSpec((L, w), lambda b, c: (b * nc + c, 3)),
                  pl.BlockSpec((L, w), lambda b, c: (b * nc + c, 4)),
                  pl.BlockSpec((L, B_SMALL), lambda b, c: (b * nc + c, 0)),
                  vec(), vec(), vec(), pl.BlockSpec((1, B_SMALL), lambda b, c: (0, 0)),
                  vec(), mat(), vec(), mat(), mat(), vec(), vec(), vec(), vec(), vec()],
        out_specs=pl.BlockSpec((L, w), lambda b, c: (b * nc + c, 0)),
        out_shape=jax.ShapeDtypeStruct((n, w), BF16),
        scratch_shapes=[pltpu.VMEM((w // RWKV_GROUP, RWKV_GROUP, RWKV_GROUP), F32),
                        pltpu.VMEM((SUBLANES, w), F32), pltpu.VMEM((SUBLANES, w), F32),
                        pltpu.VMEM((SUBLANES, w), F32), pltpu.VMEM((SUBLANES, B_SMALL), F32)],
        compiler_params=_params(("arbitrary", "arbitrary")),
        name="rwkv7",
    )(p, p, p, sm, mur, muk, muv, mus, w0, wup, a0, aup, gup, kkw, kaw, rkw, lnw, lnb)


def _mlstm_prep_kernel(xm_ref, cw_ref, cb_ref, wq_ref, wk_ref, wv_ref, wif_ref, bif_ref,
                       q_ref, k_ref, v_ref, xc_ref, gates_ref, prev_ref):
    t = pl.program_id(1)
    cblk = pl.program_id(2)
    tt, cb = xm_ref.shape

    xm = xm_ref[...]
    prev8 = jnp.where(t == 0, 0.0, prev_ref[cblk])
    prev_ref[cblk] = xm[tt - SUBLANES:]
    cw = cw_ref[...]
    conv = cb_ref[...] + xm * cw[C_CONV - 1:C_CONV]
    for kk in range(1, C_CONV):
        conv = conv + _shift_rows(xm, kk, prev8) * cw[C_CONV - 1 - kk:C_CONV - kk]
    xc = _silu(conv)
    xc_ref[...] = xc

    xcb = xc.astype(BF16)
    xmb = xm.astype(BF16)
    n_grp = cb // MXU_DIM

    def blockdiag(xb, w_ref):
        return jnp.concatenate(
            [_dot(xb[:, g * MXU_DIM:(g + 1) * MXU_DIM], w_ref[g]) for g in range(n_grp)], axis=1)

    q = blockdiag(xcb, wq_ref)
    k = blockdiag(xcb, wk_ref)
    v = blockdiag(xmb, wv_ref)
    qb = q.astype(BF16)
    kb = k.astype(BF16)
    vb = v.astype(BF16)
    q_ref[...] = qb
    k_ref[...] = (k * (C_HEAD ** -0.5)).astype(BF16)
    v_ref[...] = vb

    @pl.when(cblk == 0)
    def _():
        gates_ref[...] = jnp.broadcast_to(bif_ref[...], gates_ref.shape)

    gates_ref[...] += _dot(qb, wif_ref[0]) + _dot(kb, wif_ref[1]) + _dot(vb, wif_ref[2])


def _mlstm_prep(pm, cw, cb, wq, wk, wv, wif, bif, bsz, seq, tt, cblk):
    n = pm.shape[0]
    nt = seq // tt
    ncb = C_WIDTH // cblk
    gpb = cblk // MXU_DIM
    row = lambda b, t, c: (b * nt + t, c)
    wspec = lambda: pl.BlockSpec((gpb, MXU_DIM, MXU_DIM), lambda b, t, c: (c, 0, 0))
    act = lambda dt: jax.ShapeDtypeStruct((n, C_WIDTH), dt)
    return pl.pallas_call(
        _mlstm_prep_kernel,
        grid=(bsz, nt, ncb),
        in_specs=[pl.BlockSpec((tt, cblk), row),
                  pl.BlockSpec((C_CONV, cblk), lambda b, t, c: (0, c)),
                  pl.BlockSpec((1, cblk), lambda b, t, c: (0, c)),
                  wspec(), wspec(), wspec(),
                  pl.BlockSpec((3, cblk, 2 * C_HEADS), lambda b, t, c: (0, c, 0)),
                  pl.BlockSpec((1, 2 * C_HEADS), lambda b, t, c: (0, 0))],
        out_specs=[pl.BlockSpec((tt, cblk), row), pl.BlockSpec((tt, cblk), row),
                   pl.BlockSpec((tt, cblk), row), pl.BlockSpec((tt, cblk), row),
                   pl.BlockSpec((tt, 2 * C_HEADS), lambda b, t, c: (b * nt + t, 0))],
        out_shape=[act(BF16), act(BF16), act(BF16), act(F32),
                   jax.ShapeDtypeStruct((n, 2 * C_HEADS), F32)],
        scratch_shapes=[pltpu.VMEM((ncb, SUBLANES, cblk), F32)],
        compiler_params=_params(("arbitrary", "arbitrary", "arbitrary")),
        name="mlstm_prep",
    )(pm, cw, cb, wq, wk, wv, wif, bif)


def _mlstm_kernel(q_ref, k_ref, v_ref, xc_ref, z_ref, ic_ref, fc_ref, ir_ref, fr_ref, lnw_ref, skip_ref,
                  o_ref, ct_ref, n_ref, m_ref):
    c = pl.program_id(2)
    L = q_ref.shape[0]

    @pl.when(c == 0)
    def _():
        ct_ref[...] = jnp.zeros_like(ct_ref)
        n_ref[...] = jnp.zeros_like(n_ref)
        m_ref[...] = jnp.zeros_like(m_ref)

    def log_sigmoid(x):
        return jnp.minimum(x, 0.0) - jnp.log1p(jnp.exp(-jnp.abs(x)))

    q = q_ref[...]
    k = k_ref[...]
    v = v_ref[...]
    li_col = ic_ref[...]
    li_row = ir_ref[...]
    lf_col = log_sigmoid(fc_ref[...])
    lf_row = log_sigmoid(fr_ref[...])
    m_prev = m_ref[...]

    row = lax.broadcasted_iota(jnp.int32, (L, L), 0)
    col = lax.broadcasted_iota(jnp.int32, (L, L), 1)
    causal = row >= col
    tril = jnp.where(causal, 1.0, 0.0).astype(BF16)
    triu = jnp.where(row <= col, 1.0, 0.0).astype(BF16)
    b_col = sum(_dot(tril, jnp.broadcast_to(part, (L, 128))) for part in _split_bf16(lf_col, 3))[:, :1]
    b_row = sum(_dot(jnp.broadcast_to(part, (SUBLANES, L)), triu)
                for part in _split_bf16(lf_row, 3))[:1]
    b_last = b_col[L - 1:L]

    d_log = jnp.where(causal, b_col - b_row + li_row, -jnp.inf)
    inter = b_col + m_prev
    m_t = jnp.maximum(inter, jnp.max(d_log, axis=-1, keepdims=True))
    s = _dot_nt(q, k) * jnp.exp(d_log - m_t)
    sc = jnp.exp(inter - m_t)
    num = _dot(s.astype(BF16), v) + sc * _dot(q, ct_ref[...].astype(BF16))
    qn = jnp.sum(q.astype(F32) * n_ref[...], axis=-1, keepdims=True)
    den = jnp.sum(s, axis=-1, keepdims=True) + sc * qn
    h = num / jnp.maximum(jnp.abs(den), jnp.exp(-m_t))

    g_log = b_last - b_col + li_col
    m_new = jnp.maximum(b_last + m_prev, jnp.max(g_log, axis=0, keepdims=True))
    e = jnp.exp(g_log - m_new)
    decay = jnp.exp(b_last + m_prev - m_new)
    ke = k.astype(F32) * e
    ct_ref[...] = decay * ct_ref[...] + _dot_tn(ke.astype(BF16), v)
    n_ref[...] = decay * n_ref[...] + jnp.sum(ke, axis=0, keepdims=True)
    m_ref[...] = m_new

    hc = h - jnp.mean(h, axis=-1, keepdims=True)
    hn = hc * lax.rsqrt(jnp.mean(hc * hc, axis=-1, keepdims=True) + EPS) * lnw_ref[...]
    hs = hn + skip_ref[...] * xc_ref[...]
    o_ref[...] = (hs * _silu(z_ref[...])).astype(o_ref.dtype)


def _mlstm(q, k, v, xc, pm, i_col, f_col, i_row, f_row, lnw, skip, bsz, seq):
    n = q.shape[0]
    L = MLSTM_CHUNK
    nc = seq // L
    hd = C_HEAD
    blk = lambda: pl.BlockSpec((L, hd), lambda b, h, c: (b * nc + c, h))
    colspec = lambda: pl.BlockSpec((None, None, L, 1), lambda b, h, c: (b, h, c, 0))
    rowspec = lambda: pl.BlockSpec((None, None, 1, L), lambda b, h, c: (b, h, 0, c))
    vec = lambda: pl.BlockSpec((1, hd), lambda b, h, c: (0, h))
    return pl.pallas_call(
        _mlstm_kernel,
        grid=(bsz, C_HEADS, nc),
        in_specs=[blk(), blk(), blk(), blk(),
                  pl.BlockSpec((L, hd), lambda b, h, c: (b * nc + c, C_HEADS + h)),
                  colspec(), colspec(), rowspec(), rowspec(), vec(), vec()],
        out_specs=blk(),
        out_shape=jax.ShapeDtypeStruct((n, C_WIDTH), BF16),
        scratch_shapes=[pltpu.VMEM((hd, hd), F32), pltpu.VMEM((1, hd), F32), pltpu.VMEM((1, 1), F32)],
        compiler_params=_params(("arbitrary", "arbitrary", "arbitrary")),
        name="mlstm",
    )(q, k, v, xc, pm, i_col, f_col, i_row, f_row, lnw, skip)


def _pack_block_diag(w, tile):
    g, bs, _ = w.shape
    per = tile // bs
    w = w.reshape(g // per, per, bs, bs)
    eye = jnp.eye(per, dtype=w.dtype)
    dense = jnp.einsum("npij,pq->npiqj", w, eye)
    return dense.reshape(g // per, tile, tile)


def _row(v):
    return v.reshape(1, -1)


def _even_layer(x, bsz, seq, norm, w_in, a_conv_w, a_conv_b, a_w_r, a_b_r, a_w_i, a_b_i, a_lambda,
                b_mu, b_w0, b_w_up, b_a0, b_a_up, b_g_up, b_k_k, b_k_a, b_r_k, b_ln_w, b_ln_b, w_out):
    main_w = 2 * A_WIDTH + 3 * B_WIDTH
    n_small = B_DECAY_RANK + B_AAA_RANK + B_GATE_RANK
    pad = B_SMALL - n_small
    w_main = w_in[:, :main_w].astype(BF16)
    w_small = jnp.pad(w_in[:, main_w:], ((0, 0), (0, pad))).astype(BF16)
    g = _row(norm)
    p = _norm_matmul(x, g, w_main, 1024, 512, "even_in_main")
    sm = _norm_matmul(x, g, w_small, 1024, B_SMALL, "even_in_small")

    ya = _rglru(p, a_conv_w, _row(a_conv_b),
                _pack_block_diag(a_w_r, MXU_DIM).astype(BF16), _row(a_b_r),
                _pack_block_diag(a_w_i, MXU_DIM).astype(BF16), _row(a_b_i), _row(a_lambda),
                bsz, seq, 256)

    mu = b_mu
    mur, muk, muv = (_row(mu[i * B_WIDTH:(i + 1) * B_WIDTH]) for i in range(3))
    mus = _row(jnp.pad(mu[3 * B_WIDTH:], (0, pad)))

    def rows_at(w, start):
        out = jnp.zeros((B_SMALL, B_WIDTH), F32)
        return lax.dynamic_update_slice(out, w, (start, 0)).astype(BF16)

    wup = rows_at(b_w_up, 0)
    aup = rows_at(b_a_up, B_DECAY_RANK)
    gup = rows_at(b_g_up, B_DECAY_RANK + B_AAA_RANK)
    yb = _rwkv(p, sm, mur, muk, muv, mus, _row(b_w0), wup, _row(b_a0), aup, gup, _row(b_k_k), _row(b_k_a),
               _row(b_r_k), _row(b_ln_w), _row(b_ln_b), bsz, seq)

    wo = w_out.astype(BF16)
    return _resid_matmul(x, [ya, yb], [wo[:A_WIDTH], wo[A_WIDTH:]], 1024, 512, "even_out")


def _odd_layer(x, bsz, seq, norm, w_in, conv_w, conv_b, w_q, w_k, w_v, w_if, b_if, ln_w, skip, w_out):
    pm = _norm_matmul(x, _row(norm), w_in.astype(BF16), 1024, 512, "odd_in")
    q, k, v, xc, gates = _mlstm_prep(
        pm, conv_w, _row(conv_b),
        _pack_block_diag(w_q, MXU_DIM).astype(BF16), _pack_block_diag(w_k, MXU_DIM).astype(BF16),
        _pack_block_diag(w_v, MXU_DIM).astype(BF16), w_if.astype(BF16), _row(b_if), bsz, seq, 512, 512)
    gates = gates.reshape(bsz, seq, 2, C_HEADS)
    gt = jnp.transpose(gates, (2, 0, 3, 1))
    i_row, f_row = gt[0][:, :, None, :], gt[1][:, :, None, :]
    i_col, f_col = gt[0][..., None], gt[1][..., None]
    hs = _mlstm(q, k, v, xc, pm, i_col, f_col, i_row, f_row, _row(ln_w), _row(skip), bsz, seq)
    return _resid_matmul(x, [hs], [w_out.astype(BF16)], 1024, 512, "odd_out")


def kernel(x, even_norm, even_w_in, a_conv_w, a_conv_b, a_w_r, a_b_r, a_w_i, a_b_i, a_lambda, b_mu, b_w0, b_w_up, b_a0, b_a_up, b_g_up, b_k_k, b_k_a, b_r_k, b_ln_w, b_ln_b, even_w_out, odd_norm, odd_w_in, c_conv_w, c_conv_b, c_w_q, c_w_k, c_w_v, c_w_if, c_b_if, c_ln_w, c_skip, odd_w_out, ffn_norm, ffn_w_gate, ffn_w_up, ffn_conv_w, ffn_conv_b, ffn_w_down, final_norm):
    bsz, seq, d = x.shape
    depth = ffn_norm.shape[0]
    h = x.reshape(bsz * seq, d)
    for layer in range(depth):
        if layer % 2 == 0:
            e = layer // 2
            h = _even_layer(h, bsz, seq, even_norm[e], even_w_in[e], a_conv_w[e], a_conv_b[e], a_w_r[e],
                            a_b_r[e], a_w_i[e], a_b_i[e], a_lambda[e], b_mu[e], b_w0[e], b_w_up[e], b_a0[e],
                            b_a_up[e], b_g_up[e], b_k_k[e], b_k_a[e], b_r_k[e].reshape(-1), b_ln_w[e],
                            b_ln_b[e], even_w_out[e])
        else:
            o = layer // 2
            h = _odd_layer(h, bsz, seq, odd_norm[o], odd_w_in[o], c_conv_w[o], c_conv_b[o], c_w_q[o], c_w_k[o],
                           c_w_v[o], c_w_if[o], c_b_if[o], c_ln_w[o], c_skip[o], odd_w_out[o])
        h = _ffn(h, _row(ffn_norm[layer]), ffn_w_gate[layer].astype(BF16), ffn_w_up[layer].astype(BF16),
                 ffn_conv_w[layer], _row(ffn_conv_b[layer]), ffn_w_down[layer].astype(BF16),
                 _row(final_norm), seq, 512, 512, layer == depth - 1, "ffn%d" % layer)
    return h.reshape(bsz, seq, d)
```

```python
import functools
import math

import jax
import jax.numpy as jnp
from jax import lax
from jax.experimental import pallas as pl
from jax.experimental.pallas import tpu as pltpu

F32 = jnp.float32
BF16 = jnp.bfloat16

EPS = 1e-6
D_MODEL = 2048
A_WIDTH = 1024
A_BLOCKS = 8
A_CONV = 4
LRU_C = 8.0
B_WIDTH = 1024
B_HEAD = 64
B_DECAY_RANK = 64
B_AAA_RANK = 64
B_GATE_RANK = 160
B_SMALL = 512
B_SMALL_K = 384
B_LN_EPS = 64e-5
C_WIDTH = 4096
C_HEADS = 8
C_HEAD = 512
C_QKV_BLOCK = 4
C_CONV = 4
D_FF = 5632
FFN_CONV = 3

SUBLANES = 8
LANES = 128
MXU_DIM = 256
VMEM_LIMIT = 56 * 1024 * 1024

RWKV_CHUNK = 64
RWKV_GROUP = 4 * B_HEAD
RWKV_ROWS = 256
MLSTM_CHUNK = 256
PROJ_TM, PROJ_TN = 1024, 512
FFN_TM, FFN_TF = 512, 512
RGLRU_ROWS = 256
MLSTM_IN_COLS, MLSTM_IN_SUB = 512, 256


def _params(sem):
    return pltpu.CompilerParams(dimension_semantics=sem, vmem_limit_bytes=VMEM_LIMIT)


def _dot(a, b):
    return jnp.dot(a, b, preferred_element_type=F32)


def _dot_nt(a, b):
    return lax.dot_general(a, b, (((1,), (1,)), ((), ())), preferred_element_type=F32)


def _dot_tn(a, b):
    return lax.dot_general(a, b, (((0,), (0,)), ((), ())), preferred_element_type=F32)


def _split_bf16(x, terms):
    parts = []
    for _ in range(terms):
        p = x.astype(BF16)
        parts.append(p)
        x = x - p.astype(F32)
    return parts


def _sigmoid(x):
    return 1.0 / (1.0 + jnp.exp(-x))


def _silu(x):
    return x * _sigmoid(x)


def _rms(x, g):
    return x * lax.rsqrt(jnp.mean(x * x, axis=-1, keepdims=True) + EPS) * g


def _shift_rows(x, k, prev8):
    r = pltpu.roll(x, k, 0)
    fix = pltpu.roll(prev8, k, 0)
    row = lax.broadcasted_iota(jnp.int32, (SUBLANES, x.shape[1]), 0)
    head = jnp.where(row < k, fix, r[:SUBLANES])
    return jnp.concatenate([head, r[SUBLANES:]], axis=0)


def _norm_matmul_kernel(x_ref, g_ref, w_ref, o_ref, hn_ref):
    @pl.when(pl.program_id(1) == 0)
    def _():
        hn_ref[...] = _rms(x_ref[...], g_ref[...]).astype(BF16)

    o_ref[...] = _dot(hn_ref[...], w_ref[...]).astype(o_ref.dtype)


def _norm_matmul(x, g, w, tm, tn, name, col_start=0, nout=None):
    n, d = x.shape
    nout = w.shape[1] - col_start if nout is None else nout
    off = col_start // tn
    assert col_start % tn == 0 and nout % tn == 0
    return pl.pallas_call(
        _norm_matmul_kernel,
        grid=(n // tm, nout // tn),
        in_specs=[pl.BlockSpec((tm, d), lambda i, j: (i, 0)),
                  pl.BlockSpec((1, d), lambda i, j: (0, 0)),
                  pl.BlockSpec((d, tn), lambda i, j: (0, j + off))],
        out_specs=pl.BlockSpec((tm, tn), lambda i, j: (i, j)),
        out_shape=jax.ShapeDtypeStruct((n, nout), F32),
        scratch_shapes=[pltpu.VMEM((tm, d), BF16)],
        compiler_params=_params(("arbitrary", "arbitrary")),
        name=name,
    )(x, g, w)


def _resid_matmul_kernel(*refs, n_in):
    x_ref = refs[0]
    a_refs = refs[1:1 + n_in]
    w_refs = refs[1 + n_in:1 + 2 * n_in]
    o_ref = refs[1 + 2 * n_in]
    acc = x_ref[...]
    for a_ref, w_ref in zip(a_refs, w_refs):
        acc = acc + _dot(a_ref[...], w_ref[...])
    o_ref[...] = acc


def _resid_matmul(x, acts, ws, tm, tn, name):
    n, d = x.shape
    n_in = len(acts)
    in_specs = [pl.BlockSpec((tm, tn), lambda i, j: (i, j))]
    in_specs += [pl.BlockSpec((tm, a.shape[1]), lambda i, j: (i, 0)) for a in acts]
    in_specs += [pl.BlockSpec((w.shape[0], tn), lambda i, j: (0, j)) for w in ws]
    return pl.pallas_call(
        functools.partial(_resid_matmul_kernel, n_in=n_in),
        grid=(n // tm, d // tn),
        in_specs=in_specs,
        out_specs=pl.BlockSpec((tm, tn), lambda i, j: (i, j)),
        out_shape=jax.ShapeDtypeStruct((n, d), F32),
        compiler_params=_params(("arbitrary", "arbitrary")),
        name=name,
    )(x, *acts, *ws)


def _ffn_kernel(x_ref, g_ref, wg_ref, wu_ref, cw_ref, cb_ref, wd_ref, fg_ref, o_ref, hn_ref, carry_ref,
                *, tiles_per_seq, final_norm):
    i = pl.program_id(0)
    j = pl.program_id(1)
    tm = x_ref.shape[0]

    @pl.when(j == 0)
    def _():
        x = x_ref[...]
        hn_ref[...] = _rms(x, g_ref[...]).astype(BF16)
        o_ref[...] = x

    hn = hn_ref[...]
    gate = _dot(hn, wg_ref[...])
    up = _dot(hn, wu_ref[...])
    seq_start = (i % tiles_per_seq) == 0
    prev8 = jnp.where(seq_start, 0.0, carry_ref[j])
    carry_ref[j] = gate[tm - SUBLANES:]
    cw = cw_ref[...]
    conv = (cb_ref[...] + gate * cw[2:3] + _shift_rows(gate, 1, prev8) * cw[1:2]
            + _shift_rows(gate, 2, prev8) * cw[0:1])
    u = (_silu(conv) * up).astype(BF16)
    o_ref[...] += _dot(u, wd_ref[...])

    if final_norm:
        @pl.when(j == pl.num_programs(1) - 1)
        def _():
            o_ref[...] = _rms(o_ref[...], fg_ref[...])


def _ffn(x, g, wg, wu, cw, cb, wd, fg, seq, tm, tf, final_norm, name):
    n, d = x.shape
    f = wg.shape[1]
    nf = f // tf
    kern = functools.partial(_ffn_kernel, tiles_per_seq=seq // tm, final_norm=final_norm)
    return pl.pallas_call(
        kern,
        grid=(n // tm, nf),
        in_specs=[pl.BlockSpec((tm, d), lambda i, j: (i, 0)),
                  pl.BlockSpec((1, d), lambda i, j: (0, 0)),
                  pl.BlockSpec((d, tf), lambda i, j: (0, j)),
                  pl.BlockSpec((d, tf), lambda i, j: (0, j)),
                  pl.BlockSpec((FFN_CONV, tf), lambda i, j: (0, j)),
                  pl.BlockSpec((1, tf), lambda i, j: (0, j)),
                  pl.BlockSpec((tf, d), lambda i, j: (j, 0)),
                  pl.BlockSpec((1, d), lambda i, j: (0, 0))],
        out_specs=pl.BlockSpec((tm, d), lambda i, j: (i, 0)),
        out_shape=jax.ShapeDtypeStruct((n, d), F32),
        scratch_shapes=[pltpu.VMEM((tm, d), BF16), pltpu.VMEM((nf, SUBLANES, tf), F32)],
        compiler_params=_params(("arbitrary", "arbitrary")),
        name=name,
    )(x, g, wg, wu, cw, cb, wd, fg)


def _rglru_kernel(xa_ref, ga_ref, cw_ref, cb_ref, wr_ref, br_ref, wi_ref, bi_ref, lam_ref, o_ref,
                  prev_ref, h_ref):
    t = pl.program_id(1)
    tt, width = xa_ref.shape

    @pl.when(t == 0)
    def _():
        prev_ref[...] = jnp.zeros_like(prev_ref)
        h_ref[...] = jnp.zeros_like(h_ref)

    x = xa_ref[...]
    prev8 = prev_ref[...]
    prev_ref[...] = x[tt - SUBLANES:]
    cw = cw_ref[...]
    xc = cb_ref[...] + x * cw[3:4]
    for k in range(1, A_CONV):
        xc = xc + _shift_rows(x, k, prev8) * cw[A_CONV - 1 - k:A_CONV - k]

    xb = xc.astype(BF16)
    n_grp = width // MXU_DIM
    r_pre = jnp.concatenate(
        [_dot(xb[:, g * MXU_DIM:(g + 1) * MXU_DIM], wr_ref[g]) for g in range(n_grp)], axis=1)
    i_pre = jnp.concatenate(
        [_dot(xb[:, g * MXU_DIM:(g + 1) * MXU_DIM], wi_ref[g]) for g in range(n_grp)], axis=1)
    r = _sigmoid(r_pre + br_ref[...])
    ig = _sigmoid(i_pre + bi_ref[...])
    neg_lam = -lam_ref[...]
    softplus = jnp.maximum(neg_lam, 0.0) + jnp.log1p(jnp.exp(-jnp.abs(neg_lam)))
    log_a = (-LRU_C) * r * softplus
    a = jnp.exp(log_a)
    u = jnp.sqrt(1.0 - jnp.exp(2.0 * log_a)) * (ig * xc)

    n_sub = tt // SUBLANES
    a3 = a.reshape(n_sub, SUBLANES, width)
    u3 = u.reshape(n_sub, SUBLANES, width)
    sub = lax.broadcasted_iota(jnp.int32, (n_sub, SUBLANES, width), 1)
    s = 1
    while s < SUBLANES:
        keep = sub >= s
        a_prev = pltpu.roll(a3, s, 1)
        u_prev = pltpu.roll(u3, s, 1)
        u3 = jnp.where(keep, a3 * u_prev + u3, u3)
        a3 = jnp.where(keep, a3 * a_prev, a3)
        s *= 2
    carry = h_ref[...]
    groups = []
    for gi in range(n_sub):
        hg = u3[gi] + a3[gi] * carry
        groups.append(hg)
        carry = hg[SUBLANES - 1:]
    h = jnp.concatenate(groups, axis=0)
    h_ref[...] = carry

    ga = ga_ref[...]
    gelu = 0.5 * ga * (1.0 + jnp.tanh(math.sqrt(2.0 / math.pi) * (ga + 0.044715 * (ga * ga * ga))))
    o_ref[...] = (h * gelu).astype(o_ref.dtype)


def _rglru(p, cw, cb, wr, br, wi, bi, lam, bsz, seq, tt):
    n = p.shape[0]
    nt = seq // tt
    w = A_WIDTH
    vec = lambda: pl.BlockSpec((1, w), lambda b, t: (0, 0))
    return pl.pallas_call(
        _rglru_kernel,
        grid=(bsz, nt),
        in_specs=[pl.BlockSpec((tt, w), lambda b, t: (b * nt + t, 0)),
                  pl.BlockSpec((tt, w), lambda b, t: (b * nt + t, 1)),
                  pl.BlockSpec((A_CONV, w), lambda b, t: (0, 0)), vec(),
                  pl.BlockSpec(wr.shape, lambda b, t: (0, 0, 0)), vec(),
                  pl.BlockSpec(wi.shape, lambda b, t: (0, 0, 0)), vec(), vec()],
        out_specs=pl.BlockSpec((tt, w), lambda b, t: (b * nt + t, 0)),
        out_shape=jax.ShapeDtypeStruct((n, w), BF16),
        scratch_shapes=[pltpu.VMEM((SUBLANES, w), F32), pltpu.VMEM((1, w), F32)],
        compiler_params=_params(("arbitrary", "arbitrary")),
        name="rglru",
    )(p, p, cw, cb, wr, br, wi, bi, lam)


def _rwkv_kernel(r_ref, k_ref, v_ref, sm_ref, mur_ref, muk_ref, muv_ref, mus_ref, w0_ref, wup_ref, a0_ref,
                 aup_ref, gup_ref, kkw_ref, kaw_ref, rkw_ref, lnw_ref, lnb_ref, o_ref,
                 s_ref, pr_ref, pk_ref, pv_ref, ps_ref):
    tt = r_ref.shape[0]
    L = RWKV_CHUNK
    gw = RWKV_GROUP
    n_grp = B_WIDTH // gw
    n_chunk = tt // L

    @pl.when(pl.program_id(1) == 0)
    def _():
        s_ref[...] = jnp.zeros_like(s_ref)
        pr_ref[...] = jnp.zeros_like(pr_ref)
        pk_ref[...] = jnp.zeros_like(pk_ref)
        pv_ref[...] = jnp.zeros_like(pv_ref)
        ps_ref[...] = jnp.zeros_like(ps_ref)

    def lerp(x_ref, p_ref, mu_ref):
        x = x_ref[...]
        xs = _shift_rows(x, 1, p_ref[...])
        p_ref[...] = x[tt - SUBLANES:]
        return x + (xs - x) * mu_ref[...]

    r = lerp(r_ref, pr_ref, mur_ref)
    k = lerp(k_ref, pk_ref, muk_ref)
    v = lerp(v_ref, pv_ref, muv_ref)
    sm = lerp(sm_ref, ps_ref, mus_ref)[:, :B_SMALL_K]

    z = w0_ref[...] + _dot(jnp.tanh(sm).astype(BF16), wup_ref[...])
    log_w = (-math.exp(-0.5)) * _sigmoid(z)
    a = _sigmoid(a0_ref[...] + _dot(sm.astype(BF16), aup_ref[...]))
    g = _dot(_sigmoid(sm).astype(BF16), gup_ref[...])

    row_g = lax.broadcasted_iota(jnp.int32, (gw, gw), 0)
    col_g = lax.broadcasted_iota(jnp.int32, (gw, gw), 1)
    same_head = (row_g // B_HEAD) == (col_g // B_HEAD)
    head_mask = jnp.where(same_head, 1.0, 0.0)
    head_mask_bf = head_mask.astype(BF16)

    def head_sum(x):
        xs = jnp.concatenate([x[:, gi * gw:(gi + 1) * gw] for gi in range(n_grp)], axis=0)
        hi, lo = _split_bf16(xs, 2)
        s = _dot(jnp.concatenate([hi, lo], axis=0), head_mask_bf)
        s = s[:n_grp * tt] + s[n_grp * tt:]
        return jnp.concatenate([s[gi * tt:(gi + 1) * tt] for gi in range(n_grp)], axis=1)

    def bd(x):
        return jnp.concatenate([x.astype(BF16)] * (gw // B_HEAD), axis=0) * head_mask_bf

    kk = k * kkw_ref[...]
    kk = kk * lax.rsqrt(jnp.maximum(head_sum(kk * kk), 1e-12))
    k2 = k * (1.0 + (a - 1.0) * kaw_ref[...])

    row_t = lax.broadcasted_iota(jnp.int32, (tt, tt), 0)
    col_t = lax.broadcasted_iota(jnp.int32, (tt, tt), 1)
    tril = jnp.where((row_t >= col_t) & ((row_t // L) == (col_t // L)), 1.0, 0.0).astype(BF16)
    cum = sum(_dot(tril, part) for part in _split_bf16(log_w, 3))
    p_in = jnp.exp(cum)
    p_inv = jnp.exp(-cum)
    a_bar = (-kk) * jnp.exp(cum - log_w)
    r_bar = r * p_in
    b_bar = kk * a * p_inv
    k_bar = k2 * p_inv

    row_p = lax.broadcasted_iota(jnp.int32, (L, gw), 0)
    src_p = lax.broadcasted_iota(jnp.int32, (L, gw), 1) % B_HEAD
    strict_lower = row_p > src_p
    lower = row_p >= src_p
    n_doublings = int(math.log2(L)) - 1

    units = [(c, gi) for c in range(n_chunk) for gi in range(n_grp)]

    def tile(arr, c, gi):
        return arr[c * L:(c + 1) * L, gi * gw:(gi + 1) * gw]

    ar = {u: jnp.concatenate([tile(a_bar, *u), tile(r_bar, *u)], axis=0).astype(BF16) for u in units}
    m_b = {u: _dot_nt(ar[u], bd(tile(b_bar, *u))) for u in units}
    m_k = {u: _dot_nt(ar[u], bd(tile(k_bar, *u))) for u in units}
    x = {u: jnp.where(strict_lower, m_b[u][:L], 0.0) for u in units}
    a_rb = {u: jnp.where(lower, m_b[u][L:], 0.0).astype(BF16) for u in units}
    akrk = {u: jnp.concatenate([jnp.where(strict_lower, m_k[u][:L], 0.0),
                                jnp.where(lower, m_k[u][L:], 0.0)], axis=0).astype(BF16) for u in units}
    cy = {u: _dot(akrk[u], bd(tile(v, *u))) for u in units}
    n_inv = dict(x)
    x_pow = {u: _dot(x[u].astype(BF16), bd(x[u])) for u in units}
    for step in range(n_doublings):
        if step + 1 < n_doublings:
            both = {u: _dot(jnp.concatenate([x_pow[u], n_inv[u]], axis=0).astype(BF16), bd(x_pow[u]))
                    for u in units}
            n_inv = {u: n_inv[u] + x_pow[u] + both[u][L:] for u in units}
            x_pow = {u: both[u][:L] for u in units}
        else:
            n_inv = {u: n_inv[u] + x_pow[u] + _dot(n_inv[u].astype(BF16), bd(x_pow[u])) for u in units}
    n_inv = {u: n_inv[u].astype(BF16) for u in units}

    groups = range(n_grp)
    states = [s_ref[gi] for gi in groups]
    y_rows = []
    for c in range(n_chunk):
        p_last = p_in[(c + 1) * L - 1:(c + 1) * L]
        pl_g = [p_last[:, gi * gw:(gi + 1) * gw] for gi in groups]
        bk = [jnp.concatenate([tile(b_bar, c, gi) * pl_g[gi], tile(k_bar, c, gi) * pl_g[gi]],
                              axis=0).astype(BF16) for gi in groups]
        ah = [_dot_nt(ar[c, gi], states[gi].astype(BF16)) for gi in groups]
        rhs = [ah[gi][:L] + cy[c, gi][:L] for gi in groups]
        u_c = [rhs[gi] + _dot(n_inv[c, gi], bd(rhs[gi])) for gi in groups]
        ds = [_dot_tn(jnp.concatenate([u_c[gi], tile(v, c, gi)], axis=0).astype(BF16), bk[gi])
              for gi in groups]
        states = [states[gi] * pl_g[gi] + head_mask * ds[gi] for gi in groups]
        y_rows.append(jnp.concatenate(
            [ah[gi][L:] + cy[c, gi][L:] + _dot(a_rb[c, gi], bd(u_c[gi])) for gi in groups], axis=1))
    for gi in groups:
        s_ref[gi] = states[gi]

    y = jnp.concatenate(y_rows, axis=0)
    inv_n = 1.0 / B_HEAD
    yc = y - head_sum(y) * inv_n
    var = head_sum(yc * yc) * inv_n
    yn = yc * lax.rsqrt(var + B_LN_EPS) * lnw_ref[...] + lnb_ref[...]
    bonus = head_sum(r * k2 * rkw_ref[...]) * v
    o_ref[...] = ((yn + bonus) * g).astype(o_ref.dtype)


def _rwkv(p, mur, muk, muv, mus, w0, wup, a0, aup, gup, kkw, kaw, rkw, lnw, lnb, bsz, seq):
    n = p.shape[0]
    tt = RWKV_ROWS
    nt = seq // tt
    w = B_WIDTH
    col0 = 2 * A_WIDTH // w
    vec = lambda: pl.BlockSpec((1, w), lambda b, t: (0, 0))
    mat = lambda: pl.BlockSpec((B_SMALL_K, w), lambda b, t: (0, 0))
    return pl.pallas_call(
        _rwkv_kernel,
        grid=(bsz, nt),
        in_specs=[pl.BlockSpec((tt, w), lambda b, t: (b * nt + t, col0)),
                  pl.BlockSpec((tt, w), lambda b, t: (b * nt + t, col0 + 1)),
                  pl.BlockSpec((tt, w), lambda b, t: (b * nt + t, col0 + 2)),
                  pl.BlockSpec((tt, B_SMALL), lambda b, t: (b * nt + t, (2 * A_WIDTH + 3 * w) // B_SMALL)),
                  vec(), vec(), vec(), pl.BlockSpec((1, B_SMALL), lambda b, t: (0, 0)),
                  vec(), mat(), vec(), mat(), mat(), vec(), vec(), vec(), vec(), vec()],
        out_specs=pl.BlockSpec((tt, w), lambda b, t: (b * nt + t, 0)),
        out_shape=jax.ShapeDtypeStruct((n, w), BF16),
        scratch_shapes=[pltpu.VMEM((w // RWKV_GROUP, RWKV_GROUP, RWKV_GROUP), F32),
                        pltpu.VMEM((SUBLANES, w), F32), pltpu.VMEM((SUBLANES, w), F32),
                        pltpu.VMEM((SUBLANES, w), F32), pltpu.VMEM((SUBLANES, B_SMALL), F32)],
        compiler_params=_params(("arbitrary", "arbitrary")),
        name="rwkv7",
    )(p, p, p, p, mur, muk, muv, mus, w0, wup, a0, aup, gup, kkw, kaw, rkw, lnw, lnb)


def _mlstm_in_kernel(x_ref, g_ref, w_ref, cw_ref, cb_ref, wq_ref, wk_ref, wv_ref, wif_ref, bif_ref,
                     q_ref, k_ref, v_ref, xc_ref, gates_ref, hn_ref, prev_ref, *, tiles_per_seq, sub_rows):
    i = pl.program_id(0)
    j = pl.program_id(1)
    tm = x_ref.shape[0]
    cb = w_ref.shape[1]
    n_grp = cb // MXU_DIM

    @pl.when(j == 0)
    def _():
        hn_ref[...] = _rms(x_ref[...], g_ref[...]).astype(BF16)
        gates_ref[...] = jnp.broadcast_to(bif_ref[...], gates_ref.shape)

    def blockdiag(xb, wb_ref):
        return jnp.concatenate(
            [_dot(xb[:, g * MXU_DIM:(g + 1) * MXU_DIM], wb_ref[g]) for g in range(n_grp)], axis=1)

    seq_start = (i % tiles_per_seq) == 0
    prev8 = jnp.where(seq_start, 0.0, prev_ref[j])
    cw = cw_ref[...]
    for s in range(tm // sub_rows):
        rows = pl.ds(s * sub_rows, sub_rows)
        xm = _dot(hn_ref[rows, :], w_ref[...])
        conv = cb_ref[...] + xm * cw[C_CONV - 1:C_CONV]
        for kk in range(1, C_CONV):
            conv = conv + _shift_rows(xm, kk, prev8) * cw[C_CONV - 1 - kk:C_CONV - kk]
        prev8 = xm[sub_rows - SUBLANES:]
        xc = _silu(conv)
        xc_ref[rows, :] = xc
        xcb = xc.astype(BF16)
        q = blockdiag(xcb, wq_ref)
        k = blockdiag(xcb, wk_ref)
        v = blockdiag(xm.astype(BF16), wv_ref)
        qb = q.astype(BF16)
        kb = k.astype(BF16)
        vb = v.astype(BF16)
        q_ref[rows, :] = qb
        k_ref[rows, :] = (k * (C_HEAD ** -0.5)).astype(BF16)
        v_ref[rows, :] = vb
        gates_ref[rows, :] += _dot(qb, wif_ref[0]) + _dot(kb, wif_ref[1]) + _dot(vb, wif_ref[2])
    prev_ref[j] = prev8


def _mlstm_in(x, g, w_in, cw, cb, wq, wk, wv, wif, bif, seq, tm, cblk, sub_rows):
    n, d = x.shape
    ncb = C_WIDTH // cblk
    gpb = cblk // MXU_DIM
    blk = lambda: pl.BlockSpec((tm, cblk), lambda i, j: (i, j))
    wspec = lambda: pl.BlockSpec((gpb, MXU_DIM, MXU_DIM), lambda i, j: (j, 0, 0))
    act = lambda dt: jax.ShapeDtypeStruct((n, C_WIDTH), dt)
    kern = functools.partial(_mlstm_in_kernel, tiles_per_seq=seq // tm, sub_rows=sub_rows)
    return pl.pallas_call(
        kern,
        grid=(n // tm, ncb),
        in_specs=[pl.BlockSpec((tm, d), lambda i, j: (i, 0)),
                  pl.BlockSpec((1, d), lambda i, j: (0, 0)),
                  pl.BlockSpec((d, cblk), lambda i, j: (0, j)),
                  pl.BlockSpec((C_CONV, cblk), lambda i, j: (0, j)),
                  pl.BlockSpec((1, cblk), lambda i, j: (0, j)),
                  wspec(), wspec(), wspec(),
                  pl.BlockSpec((3, cblk, 2 * C_HEADS), lambda i, j: (0, j, 0)),
                  pl.BlockSpec((1, 2 * C_HEADS), lambda i, j: (0, 0))],
        out_specs=[blk(), blk(), blk(), blk(),
                   pl.BlockSpec((tm, 2 * C_HEADS), lambda i, j: (i, 0))],
        out_shape=[act(BF16), act(BF16), act(BF16), act(F32),
                   jax.ShapeDtypeStruct((n, 2 * C_HEADS), F32)],
        scratch_shapes=[pltpu.VMEM((tm, d), BF16), pltpu.VMEM((ncb, SUBLANES, cblk), F32)],
        compiler_params=_params(("arbitrary", "arbitrary")),
        name="mlstm_in",
    )(x, g, w_in, cw, cb, wq, wk, wv, wif, bif)


def _mlstm_gate_kernel(f_ref, b_ref):
    L = f_ref.shape[1]
    f = f_ref[...]
    lf = jnp.minimum(f, 0.0) - jnp.log1p(jnp.exp(-jnp.abs(f)))
    row = lax.broadcasted_iota(jnp.int32, (L, L), 0)
    col = lax.broadcasted_iota(jnp.int32, (L, L), 1)
    triu = jnp.where(row <= col, 1.0, 0.0).astype(BF16)
    b_ref[...] = sum(_dot(part, triu) for part in _split_bf16(lf, 3))


def _mlstm_gates(f_pre, chunk):
    rows, seq = f_pre.shape
    return pl.pallas_call(
        _mlstm_gate_kernel,
        grid=(seq // chunk,),
        in_specs=[pl.BlockSpec((rows, chunk), lambda c: (0, c))],
        out_specs=pl.BlockSpec((rows, chunk), lambda c: (0, c)),
        out_shape=jax.ShapeDtypeStruct((rows, seq), F32),
        compiler_params=_params(("arbitrary",)),
        name="mlstm_gates",
    )(f_pre)


def _mlstm_kernel(q_ref, k_ref, v_ref, xc_ref, z_ref, ic_ref, bc_ref, ir_ref, br_ref, lnw_ref, skip_ref,
                  o_ref, ct_ref, m_ref):
    c = pl.program_id(2)
    L, hd = q_ref.shape

    @pl.when(c == 0)
    def _():
        ct_ref[...] = jnp.zeros_like(ct_ref)
        m_ref[...] = jnp.zeros_like(m_ref)

    q = q_ref[...]
    k = k_ref[...]
    v = v_ref[...]
    li_col = ic_ref[...]
    li_row = ir_ref[...]
    b_col = bc_ref[...]
    b_row = br_ref[...]
    m_prev = m_ref[...]

    row = lax.broadcasted_iota(jnp.int32, (L, L), 0)
    col = lax.broadcasted_iota(jnp.int32, (L, L), 1)
    causal = row >= col
    b_last = b_col[L - 1:L]

    d_log = jnp.where(causal, b_col - b_row + li_row, -jnp.inf)
    inter = b_col + m_prev
    m_t = jnp.maximum(inter, jnp.max(d_log, axis=-1, keepdims=True))
    s = _dot_nt(q, k) * jnp.exp(d_log - m_t)
    sc = jnp.exp(inter - m_t)
    v_aug = jnp.concatenate([v, jnp.ones((L, LANES), BF16)], axis=1)
    num_den = _dot(s.astype(BF16), v_aug) + sc * _dot(q, ct_ref[...].astype(BF16))
    den = num_den[:, hd:]
    inv = 1.0 / jnp.maximum(jnp.abs(den), jnp.exp(-m_t))
    h = num_den[:, :hd] * jnp.concatenate([inv] * (hd // LANES), axis=1)

    g_log = b_last - b_col + li_col
    m_new = jnp.maximum(b_last + m_prev, jnp.max(g_log, axis=0, keepdims=True))
    e = jnp.exp(g_log - m_new)
    decay = jnp.exp(b_last + m_prev - m_new)
    e_b = jnp.broadcast_to(e, (L, LANES)).astype(BF16)
    ke = k * jnp.concatenate([e_b] * (hd // LANES), axis=1)
    ct_ref[...] = decay * ct_ref[...] + _dot_tn(ke, v_aug)
    m_ref[...] = m_new

    hc = h - jnp.mean(h, axis=-1, keepdims=True)
    hn = hc * lax.rsqrt(jnp.mean(hc * hc, axis=-1, keepdims=True) + EPS) * lnw_ref[...]
    hs = hn + skip_ref[...] * xc_ref[...]
    o_ref[...] = (hs * _silu(z_ref[...])).astype(o_ref.dtype)


def _mlstm(q, k, v, xc, z, i_col, b_col, i_row, b_row, lnw, skip, bsz, seq):
    n = q.shape[0]
    L = MLSTM_CHUNK
    nc = seq // L
    hd = C_HEAD
    blk = lambda: pl.BlockSpec((L, hd), lambda b, h, c: (b * nc + c, h))
    colspec = lambda: pl.BlockSpec((None, None, L, 1), lambda b, h, c: (b, h, c, 0))
    rowspec = lambda: pl.BlockSpec((None, None, 1, L), lambda b, h, c: (b, h, 0, c))
    vec = lambda: pl.BlockSpec((1, hd), lambda b, h, c: (0, h))
    return pl.pallas_call(
        _mlstm_kernel,
        grid=(bsz, C_HEADS, nc),
        in_specs=[blk(), blk(), blk(), blk(), blk(),
                  colspec(), colspec(), rowspec(), rowspec(), vec(), vec()],
        out_specs=blk(),
        out_shape=jax.ShapeDtypeStruct((n, C_WIDTH), BF16),
        scratch_shapes=[pltpu.VMEM((hd, hd + LANES), F32), pltpu.VMEM((1, 1), F32)],
        compiler_params=_params(("arbitrary", "arbitrary", "arbitrary")),
        name="mlstm",
    )(q, k, v, xc, z, i_col, b_col, i_row, b_row, lnw, skip)


def _pack_block_diag(w, tile):
    g, bs, _ = w.shape
    per = tile // bs
    w = w.reshape(g // per, per, bs, bs)
    eye = jnp.eye(per, dtype=w.dtype)
    dense = jnp.einsum("npij,pq->npiqj", w, eye)
    return dense.reshape(g // per, tile, tile)


def _row(v):
    return v.reshape(1, -1)


def _even_layer(x, bsz, seq, norm, w_in, a_conv_w, a_conv_b, a_w_r, a_b_r, a_w_i, a_b_i, a_lambda,
                b_mu, b_w0, b_w_up, b_a0, b_a_up, b_g_up, b_k_k, b_k_a, b_r_k, b_ln_w, b_ln_b, w_out):
    main_w = 2 * A_WIDTH + 3 * B_WIDTH
    n_small = B_DECAY_RANK + B_AAA_RANK + B_GATE_RANK
    pad = B_SMALL - n_small
    assert main_w % B_SMALL == 0
    w_all = jnp.pad(w_in, ((0, 0), (0, pad))).astype(BF16)
    p = _norm_matmul(x, _row(norm), w_all, PROJ_TM, PROJ_TN, "even_in")

    ya = _rglru(p, a_conv_w, _row(a_conv_b),
                _pack_block_diag(a_w_r, MXU_DIM).astype(BF16), _row(a_b_r),
                _pack_block_diag(a_w_i, MXU_DIM).astype(BF16), _row(a_b_i), _row(a_lambda),
                bsz, seq, RGLRU_ROWS)

    mur, muk, muv = (_row(b_mu[i * B_WIDTH:(i + 1) * B_WIDTH]) for i in range(3))
    mus = _row(jnp.pad(b_mu[3 * B_WIDTH:], (0, pad)))

    def rows_at(w, start):
        out = jnp.zeros((B_SMALL_K, B_WIDTH), F32)
        return lax.dynamic_update_slice(out, w, (start, 0)).astype(BF16)

    wup = rows_at(b_w_up, 0)
    aup = rows_at(b_a_up, B_DECAY_RANK)
    gup = rows_at(b_g_up, B_DECAY_RANK + B_AAA_RANK)
    yb = _rwkv(p, mur, muk, muv, mus, _row(b_w0), wup, _row(b_a0), aup, gup, _row(b_k_k), _row(b_k_a),
               _row(b_r_k), _row(b_ln_w), _row(b_ln_b), bsz, seq)

    wo = w_out.astype(BF16)
    return _resid_matmul(x, [ya, yb], [wo[:A_WIDTH], wo[A_WIDTH:]], PROJ_TM, PROJ_TN, "even_out")


def _odd_layer(x, bsz, seq, norm, w_in, conv_w, conv_b, w_q, w_k, w_v, w_if, b_if, ln_w, skip, w_out):
    w_in_b = w_in.astype(BF16)
    g = _row(norm)
    z = _norm_matmul(x, g, w_in_b, PROJ_TM, PROJ_TN, "odd_in_z", col_start=C_WIDTH)
    q, k, v, xc, gates = _mlstm_in(
        x, g, w_in_b, conv_w, _row(conv_b),
        _pack_block_diag(w_q, MXU_DIM).astype(BF16), _pack_block_diag(w_k, MXU_DIM).astype(BF16),
        _pack_block_diag(w_v, MXU_DIM).astype(BF16), w_if.astype(BF16), _row(b_if), seq,
        PROJ_TM, MLSTM_IN_COLS, MLSTM_IN_SUB)
    gt = jnp.transpose(gates.reshape(bsz, seq, 2, C_HEADS), (2, 0, 3, 1))
    i_pre = gt[0]
    b_cum = _mlstm_gates(gt[1].reshape(bsz * C_HEADS, seq), MLSTM_CHUNK).reshape(bsz, C_HEADS, seq)
    hs = _mlstm(q, k, v, xc, z, i_pre[..., None], b_cum[..., None], i_pre[:, :, None, :],
                b_cum[:, :, None, :], _row(ln_w), _row(skip), bsz, seq)
    return _resid_matmul(x, [hs], [w_out.astype(BF16)], PROJ_TM, PROJ_TN, "odd_out")


def kernel(x, even_norm, even_w_in, a_conv_w, a_conv_b, a_w_r, a_b_r, a_w_i, a_b_i, a_lambda, b_mu, b_w0, b_w_up, b_a0, b_a_up, b_g_up, b_k_k, b_k_a, b_r_k, b_ln_w, b_ln_b, even_w_out, odd_norm, odd_w_in, c_conv_w, c_conv_b, c_w_q, c_w_k, c_w_v, c_w_if, c_b_if, c_ln_w, c_skip, odd_w_out, ffn_norm, ffn_w_gate, ffn_w_up, ffn_conv_w, ffn_conv_b, ffn_w_down, final_norm):
    bsz, seq, d = x.shape
    depth = ffn_norm.shape[0]
    h = x.reshape(bsz * seq, d)
    for layer in range(depth):
        if layer % 2 == 0:
            e = layer // 2
            h = _even_layer(h, bsz, seq, even_norm[e], even_w_in[e], a_conv_w[e], a_conv_b[e], a_w_r[e],
                            a_b_r[e], a_w_i[e], a_b_i[e], a_lambda[e], b_mu[e], b_w0[e], b_w_up[e], b_a0[e],
                            b_a_up[e], b_g_up[e], b_k_k[e], b_k_a[e], b_r_k[e].reshape(-1), b_ln_w[e],
                            b_ln_b[e], even_w_out[e])
        else:
            o = layer // 2
            h = _odd_layer(h, bsz, seq, odd_norm[o], odd_w_in[o], c_conv_w[o], c_conv_b[o], c_w_q[o], c_w_k[o],
                           c_w_v[o], c_w_if[o], c_b_if[o], c_ln_w[o], c_skip[o], odd_w_out[o])
        h = _ffn(h, _row(ffn_norm[layer]), ffn_w_gate[layer].astype(BF16), ffn_w_up[layer].astype(BF16),
                 ffn_conv_w[layer], _row(ffn_conv_b[layer]), ffn_w_down[layer].astype(BF16),
                 _row(final_norm), seq, FFN_TM, FFN_TF, layer == depth - 1, "ffn%d" % layer)
    return h.reshape(bsz, seq, d)
```

```python
import functools
import math

import jax
import jax.numpy as jnp
from jax import lax
from jax.experimental import pallas as pl
from jax.experimental.pallas import tpu as pltpu

F32 = jnp.float32
BF16 = jnp.bfloat16

EPS = 1e-6
D_MODEL = 2048
A_WIDTH = 1024
A_BLOCKS = 8
A_CONV = 4
LRU_C = 8.0
B_WIDTH = 1024
B_HEAD = 64
B_DECAY_RANK = 64
B_AAA_RANK = 64
B_GATE_RANK = 160
B_SMALL = 512
B_SMALL_K = 384
B_LN_EPS = 64e-5
C_WIDTH = 4096
C_HEADS = 8
C_HEAD = 512
C_QKV_BLOCK = 4
C_CONV = 4
D_FF = 5632
FFN_CONV = 3

SUBLANES = 8
LANES = 128
MXU_DIM = 256
VMEM_LIMIT = 56 * 1024 * 1024

RWKV_CHUNK = 64
RWKV_GROUP = 4 * B_HEAD
RWKV_ROWS = 256
MLSTM_CHUNK = 256
MLSTM_HEADS_PER_STEP = 2
PROJ_TM, PROJ_TN = 1024, 512
PROJ_TN_WIDE = 1024
FFN_TM, FFN_TF = 512, 512
RGLRU_ROWS = 256
MLSTM_IN_COLS, MLSTM_IN_SUB = 512, 256


def _params(sem):
    return pltpu.CompilerParams(dimension_semantics=sem, vmem_limit_bytes=VMEM_LIMIT)


def _dot(a, b):
    return jnp.dot(a, b, preferred_element_type=F32)


def _dot_nt(a, b):
    return lax.dot_general(a, b, (((1,), (1,)), ((), ())), preferred_element_type=F32)


def _dot_tn(a, b):
    return lax.dot_general(a, b, (((0,), (0,)), ((), ())), preferred_element_type=F32)


def _split_bf16(x, terms):
    parts = []
    for _ in range(terms):
        p = x.astype(BF16)
        parts.append(p)
        x = x - p.astype(F32)
    return parts


def _sigmoid(x):
    return 1.0 / (1.0 + jnp.exp(-x))


def _silu(x):
    return x * _sigmoid(x)


def _rms(x, g):
    return x * lax.rsqrt(jnp.mean(x * x, axis=-1, keepdims=True) + EPS) * g


def _shift_rows(x, k, prev8):
    r = pltpu.roll(x, k, 0)
    fix = pltpu.roll(prev8, k, 0)
    row = lax.broadcasted_iota(jnp.int32, (SUBLANES, x.shape[1]), 0)
    head = jnp.where(row < k, fix, r[:SUBLANES])
    return jnp.concatenate([head, r[SUBLANES:]], axis=0)


def _norm_matmul_kernel(x_ref, g_ref, w_ref, o_ref, hn_ref, *, silu):
    @pl.when(pl.program_id(1) == 0)
    def _():
        hn_ref[...] = _rms(x_ref[...], g_ref[...]).astype(BF16)

    y = _dot(hn_ref[...], w_ref[...])
    o_ref[...] = (_silu(y) if silu else y).astype(o_ref.dtype)


def _norm_matmul(x, g, w, tm, tn, name, col_start=0, silu_bf16=False):
    n, d = x.shape
    nout = w.shape[1] - col_start
    off = col_start // tn
    assert col_start % tn == 0 and nout % tn == 0
    return pl.pallas_call(
        functools.partial(_norm_matmul_kernel, silu=silu_bf16),
        grid=(n // tm, nout // tn),
        in_specs=[pl.BlockSpec((tm, d), lambda i, j: (i, 0)),
                  pl.BlockSpec((1, d), lambda i, j: (0, 0)),
                  pl.BlockSpec((d, tn), lambda i, j: (0, j + off))],
        out_specs=pl.BlockSpec((tm, tn), lambda i, j: (i, j)),
        out_shape=jax.ShapeDtypeStruct((n, nout), BF16 if silu_bf16 else F32),
        scratch_shapes=[pltpu.VMEM((tm, d), BF16)],
        compiler_params=_params(("arbitrary", "arbitrary")),
        name=name,
    )(x, g, w)


def _resid_matmul_kernel(*refs, n_in):
    x_ref = refs[0]
    a_refs = refs[1:1 + n_in]
    w_refs = refs[1 + n_in:1 + 2 * n_in]
    o_ref = refs[1 + 2 * n_in]
    acc = x_ref[...]
    for a_ref, w_ref in zip(a_refs, w_refs):
        acc = acc + _dot(a_ref[...], w_ref[...])
    o_ref[...] = acc


def _resid_matmul(x, acts, ws, tm, tn, name):
    n, d = x.shape
    n_in = len(acts)
    in_specs = [pl.BlockSpec((tm, tn), lambda i, j: (i, j))]
    in_specs += [pl.BlockSpec((tm, a.shape[1]), lambda i, j: (i, 0)) for a in acts]
    in_specs += [pl.BlockSpec((w.shape[0], tn), lambda i, j: (0, j)) for w in ws]
    return pl.pallas_call(
        functools.partial(_resid_matmul_kernel, n_in=n_in),
        grid=(n // tm, d // tn),
        in_specs=in_specs,
        out_specs=pl.BlockSpec((tm, tn), lambda i, j: (i, j)),
        out_shape=jax.ShapeDtypeStruct((n, d), F32),
        compiler_params=_params(("arbitrary", "arbitrary")),
        name=name,
    )(x, *acts, *ws)


def _ffn_kernel(x_ref, g_ref, wg_ref, wu_ref, cw_ref, cb_ref, wd_ref, fg_ref, o_ref, hn_ref, carry_ref,
                *, tiles_per_seq, final_norm):
    i = pl.program_id(0)
    j = pl.program_id(1)
    tm = x_ref.shape[0]

    @pl.when(j == 0)
    def _():
        x = x_ref[...]
        hn_ref[...] = _rms(x, g_ref[...]).astype(BF16)
        o_ref[...] = x

    hn = hn_ref[...]
    gate = _dot(hn, wg_ref[...])
    up = _dot(hn, wu_ref[...])
    seq_start = (i % tiles_per_seq) == 0
    prev8 = jnp.where(seq_start, 0.0, carry_ref[j])
    carry_ref[j] = gate[tm - SUBLANES:]
    cw = cw_ref[...]
    conv = (cb_ref[...] + gate * cw[2:3] + _shift_rows(gate, 1, prev8) * cw[1:2]
            + _shift_rows(gate, 2, prev8) * cw[0:1])
    u = (_silu(conv) * up).astype(BF16)
    o_ref[...] += _dot(u, wd_ref[...])

    if final_norm:
        @pl.when(j == pl.num_programs(1) - 1)
        def _():
            o_ref[...] = _rms(o_ref[...], fg_ref[...])


def _ffn(x, g, wg, wu, cw, cb, wd, fg, seq, tm, tf, final_norm, name):
    n, d = x.shape
    f = wg.shape[1]
    nf = f // tf
    kern = functools.partial(_ffn_kernel, tiles_per_seq=seq // tm, final_norm=final_norm)
    return pl.pallas_call(
        kern,
        grid=(n // tm, nf),
        in_specs=[pl.BlockSpec((tm, d), lambda i, j: (i, 0)),
                  pl.BlockSpec((1, d), lambda i, j: (0, 0)),
                  pl.BlockSpec((d, tf), lambda i, j: (0, j)),
                  pl.BlockSpec((d, tf), lambda i, j: (0, j)),
                  pl.BlockSpec((FFN_CONV, tf), lambda i, j: (0, j)),
                  pl.BlockSpec((1, tf), lambda i, j: (0, j)),
                  pl.BlockSpec((tf, d), lambda i, j: (j, 0)),
                  pl.BlockSpec((1, d), lambda i, j: (0, 0))],
        out_specs=pl.BlockSpec((tm, d), lambda i, j: (i, 0)),
        out_shape=jax.ShapeDtypeStruct((n, d), F32),
        scratch_shapes=[pltpu.VMEM((tm, d), BF16), pltpu.VMEM((nf, SUBLANES, tf), F32)],
        compiler_params=_params(("arbitrary", "arbitrary")),
        name=name,
    )(x, g, wg, wu, cw, cb, wd, fg)


def _rglru_kernel(xa_ref, ga_ref, cw_ref, cb_ref, wr_ref, br_ref, wi_ref, bi_ref, lam_ref, o_ref,
                  prev_ref, h_ref):
    t = pl.program_id(1)
    tt, width = xa_ref.shape

    @pl.when(t == 0)
    def _():
        prev_ref[...] = jnp.zeros_like(prev_ref)
        h_ref[...] = jnp.zeros_like(h_ref)

    x = xa_ref[...]
    prev8 = prev_ref[...]
    prev_ref[...] = x[tt - SUBLANES:]
    cw = cw_ref[...]
    xc = cb_ref[...] + x * cw[3:4]
    for k in range(1, A_CONV):
        xc = xc + _shift_rows(x, k, prev8) * cw[A_CONV - 1 - k:A_CONV - k]

    xb = xc.astype(BF16)
    n_grp = width // MXU_DIM
    r_pre = jnp.concatenate(
        [_dot(xb[:, g * MXU_DIM:(g + 1) * MXU_DIM], wr_ref[g]) for g in range(n_grp)], axis=1)
    i_pre = jnp.concatenate(
        [_dot(xb[:, g * MXU_DIM:(g + 1) * MXU_DIM], wi_ref[g]) for g in range(n_grp)], axis=1)
    r = _sigmoid(r_pre + br_ref[...])
    ig = _sigmoid(i_pre + bi_ref[...])
    neg_lam = -lam_ref[...]
    softplus = jnp.maximum(neg_lam, 0.0) + jnp.log1p(jnp.exp(-jnp.abs(neg_lam)))
    log_a = (-LRU_C) * r * softplus
    a = jnp.exp(log_a)
    u = jnp.sqrt(1.0 - jnp.exp(2.0 * log_a)) * (ig * xc)

    n_sub = tt // SUBLANES
    a3 = a.reshape(n_sub, SUBLANES, width)
    u3 = u.reshape(n_sub, SUBLANES, width)
    sub = lax.broadcasted_iota(jnp.int32, (n_sub, SUBLANES, width), 1)
    s = 1
    while s < SUBLANES:
        keep = sub >= s
        a_prev = pltpu.roll(a3, s, 1)
        u_prev = pltpu.roll(u3, s, 1)
        u3 = jnp.where(keep, a3 * u_prev + u3, u3)
        a3 = jnp.where(keep, a3 * a_prev, a3)
        s *= 2
    carry = h_ref[...]
    groups = []
    for gi in range(n_sub):
        hg = u3[gi] + a3[gi] * carry
        groups.append(hg)
        carry = hg[SUBLANES - 1:]
    h = jnp.concatenate(groups, axis=0)
    h_ref[...] = carry

    ga = ga_ref[...]
    gelu = 0.5 * ga * (1.0 + jnp.tanh(math.sqrt(2.0 / math.pi) * (ga + 0.044715 * (ga * ga * ga))))
    o_ref[...] = (h * gelu).astype(o_ref.dtype)


def _rglru(p, cw, cb, wr, br, wi, bi, lam, bsz, seq, tt):
    n = p.shape[0]
    nt = seq // tt
    w = A_WIDTH
    vec = lambda: pl.BlockSpec((1, w), lambda b, t: (0, 0))
    return pl.pallas_call(
        _rglru_kernel,
        grid=(bsz, nt),
        in_specs=[pl.BlockSpec((tt, w), lambda b, t: (b * nt + t, 0)),
                  pl.BlockSpec((tt, w), lambda b, t: (b * nt + t, 1)),
                  pl.BlockSpec((A_CONV, w), lambda b, t: (0, 0)), vec(),
                  pl.BlockSpec(wr.shape, lambda b, t: (0, 0, 0)), vec(),
                  pl.BlockSpec(wi.shape, lambda b, t: (0, 0, 0)), vec(), vec()],
        out_specs=pl.BlockSpec((tt, w), lambda b, t: (b * nt + t, 0)),
        out_shape=jax.ShapeDtypeStruct((n, w), BF16),
        scratch_shapes=[pltpu.VMEM((SUBLANES, w), F32), pltpu.VMEM((1, w), F32)],
        compiler_params=_params(("arbitrary", "arbitrary")),
        name="rglru",
    )(p, p, cw, cb, wr, br, wi, bi, lam)


def _rwkv_kernel(r_ref, k_ref, v_ref, sm_ref, mur_ref, muk_ref, muv_ref, mus_ref, w0_ref, wup_ref, a0_ref,
                 aup_ref, gup_ref, kkw_ref, kaw_ref, rkw_ref, lnw_ref, lnb_ref, o_ref,
                 s_ref, pr_ref, pk_ref, pv_ref, ps_ref):
    tt = r_ref.shape[0]
    L = RWKV_CHUNK
    gw = RWKV_GROUP
    n_grp = B_WIDTH // gw
    n_chunk = tt // L

    @pl.when(pl.program_id(1) == 0)
    def _():
        s_ref[...] = jnp.zeros_like(s_ref)
        pr_ref[...] = jnp.zeros_like(pr_ref)
        pk_ref[...] = jnp.zeros_like(pk_ref)
        pv_ref[...] = jnp.zeros_like(pv_ref)
        ps_ref[...] = jnp.zeros_like(ps_ref)

    def lerp(x_ref, p_ref, mu_ref):
        x = x_ref[...]
        xs = _shift_rows(x, 1, p_ref[...])
        p_ref[...] = x[tt - SUBLANES:]
        return x + (xs - x) * mu_ref[...]

    r = lerp(r_ref, pr_ref, mur_ref)
    k = lerp(k_ref, pk_ref, muk_ref)
    v = lerp(v_ref, pv_ref, muv_ref)
    sm = lerp(sm_ref, ps_ref, mus_ref)[:, :B_SMALL_K]

    z = w0_ref[...] + _dot(jnp.tanh(sm).astype(BF16), wup_ref[...])
    log_w = (-math.exp(-0.5)) * _sigmoid(z)
    a = _sigmoid(a0_ref[...] + _dot(sm.astype(BF16), aup_ref[...]))
    g = _dot(_sigmoid(sm).astype(BF16), gup_ref[...])

    row_g = lax.broadcasted_iota(jnp.int32, (gw, gw), 0)
    col_g = lax.broadcasted_iota(jnp.int32, (gw, gw), 1)
    same_head = (row_g // B_HEAD) == (col_g // B_HEAD)
    head_mask = jnp.where(same_head, 1.0, 0.0)
    head_mask_bf = head_mask.astype(BF16)

    def head_sum(x):
        xs = jnp.concatenate([x[:, gi * gw:(gi + 1) * gw] for gi in range(n_grp)], axis=0)
        hi, lo = _split_bf16(xs, 2)
        s = _dot(jnp.concatenate([hi, lo], axis=0), head_mask_bf)
        s = s[:n_grp * tt] + s[n_grp * tt:]
        return jnp.concatenate([s[gi * tt:(gi + 1) * tt] for gi in range(n_grp)], axis=1)

    def bd(x):
        return jnp.concatenate([x.astype(BF16)] * (gw // B_HEAD), axis=0) * head_mask_bf

    kk = k * kkw_ref[...]
    kk = kk * lax.rsqrt(jnp.maximum(head_sum(kk * kk), 1e-12))
    k2 = k * (1.0 + (a - 1.0) * kaw_ref[...])

    row_t = lax.broadcasted_iota(jnp.int32, (tt, tt), 0)
    col_t = lax.broadcasted_iota(jnp.int32, (tt, tt), 1)
    tril = jnp.where((row_t >= col_t) & ((row_t // L) == (col_t // L)), 1.0, 0.0).astype(BF16)
    cum = sum(_dot(tril, part) for part in _split_bf16(log_w, 3))
    p_in = jnp.exp(cum)
    p_inv = jnp.exp(-cum)
    a_bar = (-kk) * jnp.exp(cum - log_w)
    r_bar = r * p_in
    b_bar = kk * a * p_inv
    k_bar = k2 * p_inv

    row_p = lax.broadcasted_iota(jnp.int32, (L, gw), 0)
    src_p = lax.broadcasted_iota(jnp.int32, (L, gw), 1) % B_HEAD
    strict_lower = row_p > src_p
    lower = row_p >= src_p
    n_doublings = int(math.log2(L)) - 1

    units = [(c, gi) for c in range(n_chunk) for gi in range(n_grp)]

    def tile(arr, c, gi):
        return arr[c * L:(c + 1) * L, gi * gw:(gi + 1) * gw]

    ar = {u: jnp.concatenate([tile(a_bar, *u), tile(r_bar, *u)], axis=0).astype(BF16) for u in units}
    m_b = {u: _dot_nt(ar[u], bd(tile(b_bar, *u))) for u in units}
    m_k = {u: _dot_nt(ar[u], bd(tile(k_bar, *u))) for u in units}
    x = {u: jnp.where(strict_lower, m_b[u][:L], 0.0) for u in units}
    a_rb = {u: jnp.where(lower, m_b[u][L:], 0.0).astype(BF16) for u in units}
    akrk = {u: jnp.concatenate([jnp.where(strict_lower, m_k[u][:L], 0.0),
                                jnp.where(lower, m_k[u][L:], 0.0)], axis=0).astype(BF16) for u in units}
    cy = {u: _dot(akrk[u], bd(tile(v, *u))) for u in units}
    n_inv = dict(x)
    x_pow = {u: _dot(x[u].astype(BF16), bd(x[u])) for u in units}
    for step in range(n_doublings):
        if step + 1 < n_doublings:
            both = {u: _dot(jnp.concatenate([x_pow[u], n_inv[u]], axis=0).astype(BF16), bd(x_pow[u]))
                    for u in units}
            n_inv = {u: n_inv[u] + x_pow[u] + both[u][L:] for u in units}
            x_pow = {u: both[u][:L] for u in units}
        else:
            n_inv = {u: n_inv[u] + x_pow[u] + _dot(n_inv[u].astype(BF16), bd(x_pow[u])) for u in units}
    n_inv = {u: n_inv[u].astype(BF16) for u in units}

    groups = range(n_grp)
    states = [s_ref[gi] for gi in groups]
    y_rows = []
    for c in range(n_chunk):
        p_last = p_in[(c + 1) * L - 1:(c + 1) * L]
        pl_g = [p_last[:, gi * gw:(gi + 1) * gw] for gi in groups]
        bk = [jnp.concatenate([tile(b_bar, c, gi) * pl_g[gi], tile(k_bar, c, gi) * pl_g[gi]],
                              axis=0).astype(BF16) for gi in groups]
        ah = [_dot_nt(ar[c, gi], states[gi].astype(BF16)) for gi in groups]
        rhs = [ah[gi][:L] + cy[c, gi][:L] for gi in groups]
        u_c = [rhs[gi] + _dot(n_inv[c, gi], bd(rhs[gi])) for gi in groups]
        ds = [_dot_tn(jnp.concatenate([u_c[gi], tile(v, c, gi)], axis=0).astype(BF16), bk[gi])
              for gi in groups]
        states = [states[gi] * pl_g[gi] + head_mask * ds[gi] for gi in groups]
        y_rows.append(jnp.concatenate(
            [ah[gi][L:] + cy[c, gi][L:] + _dot(a_rb[c, gi], bd(u_c[gi])) for gi in groups], axis=1))
    for gi in groups:
        s_ref[gi] = states[gi]

    y = jnp.concatenate(y_rows, axis=0)
    inv_n = 1.0 / B_HEAD
    yc = y - head_sum(y) * inv_n
    var = head_sum(yc * yc) * inv_n
    yn = yc * lax.rsqrt(var + B_LN_EPS) * lnw_ref[...] + lnb_ref[...]
    bonus = head_sum(r * k2 * rkw_ref[...]) * v
    o_ref[...] = ((yn + bonus) * g).astype(o_ref.dtype)


def _rwkv(p, mur, muk, muv, mus, w0, wup, a0, aup, gup, kkw, kaw, rkw, lnw, lnb, bsz, seq):
    n = p.shape[0]
    tt = RWKV_ROWS
    nt = seq // tt
    w = B_WIDTH
    col0 = 2 * A_WIDTH // w
    vec = lambda: pl.BlockSpec((1, w), lambda b, t: (0, 0))
    mat = lambda: pl.BlockSpec((B_SMALL_K, w), lambda b, t: (0, 0))
    return pl.pallas_call(
        _rwkv_kernel,
        grid=(bsz, nt),
        in_specs=[pl.BlockSpec((tt, w), lambda b, t: (b * nt + t, col0)),
                  pl.BlockSpec((tt, w), lambda b, t: (b * nt + t, col0 + 1)),
                  pl.BlockSpec((tt, w), lambda b, t: (b * nt + t, col0 + 2)),
                  pl.BlockSpec((tt, B_SMALL), lambda b, t: (b * nt + t, (2 * A_WIDTH + 3 * w) // B_SMALL)),
                  vec(), vec(), vec(), pl.BlockSpec((1, B_SMALL), lambda b, t: (0, 0)),
                  vec(), mat(), vec(), mat(), mat(), vec(), vec(), vec(), vec(), vec()],
        out_specs=pl.BlockSpec((tt, w), lambda b, t: (b * nt + t, 0)),
        out_shape=jax.ShapeDtypeStruct((n, w), BF16),
        scratch_shapes=[pltpu.VMEM((w // RWKV_GROUP, RWKV_GROUP, RWKV_GROUP), F32),
                        pltpu.VMEM((SUBLANES, w), F32), pltpu.VMEM((SUBLANES, w), F32),
                        pltpu.VMEM((SUBLANES, w), F32), pltpu.VMEM((SUBLANES, B_SMALL), F32)],
        compiler_params=_params(("arbitrary", "arbitrary")),
        name="rwkv7",
    )(p, p, p, p, mur, muk, muv, mus, w0, wup, a0, aup, gup, kkw, kaw, rkw, lnw, lnb)


def _mlstm_in_kernel(x_ref, g_ref, w_ref, cw_ref, cb_ref, wq_ref, wk_ref, wv_ref, wif_ref, bif_ref,
                     q_ref, k_ref, v_ref, xc_ref, gates_ref, hn_ref, prev_ref, *, tiles_per_seq, sub_rows):
    i = pl.program_id(0)
    j = pl.program_id(1)
    tm = x_ref.shape[0]
    cb = w_ref.shape[1]
    n_grp = cb // MXU_DIM

    @pl.when(j == 0)
    def _():
        hn_ref[...] = _rms(x_ref[...], g_ref[...]).astype(BF16)
        gates_ref[...] = jnp.broadcast_to(bif_ref[...], gates_ref.shape)

    def blockdiag(xb, wb_ref):
        return jnp.concatenate(
            [_dot(xb[:, g * MXU_DIM:(g + 1) * MXU_DIM], wb_ref[g]) for g in range(n_grp)], axis=1)

    seq_start = (i % tiles_per_seq) == 0
    prev8 = jnp.where(seq_start, 0.0, prev_ref[j])
    cw = cw_ref[...]
    for s in range(tm // sub_rows):
        rows = pl.ds(s * sub_rows, sub_rows)
        xm = _dot(hn_ref[rows, :], w_ref[...])
        conv = cb_ref[...] + xm * cw[C_CONV - 1:C_CONV]
        for kk in range(1, C_CONV):
            conv = conv + _shift_rows(xm, kk, prev8) * cw[C_CONV - 1 - kk:C_CONV - kk]
        prev8 = xm[sub_rows - SUBLANES:]
        xc = _silu(conv)
        xcb = xc.astype(BF16)
        xc_ref[rows, :] = xcb
        q = blockdiag(xcb, wq_ref)
        k = blockdiag(xcb, wk_ref)
        v = blockdiag(xm.astype(BF16), wv_ref)
        qb = q.astype(BF16)
        kb = k.astype(BF16)
        vb = v.astype(BF16)
        q_ref[rows, :] = qb
        k_ref[rows, :] = (k * (C_HEAD ** -0.5)).astype(BF16)
        v_ref[rows, :] = vb
        gates_ref[rows, :] += _dot(qb, wif_ref[0]) + _dot(kb, wif_ref[1]) + _dot(vb, wif_ref[2])
    prev_ref[j] = prev8


def _mlstm_in(x, g, w_in, cw, cb, wq, wk, wv, wif, bif, seq, tm, cblk, sub_rows):
    n, d = x.shape
    ncb = C_WIDTH // cblk
    gpb = cblk // MXU_DIM
    blk = lambda: pl.BlockSpec((tm, cblk), lambda i, j: (i, j))
    wspec = lambda: pl.BlockSpec((gpb, MXU_DIM, MXU_DIM), lambda i, j: (j, 0, 0))
    act = lambda dt: jax.ShapeDtypeStruct((n, C_WIDTH), dt)
    kern = functools.partial(_mlstm_in_kernel, tiles_per_seq=seq // tm, sub_rows=sub_rows)
    return pl.pallas_call(
        kern,
        grid=(n // tm, ncb),
        in_specs=[pl.BlockSpec((tm, d), lambda i, j: (i, 0)),
                  pl.BlockSpec((1, d), lambda i, j: (0, 0)),
                  pl.BlockSpec((d, cblk), lambda i, j: (0, j)),
                  pl.BlockSpec((C_CONV, cblk), lambda i, j: (0, j)),
                  pl.BlockSpec((1, cblk), lambda i, j: (0, j)),
                  wspec(), wspec(), wspec(),
                  pl.BlockSpec((3, cblk, 2 * C_HEADS), lambda i, j: (0, j, 0)),
                  pl.BlockSpec((1, 2 * C_HEADS), lambda i, j: (0, 0))],
        out_specs=[blk(), blk(), blk(), blk(),
                   pl.BlockSpec((tm, 2 * C_HEADS), lambda i, j: (i, 0))],
        out_shape=[act(BF16), act(BF16), act(BF16), act(BF16),
                   jax.ShapeDtypeStruct((n, 2 * C_HEADS), F32)],
        scratch_shapes=[pltpu.VMEM((tm, d), BF16), pltpu.VMEM((ncb, SUBLANES, cblk), F32)],
        compiler_params=_params(("arbitrary", "arbitrary")),
        name="mlstm_in",
    )(x, g, w_in, cw, cb, wq, wk, wv, wif, bif)


def _mlstm_gate_kernel(f_ref, b_ref):
    L = f_ref.shape[1]
    f = f_ref[...]
    lf = jnp.minimum(f, 0.0) - jnp.log1p(jnp.exp(-jnp.abs(f)))
    row = lax.broadcasted_iota(jnp.int32, (L, L), 0)
    col = lax.broadcasted_iota(jnp.int32, (L, L), 1)
    triu = jnp.where(row <= col, 1.0, 0.0).astype(BF16)
    b_ref[...] = sum(_dot(part, triu) for part in _split_bf16(lf, 3))


def _mlstm_gates(f_pre, chunk):
    rows, seq = f_pre.shape
    return pl.pallas_call(
        _mlstm_gate_kernel,
        grid=(seq // chunk,),
        in_specs=[pl.BlockSpec((rows, chunk), lambda c: (0, c))],
        out_specs=pl.BlockSpec((rows, chunk), lambda c: (0, c)),
        out_shape=jax.ShapeDtypeStruct((rows, seq), F32),
        compiler_params=_params(("arbitrary",)),
        name="mlstm_gates",
    )(f_pre)


def _mlstm_kernel(q_ref, k_ref, v_ref, xc_ref, za_ref, ir_ref, br_ref, lnw_ref, skip_ref,
                  o_ref, ct_ref, m_ref):
    c = pl.program_id(2)
    L = q_ref.shape[0]
    n_heads, hd = ct_ref.shape[0], ct_ref.shape[1]

    @pl.when(c == 0)
    def _():
        ct_ref[...] = jnp.zeros_like(ct_ref)
        m_ref[...] = jnp.zeros_like(m_ref)

    def lanes(x, width):
        return jnp.concatenate([x] * (width // LANES), axis=1)

    row = lax.broadcasted_iota(jnp.int32, (L, L), 0)
    col = lax.broadcasted_iota(jnp.int32, (L, L), 1)
    causal = row >= col
    eye = jnp.where(row == col, 1.0, 0.0).astype(BF16)

    def to_col(x_row):
        return sum(_dot_nt(eye, jnp.broadcast_to(part, (LANES, L))) for part in _split_bf16(x_row, 3))

    heads = range(n_heads)
    cols = [slice(i * hd, (i + 1) * hd) for i in heads]
    q = [q_ref[:, cols[i]] for i in heads]
    k = [k_ref[:, cols[i]] for i in heads]
    v_aug = [jnp.concatenate([v_ref[:, cols[i]], jnp.ones((L, LANES), BF16)], axis=1) for i in heads]
    li_row = [ir_ref[i] for i in heads]
    b_row = [br_ref[i] for i in heads]
    m_prev = [m_ref[i] for i in heads]

    qk = [_dot_nt(q[i], k[i]) for i in heads]
    q_ct = [_dot(q[i], ct_ref[i].astype(BF16)) for i in heads]
    b_col = [to_col(b_row[i]) for i in heads]
    li_col = [to_col(li_row[i]) for i in heads]
    b_last = [b_col[i][L - 1:L] for i in heads]

    d_log = [jnp.where(causal, lanes(b_col[i], L) - b_row[i] + li_row[i], -jnp.inf) for i in heads]
    inter = [b_col[i] + m_prev[i] for i in heads]
    m_t = [jnp.maximum(inter[i], jnp.max(d_log[i], axis=-1, keepdims=True)) for i in heads]
    s = [(qk[i] * jnp.exp(d_log[i] - lanes(m_t[i], L))).astype(BF16) for i in heads]
    sc = [jnp.exp(inter[i] - m_t[i]) for i in heads]
    num_den = [_dot(s[i], v_aug[i]) + lanes(sc[i], hd + LANES) * q_ct[i] for i in heads]

    g_log = [b_last[i] - b_col[i] + li_col[i] for i in heads]
    m_new = [jnp.maximum(b_last[i] + m_prev[i], jnp.max(g_log[i], axis=0, keepdims=True)) for i in heads]
    ke = [k[i] * lanes(jnp.exp(g_log[i] - m_new[i]).astype(BF16), hd) for i in heads]
    decay = [jnp.exp(b_last[i] + m_prev[i] - m_new[i]) for i in heads]
    for i in heads:
        ct_ref[i] = lanes(decay[i], hd + LANES) * ct_ref[i] + _dot_tn(ke[i], v_aug[i])
        m_ref[i] = m_new[i]

    for i in heads:
        inv = 1.0 / jnp.maximum(jnp.abs(num_den[i][:, hd:]), jnp.exp(-m_t[i]))
        h = num_den[i][:, :hd] * lanes(inv, hd)
        hc = h - jnp.mean(h, axis=-1, keepdims=True)
        hn = hc * lax.rsqrt(jnp.mean(hc * hc, axis=-1, keepdims=True) + EPS) * lnw_ref[:, cols[i]]
        hs = hn + skip_ref[:, cols[i]] * xc_ref[:, cols[i]].astype(F32)
        o_ref[:, cols[i]] = (hs * za_ref[:, cols[i]].astype(F32)).astype(o_ref.dtype)


def _mlstm(q, k, v, xc, z_act, i_row, b_row, lnw, skip, bsz, seq):
    n = q.shape[0]
    L = MLSTM_CHUNK
    nc = seq // L
    hd = C_HEAD
    hp = MLSTM_HEADS_PER_STEP
    blk = lambda: pl.BlockSpec((L, hp * hd), lambda b, h, c: (b * nc + c, h))
    rowspec = lambda: pl.BlockSpec((None, hp, 1, L), lambda b, h, c: (b, h, 0, c))
    vec = lambda: pl.BlockSpec((1, hp * hd), lambda b, h, c: (0, h))
    return pl.pallas_call(
        _mlstm_kernel,
        grid=(bsz, C_HEADS // hp, nc),
        in_specs=[blk(), blk(), blk(), blk(), blk(), rowspec(), rowspec(), vec(), vec()],
        out_specs=blk(),
        out_shape=jax.ShapeDtypeStruct((n, C_WIDTH), BF16),
        scratch_shapes=[pltpu.VMEM((hp, hd, hd + LANES), F32), pltpu.VMEM((hp, 1, LANES), F32)],
        compiler_params=_params(("arbitrary", "arbitrary", "arbitrary")),
        name="mlstm",
    )(q, k, v, xc, z_act, i_row, b_row, lnw, skip)


def _pack_block_diag(w, tile):
    g, bs, _ = w.shape
    per = tile // bs
    w = w.reshape(g // per, per, bs, bs)
    eye = jnp.eye(per, dtype=w.dtype)
    dense = jnp.einsum("npij,pq->npiqj", w, eye)
    return dense.reshape(g // per, tile, tile)


def _row(v):
    return v.reshape(1, -1)


def _even_layer(x, bsz, seq, norm, w_in, a_conv_w, a_conv_b, a_w_r, a_b_r, a_w_i, a_b_i, a_lambda,
                b_mu, b_w0, b_w_up, b_a0, b_a_up, b_g_up, b_k_k, b_k_a, b_r_k, b_ln_w, b_ln_b, w_out):
    main_w = 2 * A_WIDTH + 3 * B_WIDTH
    n_small = B_DECAY_RANK + B_AAA_RANK + B_GATE_RANK
    pad = B_SMALL - n_small
    assert main_w % B_SMALL == 0
    w_all = jnp.pad(w_in, ((0, 0), (0, pad))).astype(BF16)
    p = _norm_matmul(x, _row(norm), w_all, PROJ_TM, PROJ_TN, "even_in")

    ya = _rglru(p, a_conv_w, _row(a_conv_b),
                _pack_block_diag(a_w_r, MXU_DIM).astype(BF16), _row(a_b_r),
                _pack_block_diag(a_w_i, MXU_DIM).astype(BF16), _row(a_b_i), _row(a_lambda),
                bsz, seq, RGLRU_ROWS)

    mur, muk, muv = (_row(b_mu[i * B_WIDTH:(i + 1) * B_WIDTH]) for i in range(3))
    mus = _row(jnp.pad(b_mu[3 * B_WIDTH:], (0, pad)))

    def rows_at(w, start):
        out = jnp.zeros((B_SMALL_K, B_WIDTH), F32)
        return lax.dynamic_update_slice(out, w, (start, 0)).astype(BF16)

    wup = rows_at(b_w_up, 0)
    aup = rows_at(b_a_up, B_DECAY_RANK)
    gup = rows_at(b_g_up, B_DECAY_RANK + B_AAA_RANK)
    yb = _rwkv(p, mur, muk, muv, mus, _row(b_w0), wup, _row(b_a0), aup, gup, _row(b_k_k), _row(b_k_a),
               _row(b_r_k), _row(b_ln_w), _row(b_ln_b), bsz, seq)

    wo = w_out.astype(BF16)
    return _resid_matmul(x, [ya, yb], [wo[:A_WIDTH], wo[A_WIDTH:]], PROJ_TM, PROJ_TN_WIDE, "even_out")


def _odd_layer(x, bsz, seq, norm, w_in, conv_w, conv_b, w_q, w_k, w_v, w_if, b_if, ln_w, skip, w_out):
    w_in_b = w_in.astype(BF16)
    g = _row(norm)
    z_act = _norm_matmul(x, g, w_in_b, PROJ_TM, PROJ_TN_WIDE, "odd_in_z", col_start=C_WIDTH, silu_bf16=True)
    q, k, v, xc, gates = _mlstm_in(
        x, g, w_in_b, conv_w, _row(conv_b),
        _pack_block_diag(w_q, MXU_DIM).astype(BF16), _pack_block_diag(w_k, MXU_DIM).astype(BF16),
        _pack_block_diag(w_v, MXU_DIM).astype(BF16), w_if.astype(BF16), _row(b_if), seq,
        PROJ_TM, MLSTM_IN_COLS, MLSTM_IN_SUB)
    gt = jnp.transpose(gates.reshape(bsz, seq, 2, C_HEADS), (2, 0, 3, 1))
    i_pre = gt[0]
    b_cum = _mlstm_gates(gt[1].reshape(bsz * C_HEADS, seq), MLSTM_CHUNK).reshape(bsz, C_HEADS, seq)
    hs = _mlstm(q, k, v, xc, z_act, i_pre[:, :, None, :], b_cum[:, :, None, :], _row(ln_w), _row(skip),
                bsz, seq)
    return _resid_matmul(x, [hs], [w_out.astype(BF16)], PROJ_TM, PROJ_TN_WIDE, "odd_out")


def kernel(x, even_norm, even_w_in, a_conv_w, a_conv_b, a_w_r, a_b_r, a_w_i, a_b_i, a_lambda, b_mu, b_w0, b_w_up, b_a0, b_a_up, b_g_up, b_k_k, b_k_a, b_r_k, b_ln_w, b_ln_b, even_w_out, odd_norm, odd_w_in, c_conv_w, c_conv_b, c_w_q, c_w_k, c_w_v, c_w_if, c_b_if, c_ln_w, c_skip, odd_w_out, ffn_norm, ffn_w_gate, ffn_w_up, ffn_conv_w, ffn_conv_b, ffn_w_down, final_norm):
    bsz, seq, d = x.shape
    depth = ffn_norm.shape[0]
    h = x.reshape(bsz * seq, d)
    for layer in range(depth):
        if layer % 2 == 0:
            e = layer // 2
            h = _even_layer(h, bsz, seq, even_norm[e], even_w_in[e], a_conv_w[e], a_conv_b[e], a_w_r[e],
                            a_b_r[e], a_w_i[e], a_b_i[e], a_lambda[e], b_mu[e], b_w0[e], b_w_up[e], b_a0[e],
                            b_a_up[e], b_g_up[e], b_k_k[e], b_k_a[e], b_r_k[e].reshape(-1), b_ln_w[e],
                            b_ln_b[e], even_w_out[e])
        else:
            o = layer // 2
            h = _odd_layer(h, bsz, seq, odd_norm[o], odd_w_in[o], c_conv_w[o], c_conv_b[o], c_w_q[o], c_w_k[o],
                           c_w_v[o], c_w_if[o], c_b_if[o], c_ln_w[o], c_skip[o], odd_w_out[o])
        h = _ffn(h, _row(ffn_norm[layer]), ffn_w_gate[layer].astype(BF16), ffn_w_up[layer].astype(BF16),
                 ffn_conv_w[layer], _row(ffn_conv_b[layer]), ffn_w_down[layer].astype(BF16),
                 _row(final_norm), seq, FFN_TM, FFN_TF, layer == depth - 1, "ffn%d" % layer)
    return h.reshape(bsz, seq, d)
```

```python
import functools
import math

import jax
import jax.numpy as jnp
from jax import lax
from jax.experimental import pallas as pl
from jax.experimental.pallas import tpu as pltpu

F32 = jnp.float32
BF16 = jnp.bfloat16

EPS = 1e-6
D_MODEL = 2048
A_WIDTH = 1024
A_BLOCKS = 8
A_CONV = 4
LRU_C = 8.0
B_WIDTH = 1024
B_HEAD = 64
B_DECAY_RANK = 64
B_AAA_RANK = 64
B_GATE_RANK = 160
B_SMALL = 512
B_SMALL_K = 384
B_LN_EPS = 64e-5
C_WIDTH = 4096
C_HEADS = 8
C_HEAD = 512
C_QKV_BLOCK = 4
C_CONV = 4
D_FF = 5632
FFN_CONV = 3

SUBLANES = 8
LANES = 128
MXU_DIM = 256
VMEM_LIMIT = 56 * 1024 * 1024

RWKV_CHUNK = 64
RWKV_GROUP = 4 * B_HEAD
RWKV_ROWS = 256
MLSTM_CHUNK = 256
MLSTM_HEADS_PER_STEP = 4
PROJ_TM, PROJ_TN = 1024, 512
NORM_SUB_ROWS = 256
PROJ_TN_WIDE = 1024
FFN_TM, FFN_TF = 512, 512
RGLRU_ROWS = 256
MLSTM_IN_COLS, MLSTM_IN_SUB = 512, 256


def _params(sem):
    return pltpu.CompilerParams(dimension_semantics=sem, vmem_limit_bytes=VMEM_LIMIT)


def _dot(a, b):
    return jnp.dot(a, b, preferred_element_type=F32)


def _dot_nt(a, b):
    return lax.dot_general(a, b, (((1,), (1,)), ((), ())), preferred_element_type=F32)


def _dot_tn(a, b):
    return lax.dot_general(a, b, (((0,), (0,)), ((), ())), preferred_element_type=F32)


def _split_bf16(x, terms):
    parts = []
    for _ in range(terms):
        p = x.astype(BF16)
        parts.append(p)
        x = x - p.astype(F32)
    return parts


def _sigmoid(x):
    return 1.0 / (1.0 + jnp.exp(-x))


def _silu(x):
    return x * _sigmoid(x)


def _rms(x, g):
    return x * lax.rsqrt(jnp.mean(x * x, axis=-1, keepdims=True) + EPS) * g


def _shift_rows(x, k, prev8):
    r = pltpu.roll(x, k, 0)
    fix = pltpu.roll(prev8, k, 0)
    row = lax.broadcasted_iota(jnp.int32, (SUBLANES, x.shape[1]), 0)
    head = jnp.where(row < k, fix, r[:SUBLANES])
    return jnp.concatenate([head, r[SUBLANES:]], axis=0)


def _norm_matmul_kernel(x_ref, g_ref, w_ref, o_ref, hn_ref, *, silu):
    tm = x_ref.shape[0]

    def store(rows, y):
        o_ref[rows, :] = (_silu(y) if silu else y).astype(o_ref.dtype)

    @pl.when(pl.program_id(1) == 0)
    def _():
        for s in range(tm // NORM_SUB_ROWS):
            rows = pl.ds(s * NORM_SUB_ROWS, NORM_SUB_ROWS)
            hn = _rms(x_ref[rows, :], g_ref[...]).astype(BF16)
            hn_ref[rows, :] = hn
            store(rows, _dot(hn, w_ref[...]))

    @pl.when(pl.program_id(1) != 0)
    def _():
        store(pl.ds(0, tm), _dot(hn_ref[...], w_ref[...]))


def _norm_matmul(x, g, w, tm, tn, name, col_start=0, silu_bf16=False):
    n, d = x.shape
    nout = w.shape[1] - col_start
    off = col_start // tn
    assert col_start % tn == 0 and nout % tn == 0
    return pl.pallas_call(
        functools.partial(_norm_matmul_kernel, silu=silu_bf16),
        grid=(n // tm, nout // tn),
        in_specs=[pl.BlockSpec((tm, d), lambda i, j: (i, 0)),
                  pl.BlockSpec((1, d), lambda i, j: (0, 0)),
                  pl.BlockSpec((d, tn), lambda i, j: (0, j + off))],
        out_specs=pl.BlockSpec((tm, tn), lambda i, j: (i, j)),
        out_shape=jax.ShapeDtypeStruct((n, nout), BF16 if silu_bf16 else F32),
        scratch_shapes=[pltpu.VMEM((tm, d), BF16)],
        compiler_params=_params(("arbitrary", "arbitrary")),
        name=name,
    )(x, g, w)


def _resid_matmul_kernel(*refs, n_in):
    x_ref = refs[0]
    a_refs = refs[1:1 + n_in]
    w_refs = refs[1 + n_in:1 + 2 * n_in]
    o_ref = refs[1 + 2 * n_in]
    acc = x_ref[...]
    for a_ref, w_ref in zip(a_refs, w_refs):
        acc = acc + _dot(a_ref[...], w_ref[...])
    o_ref[...] = acc


def _resid_matmul(x, acts, ws, tm, tn, name):
    n, d = x.shape
    n_in = len(acts)
    in_specs = [pl.BlockSpec((tm, tn), lambda i, j: (i, j))]
    in_specs += [pl.BlockSpec((tm, a.shape[1]), lambda i, j: (i, 0)) for a in acts]
    in_specs += [pl.BlockSpec((w.shape[0], tn), lambda i, j: (0, j)) for w in ws]
    return pl.pallas_call(
        functools.partial(_resid_matmul_kernel, n_in=n_in),
        grid=(n // tm, d // tn),
        in_specs=in_specs,
        out_specs=pl.BlockSpec((tm, tn), lambda i, j: (i, j)),
        out_shape=jax.ShapeDtypeStruct((n, d), F32),
        compiler_params=_params(("arbitrary", "arbitrary")),
        name=name,
    )(x, *acts, *ws)


def _ffn_kernel(x_ref, g_ref, wg_ref, wu_ref, cw_ref, cb_ref, wd_ref, fg_ref, o_ref, hn_ref, carry_ref,
                *, tiles_per_seq, final_norm):
    i = pl.program_id(0)
    j = pl.program_id(1)
    tm = x_ref.shape[0]

    @pl.when(j == 0)
    def _():
        x = x_ref[...]
        hn_ref[...] = _rms(x, g_ref[...]).astype(BF16)
        o_ref[...] = x

    hn = hn_ref[...]
    gate = _dot(hn, wg_ref[...])
    up = _dot(hn, wu_ref[...])
    seq_start = (i % tiles_per_seq) == 0
    prev8 = jnp.where(seq_start, 0.0, carry_ref[j])
    carry_ref[j] = gate[tm - SUBLANES:]
    cw = cw_ref[...]
    conv = (cb_ref[...] + gate * cw[2:3] + _shift_rows(gate, 1, prev8) * cw[1:2]
            + _shift_rows(gate, 2, prev8) * cw[0:1])
    u = (_silu(conv) * up).astype(BF16)
    o_ref[...] += _dot(u, wd_ref[...])

    if final_norm:
        @pl.when(j == pl.num_programs(1) - 1)
        def _():
            o_ref[...] = _rms(o_ref[...], fg_ref[...])


def _ffn(x, g, wg, wu, cw, cb, wd, fg, seq, tm, tf, final_norm, name):
    n, d = x.shape
    f = wg.shape[1]
    nf = f // tf
    kern = functools.partial(_ffn_kernel, tiles_per_seq=seq // tm, final_norm=final_norm)
    return pl.pallas_call(
        kern,
        grid=(n // tm, nf),
        in_specs=[pl.BlockSpec((tm, d), lambda i, j: (i, 0)),
                  pl.BlockSpec((1, d), lambda i, j: (0, 0)),
                  pl.BlockSpec((d, tf), lambda i, j: (0, j)),
                  pl.BlockSpec((d, tf), lambda i, j: (0, j)),
                  pl.BlockSpec((FFN_CONV, tf), lambda i, j: (0, j)),
                  pl.BlockSpec((1, tf), lambda i, j: (0, j)),
                  pl.BlockSpec((tf, d), lambda i, j: (j, 0)),
                  pl.BlockSpec((1, d), lambda i, j: (0, 0))],
        out_specs=pl.BlockSpec((tm, d), lambda i, j: (i, 0)),
        out_shape=jax.ShapeDtypeStruct((n, d), F32),
        scratch_shapes=[pltpu.VMEM((tm, d), BF16), pltpu.VMEM((nf, SUBLANES, tf), F32)],
        compiler_params=_params(("arbitrary", "arbitrary")),
        name=name,
    )(x, g, wg, wu, cw, cb, wd, fg)


def _rglru_kernel(xa_ref, ga_ref, cw_ref, cb_ref, wr_ref, br_ref, wi_ref, bi_ref, lam_ref, o_ref,
                  prev_ref, h_ref):
    t = pl.program_id(1)
    tt, width = xa_ref.shape

    @pl.when(t == 0)
    def _():
        prev_ref[...] = jnp.zeros_like(prev_ref)
        h_ref[...] = jnp.zeros_like(h_ref)

    x = xa_ref[...]
    prev8 = prev_ref[...]
    prev_ref[...] = x[tt - SUBLANES:]
    cw = cw_ref[...]
    xc = cb_ref[...] + x * cw[3:4]
    for k in range(1, A_CONV):
        xc = xc + _shift_rows(x, k, prev8) * cw[A_CONV - 1 - k:A_CONV - k]

    xb = xc.astype(BF16)
    n_grp = width // MXU_DIM
    r_pre = jnp.concatenate(
        [_dot(xb[:, g * MXU_DIM:(g + 1) * MXU_DIM], wr_ref[g]) for g in range(n_grp)], axis=1)
    i_pre = jnp.concatenate(
        [_dot(xb[:, g * MXU_DIM:(g + 1) * MXU_DIM], wi_ref[g]) for g in range(n_grp)], axis=1)
    r = _sigmoid(r_pre + br_ref[...])
    ig = _sigmoid(i_pre + bi_ref[...])
    neg_lam = -lam_ref[...]
    softplus = jnp.maximum(neg_lam, 0.0) + jnp.log1p(jnp.exp(-jnp.abs(neg_lam)))
    log_a = (-LRU_C) * r * softplus
    a = jnp.exp(log_a)
    u = jnp.sqrt(1.0 - jnp.exp(2.0 * log_a)) * (ig * xc)

    n_sub = tt // SUBLANES
    a3 = a.reshape(n_sub, SUBLANES, width)
    u3 = u.reshape(n_sub, SUBLANES, width)
    sub = lax.broadcasted_iota(jnp.int32, (n_sub, SUBLANES, width), 1)
    s = 1
    while s < SUBLANES:
        keep = sub >= s
        a_prev = pltpu.roll(a3, s, 1)
        u_prev = pltpu.roll(u3, s, 1)
        u3 = jnp.where(keep, a3 * u_prev + u3, u3)
        a3 = jnp.where(keep, a3 * a_prev, a3)
        s *= 2
    carry = h_ref[...]
    groups = []
    for gi in range(n_sub):
        hg = u3[gi] + a3[gi] * carry
        groups.append(hg)
        carry = hg[SUBLANES - 1:]
    h = jnp.concatenate(groups, axis=0)
    h_ref[...] = carry

    ga = ga_ref[...]
    gelu = 0.5 * ga * (1.0 + jnp.tanh(math.sqrt(2.0 / math.pi) * (ga + 0.044715 * (ga * ga * ga))))
    o_ref[...] = (h * gelu).astype(o_ref.dtype)


def _rglru(p, cw, cb, wr, br, wi, bi, lam, bsz, seq, tt):
    n = p.shape[0]
    nt = seq // tt
    w = A_WIDTH
    vec = lambda: pl.BlockSpec((1, w), lambda b, t: (0, 0))
    return pl.pallas_call(
        _rglru_kernel,
        grid=(bsz, nt),
        in_specs=[pl.BlockSpec((tt, w), lambda b, t: (b * nt + t, 0)),
                  pl.BlockSpec((tt, w), lambda b, t: (b * nt + t, 1)),
                  pl.BlockSpec((A_CONV, w), lambda b, t: (0, 0)), vec(),
                  pl.BlockSpec(wr.shape, lambda b, t: (0, 0, 0)), vec(),
                  pl.BlockSpec(wi.shape, lambda b, t: (0, 0, 0)), vec(), vec()],
        out_specs=pl.BlockSpec((tt, w), lambda b, t: (b * nt + t, 0)),
        out_shape=jax.ShapeDtypeStruct((n, w), BF16),
        scratch_shapes=[pltpu.VMEM((SUBLANES, w), F32), pltpu.VMEM((1, w), F32)],
        compiler_params=_params(("arbitrary", "arbitrary")),
        name="rglru",
    )(p, p, cw, cb, wr, br, wi, bi, lam)


def _alternate(*stage_streams):
    results = [None] * len(stage_streams)
    live = list(range(len(stage_streams)))
    while live:
        for idx in list(live):
            try:
                next(stage_streams[idx])
            except StopIteration as stop:
                results[idx] = stop.value
                live.remove(idx)
    return results


def _rwkv_kernel(r_ref, k_ref, v_ref, sm_ref, mur_ref, muk_ref, muv_ref, mus_ref, w0_ref, wup_ref, a0_ref,
                 aup_ref, gup_ref, kkw_ref, kaw_ref, rkw_ref, lnw_ref, lnb_ref, o_ref,
                 s_ref, pr_ref, pk_ref, pv_ref, ps_ref):
    tt = r_ref.shape[0]
    L = RWKV_CHUNK
    gw = RWKV_GROUP
    n_grp = B_WIDTH // gw
    hr = tt // 2

    @pl.when(pl.program_id(1) == 0)
    def _():
        s_ref[...] = jnp.zeros_like(s_ref)
        pr_ref[...] = jnp.zeros_like(pr_ref)
        pk_ref[...] = jnp.zeros_like(pk_ref)
        pv_ref[...] = jnp.zeros_like(pv_ref)
        ps_ref[...] = jnp.zeros_like(ps_ref)

    def lerp(x_ref, p_ref, mu_ref):
        x = x_ref[...]
        xs = _shift_rows(x, 1, p_ref[...])
        p_ref[...] = x[tt - SUBLANES:]
        return x + (xs - x) * mu_ref[...]

    r_all = lerp(r_ref, pr_ref, mur_ref)
    k_all = lerp(k_ref, pk_ref, muk_ref)
    v_all = lerp(v_ref, pv_ref, muv_ref)
    sm_all = lerp(sm_ref, ps_ref, mus_ref)[:, :B_SMALL_K]

    row_g = lax.broadcasted_iota(jnp.int32, (gw, gw), 0)
    col_g = lax.broadcasted_iota(jnp.int32, (gw, gw), 1)
    same_head = (row_g // B_HEAD) == (col_g // B_HEAD)
    head_mask = jnp.where(same_head, 1.0, 0.0)
    head_mask_bf = head_mask.astype(BF16)

    def head_sum(x, terms):
        n = x.shape[0]
        xs = jnp.concatenate([x[:, gi * gw:(gi + 1) * gw] for gi in range(n_grp)], axis=0)
        s = _dot(jnp.concatenate(_split_bf16(xs, terms), axis=0), head_mask_bf)
        s = sum(s[t * n_grp * n:(t + 1) * n_grp * n] for t in range(terms))
        return jnp.concatenate([s[gi * n:(gi + 1) * n] for gi in range(n_grp)], axis=1)

    def bd(x):
        return jnp.concatenate([x.astype(BF16)] * (gw // B_HEAD), axis=0) * head_mask_bf

    row_t = lax.broadcasted_iota(jnp.int32, (hr, hr), 0)
    col_t = lax.broadcasted_iota(jnp.int32, (hr, hr), 1)
    tril = jnp.where((row_t >= col_t) & ((row_t // L) == (col_t // L)), 1.0, 0.0).astype(BF16)
    row_p = lax.broadcasted_iota(jnp.int32, (L, gw), 0)
    src_p = lax.broadcasted_iota(jnp.int32, (L, gw), 1) % B_HEAD
    strict_lower = row_p > src_p
    lower = row_p >= src_p
    n_doublings = int(math.log2(L)) - 1
    groups = range(n_grp)
    half_units = [(c, gi) for c in range(hr // L) for gi in groups]

    def tile(arr, c, gi):
        return arr[c * L:(c + 1) * L, gi * gw:(gi + 1) * gw]

    def prepare(h):
        rows = slice(h * hr, (h + 1) * hr)
        r, k, v, sm = r_all[rows], k_all[rows], v_all[rows], sm_all[rows]
        z = w0_ref[...] + _dot(jnp.tanh(sm).astype(BF16), wup_ref[...])
        log_w = (-math.exp(-0.5)) * _sigmoid(z)
        yield
        a = _sigmoid(a0_ref[...] + _dot(sm.astype(BF16), aup_ref[...]))
        yield
        g = _dot(_sigmoid(sm).astype(BF16), gup_ref[...])
        kk = k * kkw_ref[...]
        yield
        kk = kk * lax.rsqrt(jnp.maximum(head_sum(kk * kk, 2), 1e-12))
        yield
        k2 = k * (1.0 + (a - 1.0) * kaw_ref[...])
        yield
        cum = sum(_dot(tril, part) for part in _split_bf16(log_w, 3))
        yield
        p_in = jnp.exp(cum)
        p_inv = jnp.exp(-cum)
        yield
        a_bar = (-kk) * jnp.exp(cum - log_w)
        r_bar = r * p_in
        yield
        b_bar = kk * a * p_inv
        k_bar = k2 * p_inv
        return dict(r=r, v=v, g=g, k2=k2, p_in=p_in, a_bar=a_bar, r_bar=r_bar, b_bar=b_bar, k_bar=k_bar)

    def products(pre, states):
        units = half_units
        ar = {u: jnp.concatenate([tile(pre["a_bar"], *u), tile(pre["r_bar"], *u)], axis=0).astype(BF16)
              for u in units}
        m_b = {u: _dot_nt(ar[u], bd(tile(pre["b_bar"], *u))) for u in units}
        yield
        m_k = {u: _dot_nt(ar[u], bd(tile(pre["k_bar"], *u))) for u in units}
        yield
        x = {u: jnp.where(strict_lower, m_b[u][:L], 0.0) for u in units}
        a_rb = {u: jnp.where(lower, m_b[u][L:], 0.0).astype(BF16) for u in units}
        akrk = {u: jnp.concatenate([jnp.where(strict_lower, m_k[u][:L], 0.0),
                                    jnp.where(lower, m_k[u][L:], 0.0)], axis=0).astype(BF16) for u in units}
        cy = {u: _dot(akrk[u], bd(tile(pre["v"], *u))) for u in units}
        yield
        n_inv = dict(x)
        x_pow = {u: _dot(x[u].astype(BF16), bd(x[u])) for u in units}
        yield
        for step in range(n_doublings):
            if step + 1 < n_doublings:
                both = {u: _dot(jnp.concatenate([x_pow[u], n_inv[u]], axis=0).astype(BF16), bd(x_pow[u]))
                        for u in units}
                n_inv = {u: n_inv[u] + x_pow[u] + both[u][L:] for u in units}
                x_pow = {u: both[u][:L] for u in units}
            else:
                n_inv = {u: n_inv[u] + x_pow[u] + _dot(n_inv[u].astype(BF16), bd(x_pow[u])) for u in units}
            yield
        n_inv = {u: n_inv[u].astype(BF16) for u in units}
        y_rows = []
        for c in range(hr // L):
            p_last = pre["p_in"][(c + 1) * L - 1:(c + 1) * L]
            pl_g = [p_last[:, gi * gw:(gi + 1) * gw] for gi in groups]
            bk = [jnp.concatenate([tile(pre["b_bar"], c, gi) * pl_g[gi], tile(pre["k_bar"], c, gi) * pl_g[gi]],
                                  axis=0).astype(BF16) for gi in groups]
            ah = [_dot_nt(ar[c, gi], states[gi].astype(BF16)) for gi in groups]
            yield
            rhs = [ah[gi][:L] + cy[c, gi][:L] for gi in groups]
            u_c = [rhs[gi] + _dot(n_inv[c, gi], bd(rhs[gi])) for gi in groups]
            yield
            ds = [_dot_tn(jnp.concatenate([u_c[gi], tile(pre["v"], c, gi)], axis=0).astype(BF16), bk[gi])
                  for gi in groups]
            states = [states[gi] * pl_g[gi] + head_mask * ds[gi] for gi in groups]
            y_rows.append(jnp.concatenate(
                [ah[gi][L:] + cy[c, gi][L:] + _dot(a_rb[c, gi], bd(u_c[gi])) for gi in groups], axis=1))
            yield
        return jnp.concatenate(y_rows, axis=0), states

    def finish(h, pre, y):
        inv_n = 1.0 / B_HEAD
        yc = y - head_sum(y, 1) * inv_n
        yield
        var = head_sum(yc * yc, 1) * inv_n
        yield
        yn = yc * lax.rsqrt(var + B_LN_EPS) * lnw_ref[...] + lnb_ref[...]
        yield
        bonus = head_sum(pre["r"] * pre["k2"] * rkw_ref[...], 1) * pre["v"]
        yield
        o_ref[pl.ds(h * hr, hr), :] = ((yn + bonus) * pre["g"]).astype(o_ref.dtype)

    states = [s_ref[gi] for gi in groups]
    (pre0,) = _alternate(prepare(0))
    (y0, states), pre1 = _alternate(products(pre0, states), prepare(1))
    (y1, states), _ = _alternate(products(pre1, states), finish(0, pre0, y0))
    _alternate(finish(1, pre1, y1))
    for gi in groups:
        s_ref[gi] = states[gi]


def _rwkv(p, mur, muk, muv, mus, w0, wup, a0, aup, gup, kkw, kaw, rkw, lnw, lnb, bsz, seq):
    n = p.shape[0]
    tt = RWKV_ROWS
    nt = seq // tt
    w = B_WIDTH
    col0 = 2 * A_WIDTH // w
    vec = lambda: pl.BlockSpec((1, w), lambda b, t: (0, 0))
    mat = lambda: pl.BlockSpec((B_SMALL_K, w), lambda b, t: (0, 0))
    return pl.pallas_call(
        _rwkv_kernel,
        grid=(bsz, nt),
        in_specs=[pl.BlockSpec((tt, w), lambda b, t: (b * nt + t, col0)),
                  pl.BlockSpec((tt, w), lambda b, t: (b * nt + t, col0 + 1)),
                  pl.BlockSpec((tt, w), lambda b, t: (b * nt + t, col0 + 2)),
                  pl.BlockSpec((tt, B_SMALL), lambda b, t: (b * nt + t, (2 * A_WIDTH + 3 * w) // B_SMALL)),
                  vec(), vec(), vec(), pl.BlockSpec((1, B_SMALL), lambda b, t: (0, 0)),
                  vec(), mat(), vec(), mat(), mat(), vec(), vec(), vec(), vec(), vec()],
        out_specs=pl.BlockSpec((tt, w), lambda b, t: (b * nt + t, 0)),
        out_shape=jax.ShapeDtypeStruct((n, w), BF16),
        scratch_shapes=[pltpu.VMEM((w // RWKV_GROUP, RWKV_GROUP, RWKV_GROUP), F32),
                        pltpu.VMEM((SUBLANES, w), F32), pltpu.VMEM((SUBLANES, w), F32),
                        pltpu.VMEM((SUBLANES, w), F32), pltpu.VMEM((SUBLANES, B_SMALL), F32)],
        compiler_params=_params(("arbitrary", "arbitrary")),
        name="rwkv7",
    )(p, p, p, p, mur, muk, muv, mus, w0, wup, a0, aup, gup, kkw, kaw, rkw, lnw, lnb)


def _mlstm_in_kernel(x_ref, g_ref, w_ref, cw_ref, cb_ref, wq_ref, wk_ref, wv_ref, wif_ref, bif_ref,
                     q_ref, k_ref, v_ref, xc_ref, gates_ref, hn_ref, prev_ref, *, tiles_per_seq, sub_rows):
    i = pl.program_id(0)
    j = pl.program_id(1)
    tm = x_ref.shape[0]
    cb = w_ref.shape[1]
    n_grp = cb // MXU_DIM

    def blockdiag(xb, wb_ref):
        return jnp.concatenate(
            [_dot(xb[:, g * MXU_DIM:(g + 1) * MXU_DIM], wb_ref[g]) for g in range(n_grp)], axis=1)

    def body(first_col_tile):
        def project(s):
            rows = pl.ds(s * sub_rows, sub_rows)
            if first_col_tile:
                hn = _rms(x_ref[rows, :], g_ref[...]).astype(BF16)
                hn_ref[rows, :] = hn
            else:
                hn = hn_ref[rows, :]
            return _dot(hn, w_ref[...])

        seq_start = (i % tiles_per_seq) == 0
        prev8 = jnp.where(seq_start, 0.0, prev_ref[j])
        cw = cw_ref[...]
        n_sub = tm // sub_rows
        xm_next = project(0)
        for s in range(n_sub):
            rows = pl.ds(s * sub_rows, sub_rows)
            xm = xm_next
            if s + 1 < n_sub:
                xm_next = project(s + 1)
            conv = cb_ref[...] + xm * cw[C_CONV - 1:C_CONV]
            for kk in range(1, C_CONV):
                conv = conv + _shift_rows(xm, kk, prev8) * cw[C_CONV - 1 - kk:C_CONV - kk]
            prev8 = xm[sub_rows - SUBLANES:]
            xc = _silu(conv)
            xcb = xc.astype(BF16)
            xc_ref[rows, :] = xcb
            q = blockdiag(xcb, wq_ref)
            k = blockdiag(xcb, wk_ref)
            v = blockdiag(xm.astype(BF16), wv_ref)
            qb = q.astype(BF16)
            kb = k.astype(BF16)
            vb = v.astype(BF16)
            q_ref[rows, :] = qb
            k_ref[rows, :] = (k * (C_HEAD ** -0.5)).astype(BF16)
            v_ref[rows, :] = vb
            gate_part = _dot(qb, wif_ref[0]) + _dot(kb, wif_ref[1]) + _dot(vb, wif_ref[2])
            if first_col_tile:
                gates_ref[rows, :] = bif_ref[...] + gate_part
            else:
                gates_ref[rows, :] += gate_part
        prev_ref[j] = prev8

    pl.when(j == 0)(functools.partial(body, True))
    pl.when(j != 0)(functools.partial(body, False))


def _mlstm_in(x, g, w_in, cw, cb, wq, wk, wv, wif, bif, seq, tm, cblk, sub_rows):
    n, d = x.shape
    ncb = C_WIDTH // cblk
    gpb = cblk // MXU_DIM
    blk = lambda: pl.BlockSpec((tm, cblk), lambda i, j: (i, j))
    wspec = lambda: pl.BlockSpec((gpb, MXU_DIM, MXU_DIM), lambda i, j: (j, 0, 0))
    act = lambda dt: jax.ShapeDtypeStruct((n, C_WIDTH), dt)
    kern = functools.partial(_mlstm_in_kernel, tiles_per_seq=seq // tm, sub_rows=sub_rows)
    return pl.pallas_call(
        kern,
        grid=(n // tm, ncb),
        in_specs=[pl.BlockSpec((tm, d), lambda i, j: (i, 0)),
                  pl.BlockSpec((1, d), lambda i, j: (0, 0)),
                  pl.BlockSpec((d, cblk), lambda i, j: (0, j)),
                  pl.BlockSpec((C_CONV, cblk), lambda i, j: (0, j)),
                  pl.BlockSpec((1, cblk), lambda i, j: (0, j)),
                  wspec(), wspec(), wspec(),
                  pl.BlockSpec((3, cblk, 2 * C_HEADS), lambda i, j: (0, j, 0)),
                  pl.BlockSpec((1, 2 * C_HEADS), lambda i, j: (0, 0))],
        out_specs=[blk(), blk(), blk(), blk(),
                   pl.BlockSpec((tm, 2 * C_HEADS), lambda i, j: (i, 0))],
        out_shape=[act(BF16), act(BF16), act(BF16), act(BF16),
                   jax.ShapeDtypeStruct((n, 2 * C_HEADS), F32)],
        scratch_shapes=[pltpu.VMEM((tm, d), BF16), pltpu.VMEM((ncb, SUBLANES, cblk), F32)],
        compiler_params=_params(("arbitrary", "arbitrary")),
        name="mlstm_in",
    )(x, g, w_in, cw, cb, wq, wk, wv, wif, bif)


def _mlstm_gate_kernel(f_ref, b_ref):
    L = f_ref.shape[1]
    f = f_ref[...]
    lf = jnp.minimum(f, 0.0) - jnp.log1p(jnp.exp(-jnp.abs(f)))
    row = lax.broadcasted_iota(jnp.int32, (L, L), 0)
    col = lax.broadcasted_iota(jnp.int32, (L, L), 1)
    triu = jnp.where(row <= col, 1.0, 0.0).astype(BF16)
    b_ref[...] = sum(_dot(part, triu) for part in _split_bf16(lf, 3))


def _mlstm_gates(f_pre, chunk):
    rows, seq = f_pre.shape
    return pl.pallas_call(
        _mlstm_gate_kernel,
        grid=(seq // chunk,),
        in_specs=[pl.BlockSpec((rows, chunk), lambda c: (0, c))],
        out_specs=pl.BlockSpec((rows, chunk), lambda c: (0, c)),
        out_shape=jax.ShapeDtypeStruct((rows, seq), F32),
        compiler_params=_params(("arbitrary",)),
        name="mlstm_gates",
    )(f_pre)


def _mlstm_kernel(q_ref, k_ref, v_ref, xc_ref, za_ref, ir_ref, br_ref, lnw_ref, skip_ref,
                  o_ref, ct_ref, m_ref):
    c = pl.program_id(2)
    L = q_ref.shape[0]
    n_heads, hd = ct_ref.shape[0], ct_ref.shape[1]

    @pl.when(c == 0)
    def _():
        ct_ref[...] = jnp.zeros_like(ct_ref)
        m_ref[...] = jnp.zeros_like(m_ref)

    def lanes(x, width):
        return jnp.concatenate([x] * (width // LANES), axis=1)

    row = lax.broadcasted_iota(jnp.int32, (L, L), 0)
    col = lax.broadcasted_iota(jnp.int32, (L, L), 1)
    causal = row >= col
    eye = jnp.where(row == col, 1.0, 0.0).astype(BF16)

    def to_col(x_row):
        return sum(_dot_nt(eye, jnp.broadcast_to(part, (LANES, L))) for part in _split_bf16(x_row, 3))

    heads = range(n_heads)
    cols = [slice(i * hd, (i + 1) * hd) for i in heads]
    q = [q_ref[:, cols[i]] for i in heads]
    k = [k_ref[:, cols[i]] for i in heads]
    v_aug = [jnp.concatenate([v_ref[:, cols[i]], jnp.ones((L, LANES), BF16)], axis=1) for i in heads]
    li_row = [ir_ref[i] for i in heads]
    b_row = [br_ref[i] for i in heads]
    m_prev = [m_ref[i] for i in heads]

    qk = [_dot_nt(q[i], k[i]) for i in heads]
    q_ct = [_dot(q[i], ct_ref[i].astype(BF16)) for i in heads]
    b_col = [to_col(b_row[i]) for i in heads]
    li_col = [to_col(li_row[i]) for i in heads]
    b_last = [b_col[i][L - 1:L] for i in heads]

    d_log = [jnp.where(causal, lanes(b_col[i], L) - b_row[i] + li_row[i], -jnp.inf) for i in heads]
    inter = [b_col[i] + m_prev[i] for i in heads]
    m_t = [jnp.maximum(inter[i], jnp.max(d_log[i], axis=-1, keepdims=True)) for i in heads]
    s = [(qk[i] * jnp.exp(d_log[i] - lanes(m_t[i], L))).astype(BF16) for i in heads]
    sc = [jnp.exp(inter[i] - m_t[i]) for i in heads]
    num_den = [_dot(s[i], v_aug[i]) + lanes(sc[i], hd + LANES) * q_ct[i] for i in heads]

    g_log = [b_last[i] - b_col[i] + li_col[i] for i in heads]
    m_new = [jnp.maximum(b_last[i] + m_prev[i], jnp.max(g_log[i], axis=0, keepdims=True)) for i in heads]
    ke = [k[i] * lanes(jnp.exp(g_log[i] - m_new[i]).astype(BF16), hd) for i in heads]
    decay = [jnp.exp(b_last[i] + m_prev[i] - m_new[i]) for i in heads]
    for i in heads:
        ct_ref[i] = lanes(decay[i], hd + LANES) * ct_ref[i] + _dot_tn(ke[i], v_aug[i])
        m_ref[i] = m_new[i]

    for i in heads:
        inv = 1.0 / jnp.maximum(jnp.abs(num_den[i][:, hd:]), jnp.exp(-m_t[i]))
        h = num_den[i][:, :hd] * lanes(inv, hd)
        hc = h - jnp.mean(h, axis=-1, keepdims=True)
        hn = hc * lax.rsqrt(jnp.mean(hc * hc, axis=-1, keepdims=True) + EPS) * lnw_ref[:, cols[i]]
        hs = hn + skip_ref[:, cols[i]] * xc_ref[:, cols[i]].astype(F32)
        o_ref[:, cols[i]] = (hs * za_ref[:, cols[i]].astype(F32)).astype(o_ref.dtype)


def _mlstm(q, k, v, xc, z_act, i_row, b_row, lnw, skip, bsz, seq):
    n = q.shape[0]
    L = MLSTM_CHUNK
    nc = seq // L
    hd = C_HEAD
    hp = MLSTM_HEADS_PER_STEP
    blk = lambda: pl.BlockSpec((L, hp * hd), lambda b, h, c: (b * nc + c, h))
    rowspec = lambda: pl.BlockSpec((None, hp, 1, L), lambda b, h, c: (b, h, 0, c))
    vec = lambda: pl.BlockSpec((1, hp * hd), lambda b, h, c: (0, h))
    return pl.pallas_call(
        _mlstm_kernel,
        grid=(bsz, C_HEADS // hp, nc),
        in_specs=[blk(), blk(), blk(), blk(), blk(), rowspec(), rowspec(), vec(), vec()],
        out_specs=blk(),
        out_shape=jax.ShapeDtypeStruct((n, C_WIDTH), BF16),
        scratch_shapes=[pltpu.VMEM((hp, hd, hd + LANES), F32), pltpu.VMEM((hp, 1, LANES), F32)],
        compiler_params=_params(("arbitrary", "arbitrary", "arbitrary")),
        name="mlstm",
    )(q, k, v, xc, z_act, i_row, b_row, lnw, skip)


def _pack_block_diag(w, tile):
    g, bs, _ = w.shape
    per = tile // bs
    w = w.reshape(g // per, per, bs, bs)
    eye = jnp.eye(per, dtype=w.dtype)
    dense = jnp.einsum("npij,pq->npiqj", w, eye)
    return dense.reshape(g // per, tile, tile)


def _row(v):
    return v.reshape(1, -1)


def _even_layer(x, bsz, seq, norm, w_in, a_conv_w, a_conv_b, a_w_r, a_b_r, a_w_i, a_b_i, a_lambda,
                b_mu, b_w0, b_w_up, b_a0, b_a_up, b_g_up, b_k_k, b_k_a, b_r_k, b_ln_w, b_ln_b, w_out):
    main_w = 2 * A_WIDTH + 3 * B_WIDTH
    n_small = B_DECAY_RANK + B_AAA_RANK + B_GATE_RANK
    pad = B_SMALL - n_small
    assert main_w % B_SMALL == 0
    w_all = jnp.pad(w_in, ((0, 0), (0, pad))).astype(BF16)
    p = _norm_matmul(x, _row(norm), w_all, PROJ_TM, PROJ_TN, "even_in")

    ya = _rglru(p, a_conv_w, _row(a_conv_b),
                _pack_block_diag(a_w_r, MXU_DIM).astype(BF16), _row(a_b_r),
                _pack_block_diag(a_w_i, MXU_DIM).astype(BF16), _row(a_b_i), _row(a_lambda),
                bsz, seq, RGLRU_ROWS)

    mur, muk, muv = (_row(b_mu[i * B_WIDTH:(i + 1) * B_WIDTH]) for i in range(3))
    mus = _row(jnp.pad(b_mu[3 * B_WIDTH:], (0, pad)))

    def rows_at(w, start):
        out = jnp.zeros((B_SMALL_K, B_WIDTH), F32)
        return lax.dynamic_update_slice(out, w, (start, 0)).astype(BF16)

    wup = rows_at(b_w_up, 0)
    aup = rows_at(b_a_up, B_DECAY_RANK)
    gup = rows_at(b_g_up, B_DECAY_RANK + B_AAA_RANK)
    yb = _rwkv(p, mur, muk, muv, mus, _row(b_w0), wup, _row(b_a0), aup, gup, _row(b_k_k), _row(b_k_a),
               _row(b_r_k), _row(b_ln_w), _row(b_ln_b), bsz, seq)

    wo = w_out.astype(BF16)
    return _resid_matmul(x, [ya, yb], [wo[:A_WIDTH], wo[A_WIDTH:]], PROJ_TM, PROJ_TN_WIDE, "even_out")


def _odd_layer(x, bsz, seq, norm, w_in, conv_w, conv_b, w_q, w_k, w_v, w_if, b_if, ln_w, skip, w_out):
    w_in_b = w_in.astype(BF16)
    g = _row(norm)
    z_act = _norm_matmul(x, g, w_in_b, PROJ_TM, PROJ_TN_WIDE, "odd_in_z", col_start=C_WIDTH, silu_bf16=True)
    q, k, v, xc, gates = _mlstm_in(
        x, g, w_in_b, conv_w, _row(conv_b),
        _pack_block_diag(w_q, MXU_DIM).astype(BF16), _pack_block_diag(w_k, MXU_DIM).astype(BF16),
        _pack_block_diag(w_v, MXU_DIM).astype(BF16), w_if.astype(BF16), _row(b_if), seq,
        PROJ_TM, MLSTM_IN_COLS, MLSTM_IN_SUB)
    gt = jnp.transpose(gates.reshape(bsz, seq, 2, C_HEADS), (2, 0, 3, 1))
    i_pre = gt[0]
    b_cum = _mlstm_gates(gt[1].reshape(bsz * C_HEADS, seq), MLSTM_CHUNK).reshape(bsz, C_HEADS, seq)
    hs = _mlstm(q, k, v, xc, z_act, i_pre[:, :, None, :], b_cum[:, :, None, :], _row(ln_w), _row(skip),
                bsz, seq)
    return _resid_matmul(x, [hs], [w_out.astype(BF16)], PROJ_TM, PROJ_TN_WIDE, "odd_out")


def kernel(x, even_norm, even_w_in, a_conv_w, a_conv_b, a_w_r, a_b_r, a_w_i, a_b_i, a_lambda, b_mu, b_w0, b_w_up, b_a0, b_a_up, b_g_up, b_k_k, b_k_a, b_r_k, b_ln_w, b_ln_b, even_w_out, odd_norm, odd_w_in, c_conv_w, c_conv_b, c_w_q, c_w_k, c_w_v, c_w_if, c_b_if, c_ln_w, c_skip, odd_w_out, ffn_norm, ffn_w_gate, ffn_w_up, ffn_conv_w, ffn_conv_b, ffn_w_down, final_norm):
    bsz, seq, d = x.shape
    depth = ffn_norm.shape[0]
    h = x.reshape(bsz * seq, d)
    for layer in range(depth):
        if layer % 2 == 0:
            e = layer // 2
            h = _even_layer(h, bsz, seq, even_norm[e], even_w_in[e], a_conv_w[e], a_conv_b[e], a_w_r[e],
                            a_b_r[e], a_w_i[e], a_b_i[e], a_lambda[e], b_mu[e], b_w0[e], b_w_up[e], b_a0[e],
                            b_a_up[e], b_g_up[e], b_k_k[e], b_k_a[e], b_r_k[e].reshape(-1), b_ln_w[e],
                            b_ln_b[e], even_w_out[e])
        else:
            o = layer // 2
            h = _odd_layer(h, bsz, seq, odd_norm[o], odd_w_in[o], c_conv_w[o], c_conv_b[o], c_w_q[o], c_w_k[o],
                           c_w_v[o], c_w_if[o], c_b_if[o], c_ln_w[o], c_skip[o], odd_w_out[o])
        h = _ffn(h, _row(ffn_norm[layer]), ffn_w_gate[layer].astype(BF16), ffn_w_up[layer].astype(BF16),
                 ffn_conv_w[layer], _row(ffn_conv_b[layer]), ffn_w_down[layer].astype(BF16),
                 _row(final_norm), seq, FFN_TM, FFN_TF, layer == depth - 1, "ffn%d" % layer)
    return h.reshape(bsz, seq, d)
```

```python
import functools
import math

import jax
import jax.numpy as jnp
from jax import lax
from jax.experimental import pallas as pl
from jax.experimental.pallas import tpu as pltpu

F32 = jnp.float32
BF16 = jnp.bfloat16

EPS = 1e-6
D_MODEL = 2048
A_WIDTH = 1024
A_BLOCKS = 8
A_CONV = 4
LRU_C = 8.0
B_WIDTH = 1024
B_HEAD = 64
B_DECAY_RANK = 64
B_AAA_RANK = 64
B_GATE_RANK = 160
B_SMALL = 512
B_SMALL_K = 384
B_LN_EPS = 64e-5
C_WIDTH = 4096
C_HEADS = 8
C_HEAD = 512
C_QKV_BLOCK = 4
C_CONV = 4
D_FF = 5632
FFN_CONV = 3

SUBLANES = 8
LANES = 128
MXU_DIM = 256
VMEM_LIMIT = 56 * 1024 * 1024

RWKV_CHUNK = 64
RWKV_GROUP = 4 * B_HEAD
RWKV_ROWS = 256
MLSTM_CHUNK = 256
MLSTM_HEADS_PER_STEP = 4
PROJ_TM, PROJ_TN = 1024, 512
NORM_SUB_ROWS = 256
PROJ_TN_WIDE = 1024
FFN_TM, FFN_TF = 512, 512
RGLRU_ROWS = 256
MLSTM_IN_COLS, MLSTM_IN_SUB = 512, 256


def _params(sem):
    return pltpu.CompilerParams(dimension_semantics=sem, vmem_limit_bytes=VMEM_LIMIT)


def _dot(a, b):
    return jnp.dot(a, b, preferred_element_type=F32)


def _dot_nt(a, b):
    return lax.dot_general(a, b, (((1,), (1,)), ((), ())), preferred_element_type=F32)


def _dot_tn(a, b):
    return lax.dot_general(a, b, (((0,), (0,)), ((), ())), preferred_element_type=F32)


def _split_bf16(x, terms):
    parts = []
    for _ in range(terms):
        p = x.astype(BF16)
        parts.append(p)
        x = x - p.astype(F32)
    return parts


def _sigmoid(x):
    return 1.0 / (1.0 + jnp.exp(-x))


def _silu(x):
    return x * _sigmoid(x)


def _rms(x, g):
    return x * lax.rsqrt(jnp.mean(x * x, axis=-1, keepdims=True) + EPS) * g


def _shift_rows(x, k, prev8):
    r = pltpu.roll(x, k, 0)
    fix = pltpu.roll(prev8, k, 0)
    row = lax.broadcasted_iota(jnp.int32, (SUBLANES, x.shape[1]), 0)
    head = jnp.where(row < k, fix, r[:SUBLANES])
    return jnp.concatenate([head, r[SUBLANES:]], axis=0)


def _norm_matmul_kernel(x_ref, g_ref, w_ref, o_ref, hn_ref, *, silu):
    j = pl.program_id(1)
    tm = x_ref.shape[0]

    def store(rows, y):
        o_ref[rows, :] = (_silu(y) if silu else y).astype(o_ref.dtype)

    @pl.when(j == 0)
    def _():
        for s in range(tm // NORM_SUB_ROWS):
            rows = pl.ds(s * NORM_SUB_ROWS, NORM_SUB_ROWS)
            hn = _rms(x_ref[rows, :], g_ref[...]).astype(BF16)
            hn_ref[rows, :] = hn
            store(rows, _dot(hn, w_ref[0]))

    @pl.when(j != 0)
    def _():
        store(pl.ds(0, tm), _dot(hn_ref[...], w_ref[j]))


def _norm_matmul(x, g, w_tiles, tm, name, silu_bf16=False):
    n, d = x.shape
    n_tiles, _, tn = w_tiles.shape
    return pl.pallas_call(
        functools.partial(_norm_matmul_kernel, silu=silu_bf16),
        grid=(n // tm, n_tiles),
        in_specs=[pl.BlockSpec((tm, d), lambda i, j: (i, 0)),
                  pl.BlockSpec((1, d), lambda i, j: (0, 0)),
                  pl.BlockSpec((n_tiles, d, tn), lambda i, j: (0, 0, 0), pipeline_mode=pl.Buffered(1))],
        out_specs=pl.BlockSpec((tm, tn), lambda i, j: (i, j)),
        out_shape=jax.ShapeDtypeStruct((n, n_tiles * tn), BF16 if silu_bf16 else F32),
        scratch_shapes=[pltpu.VMEM((tm, d), BF16)],
        compiler_params=_params(("arbitrary", "arbitrary")),
        name=name,
    )(x, g, w_tiles)


def _column_tiles(w, tn):
    d, n = w.shape
    return jnp.transpose(w.reshape(d, n // tn, tn), (1, 0, 2))


def _resid_matmul_kernel(*refs, n_in):
    x_ref = refs[0]
    a_refs = refs[1:1 + n_in]
    w_refs = refs[1 + n_in:1 + 2 * n_in]
    o_ref = refs[1 + 2 * n_in]
    acc = x_ref[...]
    for a_ref, w_ref in zip(a_refs, w_refs):
        acc = acc + _dot(a_ref[...], w_ref[...])
    o_ref[...] = acc


def _resid_matmul(x, acts, ws, tm, tn, name):
    n, d = x.shape
    n_in = len(acts)
    in_specs = [pl.BlockSpec((tm, tn), lambda i, j: (i, j))]
    in_specs += [pl.BlockSpec((tm, a.shape[1]), lambda i, j: (i, 0)) for a in acts]
    in_specs += [pl.BlockSpec((w.shape[0], tn), lambda i, j: (0, j)) for w in ws]
    return pl.pallas_call(
        functools.partial(_resid_matmul_kernel, n_in=n_in),
        grid=(n // tm, d // tn),
        in_specs=in_specs,
        out_specs=pl.BlockSpec((tm, tn), lambda i, j: (i, j)),
        out_shape=jax.ShapeDtypeStruct((n, d), F32),
        compiler_params=_params(("arbitrary", "arbitrary")),
        name=name,
    )(x, *acts, *ws)


def _ffn_kernel(x_ref, g_ref, wg_ref, wu_ref, cw_ref, cb_ref, wd_ref, fg_ref, o_ref, hn_ref, carry_ref,
                *, tiles_per_seq, final_norm):
    i = pl.program_id(0)
    j = pl.program_id(1)
    tm = x_ref.shape[0]

    @pl.when(j == 0)
    def _():
        x = x_ref[...]
        hn_ref[...] = _rms(x, g_ref[...]).astype(BF16)
        o_ref[...] = x

    hn = hn_ref[...]
    gate = _dot(hn, wg_ref[...])
    up = _dot(hn, wu_ref[...])
    seq_start = (i % tiles_per_seq) == 0
    prev8 = jnp.where(seq_start, 0.0, carry_ref[j])
    carry_ref[j] = gate[tm - SUBLANES:]
    cw = cw_ref[...]
    conv = (cb_ref[...] + gate * cw[2:3] + _shift_rows(gate, 1, prev8) * cw[1:2]
            + _shift_rows(gate, 2, prev8) * cw[0:1])
    u = (_silu(conv) * up).astype(BF16)
    o_ref[...] += _dot(u, wd_ref[...])

    if final_norm:
        @pl.when(j == pl.num_programs(1) - 1)
        def _():
            o_ref[...] = _rms(o_ref[...], fg_ref[...])


def _ffn(x, g, wg, wu, cw, cb, wd, fg, seq, tm, tf, final_norm, name):
    n, d = x.shape
    f = wg.shape[1]
    nf = f // tf
    kern = functools.partial(_ffn_kernel, tiles_per_seq=seq // tm, final_norm=final_norm)
    return pl.pallas_call(
        kern,
        grid=(n // tm, nf),
        in_specs=[pl.BlockSpec((tm, d), lambda i, j: (i, 0)),
                  pl.BlockSpec((1, d), lambda i, j: (0, 0)),
                  pl.BlockSpec((d, tf), lambda i, j: (0, j)),
                  pl.BlockSpec((d, tf), lambda i, j: (0, j)),
                  pl.BlockSpec((FFN_CONV, tf), lambda i, j: (0, j)),
                  pl.BlockSpec((1, tf), lambda i, j: (0, j)),
                  pl.BlockSpec((tf, d), lambda i, j: (j, 0)),
                  pl.BlockSpec((1, d), lambda i, j: (0, 0))],
        out_specs=pl.BlockSpec((tm, d), lambda i, j: (i, 0)),
        out_shape=jax.ShapeDtypeStruct((n, d), F32),
        scratch_shapes=[pltpu.VMEM((tm, d), BF16), pltpu.VMEM((nf, SUBLANES, tf), F32)],
        compiler_params=_params(("arbitrary", "arbitrary")),
        name=name,
    )(x, g, wg, wu, cw, cb, wd, fg)


def _rglru_kernel(xa_ref, ga_ref, cw_ref, cb_ref, wr_ref, br_ref, wi_ref, bi_ref, lam_ref, o_ref,
                  prev_ref, h_ref):
    t = pl.program_id(1)
    tt, width = xa_ref.shape

    @pl.when(t == 0)
    def _():
        prev_ref[...] = jnp.zeros_like(prev_ref)
        h_ref[...] = jnp.zeros_like(h_ref)

    x = xa_ref[...]
    prev8 = prev_ref[...]
    prev_ref[...] = x[tt - SUBLANES:]
    cw = cw_ref[...]
    xc = cb_ref[...] + x * cw[3:4]
    for k in range(1, A_CONV):
        xc = xc + _shift_rows(x, k, prev8) * cw[A_CONV - 1 - k:A_CONV - k]

    xb = xc.astype(BF16)
    n_grp = width // MXU_DIM
    r_pre = jnp.concatenate(
        [_dot(xb[:, g * MXU_DIM:(g + 1) * MXU_DIM], wr_ref[g]) for g in range(n_grp)], axis=1)
    i_pre = jnp.concatenate(
        [_dot(xb[:, g * MXU_DIM:(g + 1) * MXU_DIM], wi_ref[g]) for g in range(n_grp)], axis=1)
    r = _sigmoid(r_pre + br_ref[...])
    ig = _sigmoid(i_pre + bi_ref[...])
    neg_lam = -lam_ref[...]
    softplus = jnp.maximum(neg_lam, 0.0) + jnp.log1p(jnp.exp(-jnp.abs(neg_lam)))
    log_a = (-LRU_C) * r * softplus
    a = jnp.exp(log_a)
    u = jnp.sqrt(1.0 - jnp.exp(2.0 * log_a)) * (ig * xc)

    n_sub = tt // SUBLANES
    a3 = a.reshape(n_sub, SUBLANES, width)
    u3 = u.reshape(n_sub, SUBLANES, width)
    sub = lax.broadcasted_iota(jnp.int32, (n_sub, SUBLANES, width), 1)
    s = 1
    while s < SUBLANES:
        keep = sub >= s
        a_prev = pltpu.roll(a3, s, 1)
        u_prev = pltpu.roll(u3, s, 1)
        u3 = jnp.where(keep, a3 * u_prev + u3, u3)
        a3 = jnp.where(keep, a3 * a_prev, a3)
        s *= 2
    carry = h_ref[...]
    groups = []
    for gi in range(n_sub):
        hg = u3[gi] + a3[gi] * carry
        groups.append(hg)
        carry = hg[SUBLANES - 1:]
    h = jnp.concatenate(groups, axis=0)
    h_ref[...] = carry

    ga = ga_ref[...]
    gelu = 0.5 * ga * (1.0 + jnp.tanh(math.sqrt(2.0 / math.pi) * (ga + 0.044715 * (ga * ga * ga))))
    o_ref[...] = (h * gelu).astype(o_ref.dtype)


def _rglru(p, cw, cb, wr, br, wi, bi, lam, bsz, seq, tt):
    n = p.shape[0]
    nt = seq // tt
    w = A_WIDTH
    vec = lambda: pl.BlockSpec((1, w), lambda b, t: (0, 0))
    return pl.pallas_call(
        _rglru_kernel,
        grid=(bsz, nt),
        in_specs=[pl.BlockSpec((tt, w), lambda b, t: (b * nt + t, 0)),
                  pl.BlockSpec((tt, w), lambda b, t: (b * nt + t, 1)),
                  pl.BlockSpec((A_CONV, w), lambda b, t: (0, 0)), vec(),
                  pl.BlockSpec(wr.shape, lambda b, t: (0, 0, 0)), vec(),
                  pl.BlockSpec(wi.shape, lambda b, t: (0, 0, 0)), vec(), vec()],
        out_specs=pl.BlockSpec((tt, w), lambda b, t: (b * nt + t, 0)),
        out_shape=jax.ShapeDtypeStruct((n, w), BF16),
        scratch_shapes=[pltpu.VMEM((SUBLANES, w), F32), pltpu.VMEM((1, w), F32)],
        compiler_params=_params(("arbitrary", "arbitrary")),
        name="rglru",
    )(p, p, cw, cb, wr, br, wi, bi, lam)


def _alternate(*stage_streams):
    results = [None] * len(stage_streams)
    live = list(range(len(stage_streams)))
    while live:
        for idx in list(live):
            try:
                next(stage_streams[idx])
            except StopIteration as stop:
                results[idx] = stop.value
                live.remove(idx)
    return results


def _rwkv_kernel(r_ref, k_ref, v_ref, sm_ref, mur_ref, muk_ref, muv_ref, mus_ref, w0_ref, wup_ref, a0_ref,
                 aup_ref, gup_ref, kkw_ref, kaw_ref, rkw_ref, lnw_ref, lnb_ref, o_ref,
                 s_ref, pr_ref, pk_ref, pv_ref, ps_ref):
    tt = r_ref.shape[0]
    L = RWKV_CHUNK
    gw = RWKV_GROUP
    n_grp = B_WIDTH // gw
    hr = tt // 2

    @pl.when(pl.program_id(1) == 0)
    def _():
        s_ref[...] = jnp.zeros_like(s_ref)
        pr_ref[...] = jnp.zeros_like(pr_ref)
        pk_ref[...] = jnp.zeros_like(pk_ref)
        pv_ref[...] = jnp.zeros_like(pv_ref)
        ps_ref[...] = jnp.zeros_like(ps_ref)

    def lerp(x_ref, p_ref, mu_ref):
        x = x_ref[...]
        xs = _shift_rows(x, 1, p_ref[...])
        p_ref[...] = x[tt - SUBLANES:]
        return x + (xs - x) * mu_ref[...]

    r_all = lerp(r_ref, pr_ref, mur_ref)
    k_all = lerp(k_ref, pk_ref, muk_ref)
    v_all = lerp(v_ref, pv_ref, muv_ref)
    sm_all = lerp(sm_ref, ps_ref, mus_ref)[:, :B_SMALL_K]

    row_g = lax.broadcasted_iota(jnp.int32, (gw, gw), 0)
    col_g = lax.broadcasted_iota(jnp.int32, (gw, gw), 1)
    same_head = (row_g // B_HEAD) == (col_g // B_HEAD)
    head_mask = jnp.where(same_head, 1.0, 0.0)
    head_mask_bf = head_mask.astype(BF16)

    def head_sum(x, terms):
        n = x.shape[0]
        xs = jnp.concatenate([x[:, gi * gw:(gi + 1) * gw] for gi in range(n_grp)], axis=0)
        s = _dot(jnp.concatenate(_split_bf16(xs, terms), axis=0), head_mask_bf)
        s = sum(s[t * n_grp * n:(t + 1) * n_grp * n] for t in range(terms))
        return jnp.concatenate([s[gi * n:(gi + 1) * n] for gi in range(n_grp)], axis=1)

    def bd(x):
        return jnp.concatenate([x.astype(BF16)] * (gw // B_HEAD), axis=0) * head_mask_bf

    row_t = lax.broadcasted_iota(jnp.int32, (hr, hr), 0)
    col_t = lax.broadcasted_iota(jnp.int32, (hr, hr), 1)
    tril = jnp.where((row_t >= col_t) & ((row_t // L) == (col_t // L)), 1.0, 0.0).astype(BF16)
    row_p = lax.broadcasted_iota(jnp.int32, (L, gw), 0)
    src_p = lax.broadcasted_iota(jnp.int32, (L, gw), 1) % B_HEAD
    strict_lower = row_p > src_p
    lower = row_p >= src_p
    n_doublings = int(math.log2(L)) - 1
    groups = range(n_grp)
    half_units = [(c, gi) for c in range(hr // L) for gi in groups]

    def tile(arr, c, gi):
        return arr[c * L:(c + 1) * L, gi * gw:(gi + 1) * gw]

    def prepare(h):
        rows = slice(h * hr, (h + 1) * hr)
        r, k, v, sm = r_all[rows], k_all[rows], v_all[rows], sm_all[rows]
        z = w0_ref[...] + _dot(jnp.tanh(sm).astype(BF16), wup_ref[...])
        log_w = (-math.exp(-0.5)) * _sigmoid(z)
        yield
        a = _sigmoid(a0_ref[...] + _dot(sm.astype(BF16), aup_ref[...]))
        yield
        g = _dot(_sigmoid(sm).astype(BF16), gup_ref[...])
        kk = k * kkw_ref[...]
        yield
        kk = kk * lax.rsqrt(jnp.maximum(head_sum(kk * kk, 2), 1e-12))
        yield
        k2 = k * (1.0 + (a - 1.0) * kaw_ref[...])
        yield
        cum = sum(_dot(tril, part) for part in _split_bf16(log_w, 3))
        yield
        p_in = jnp.exp(cum)
        p_inv = jnp.exp(-cum)
        yield
        a_bar = (-kk) * jnp.exp(cum - log_w)
        r_bar = r * p_in
        yield
        b_bar = kk * a * p_inv
        k_bar = k2 * p_inv
        return dict(r=r, v=v, g=g, k2=k2, p_in=p_in, a_bar=a_bar, r_bar=r_bar, b_bar=b_bar, k_bar=k_bar)

    def products(pre, states):
        units = half_units
        ar = {u: jnp.concatenate([tile(pre["a_bar"], *u), tile(pre["r_bar"], *u)], axis=0).astype(BF16)
              for u in units}
        m_b = {u: _dot_nt(ar[u], bd(tile(pre["b_bar"], *u))) for u in units}
        yield
        m_k = {u: _dot_nt(ar[u], bd(tile(pre["k_bar"], *u))) for u in units}
        yield
        x = {u: jnp.where(strict_lower, m_b[u][:L], 0.0) for u in units}
        a_rb = {u: jnp.where(lower, m_b[u][L:], 0.0).astype(BF16) for u in units}
        akrk = {u: jnp.concatenate([jnp.where(strict_lower, m_k[u][:L], 0.0),
                                    jnp.where(lower, m_k[u][L:], 0.0)], axis=0).astype(BF16) for u in units}
        cy = {u: _dot(akrk[u], bd(tile(pre["v"], *u))) for u in units}
        yield
        n_inv = dict(x)
        x_pow = {u: _dot(x[u].astype(BF16), bd(x[u])) for u in units}
        yield
        for step in range(n_doublings):
            if step + 1 < n_doublings:
                both = {u: _dot(jnp.concatenate([x_pow[u], n_inv[u]], axis=0).astype(BF16), bd(x_pow[u]))
                        for u in units}
                n_inv = {u: n_inv[u] + x_pow[u] + both[u][L:] for u in units}
                x_pow = {u: both[u][:L] for u in units}
            else:
                n_inv = {u: n_inv[u] + x_pow[u] + _dot(n_inv[u].astype(BF16), bd(x_pow[u])) for u in units}
            yield
        n_inv = {u: n_inv[u].astype(BF16) for u in units}
        y_rows = []
        for c in range(hr // L):
            p_last = pre["p_in"][(c + 1) * L - 1:(c + 1) * L]
            pl_g = [p_last[:, gi * gw:(gi + 1) * gw] for gi in groups]
            bk = [jnp.concatenate([tile(pre["b_bar"], c, gi) * pl_g[gi], tile(pre["k_bar"], c, gi) * pl_g[gi]],
                                  axis=0).astype(BF16) for gi in groups]
            ah = [_dot_nt(ar[c, gi], states[gi].astype(BF16)) for gi in groups]
            yield
            rhs = [ah[gi][:L] + cy[c, gi][:L] for gi in groups]
            u_c = [rhs[gi] + _dot(n_inv[c, gi], bd(rhs[gi])) for gi in groups]
            yield
            ds = [_dot_tn(jnp.concatenate([u_c[gi], tile(pre["v"], c, gi)], axis=0).astype(BF16), bk[gi])
                  for gi in groups]
            states = [states[gi] * pl_g[gi] + head_mask * ds[gi] for gi in groups]
            y_rows.append(jnp.concatenate(
                [ah[gi][L:] + cy[c, gi][L:] + _dot(a_rb[c, gi], bd(u_c[gi])) for gi in groups], axis=1))
            yield
        return jnp.concatenate(y_rows, axis=0), states

    def finish(h, pre, y):
        inv_n = 1.0 / B_HEAD
        yc = y - head_sum(y, 1) * inv_n
        yield
        var = head_sum(yc * yc, 1) * inv_n
        yield
        yn = yc * lax.rsqrt(var + B_LN_EPS) * lnw_ref[...] + lnb_ref[...]
        yield
        bonus = head_sum(pre["r"] * pre["k2"] * rkw_ref[...], 1) * pre["v"]
        yield
        o_ref[pl.ds(h * hr, hr), :] = ((yn + bonus) * pre["g"]).astype(o_ref.dtype)

    states = [s_ref[gi] for gi in groups]
    (pre0,) = _alternate(prepare(0))
    (y0, states), pre1 = _alternate(products(pre0, states), prepare(1))
    (y1, states), _ = _alternate(products(pre1, states), finish(0, pre0, y0))
    _alternate(finish(1, pre1, y1))
    for gi in groups:
        s_ref[gi] = states[gi]


def _rwkv(p, mur, muk, muv, mus, w0, wup, a0, aup, gup, kkw, kaw, rkw, lnw, lnb, bsz, seq):
    n = p.shape[0]
    tt = RWKV_ROWS
    nt = seq // tt
    w = B_WIDTH
    col0 = 2 * A_WIDTH // w
    vec = lambda: pl.BlockSpec((1, w), lambda b, t: (0, 0))
    mat = lambda: pl.BlockSpec((B_SMALL_K, w), lambda b, t: (0, 0))
    return pl.pallas_call(
        _rwkv_kernel,
        grid=(bsz, nt),
        in_specs=[pl.BlockSpec((tt, w), lambda b, t: (b * nt + t, col0)),
                  pl.BlockSpec((tt, w), lambda b, t: (b * nt + t, col0 + 1)),
                  pl.BlockSpec((tt, w), lambda b, t: (b * nt + t, col0 + 2)),
                  pl.BlockSpec((tt, B_SMALL), lambda b, t: (b * nt + t, (2 * A_WIDTH + 3 * w) // B_SMALL)),
                  vec(), vec(), vec(), pl.BlockSpec((1, B_SMALL), lambda b, t: (0, 0)),
                  vec(), mat(), vec(), mat(), mat(), vec(), vec(), vec(), vec(), vec()],
        out_specs=pl.BlockSpec((tt, w), lambda b, t: (b * nt + t, 0)),
        out_shape=jax.ShapeDtypeStruct((n, w), BF16),
        scratch_shapes=[pltpu.VMEM((w // RWKV_GROUP, RWKV_GROUP, RWKV_GROUP), F32),
                        pltpu.VMEM((SUBLANES, w), F32), pltpu.VMEM((SUBLANES, w), F32),
                        pltpu.VMEM((SUBLANES, w), F32), pltpu.VMEM((SUBLANES, B_SMALL), F32)],
        compiler_params=_params(("arbitrary", "arbitrary")),
        name="rwkv7",
    )(p, p, p, p, mur, muk, muv, mus, w0, wup, a0, aup, gup, kkw, kaw, rkw, lnw, lnb)


def _mlstm_in_kernel(x_ref, g_ref, w_ref, cw_ref, cb_ref, wq_ref, wk_ref, wv_ref, wif_ref, bif_ref,
                     q_ref, k_ref, v_ref, xc_ref, gates_ref, hn_ref, prev_ref, *, tiles_per_seq, sub_rows):
    i = pl.program_id(0)
    j = pl.program_id(1)
    tm = x_ref.shape[0]
    cb = w_ref.shape[1]
    n_grp = cb // MXU_DIM

    def blockdiag(xb, wb_ref):
        return jnp.concatenate(
            [_dot(xb[:, g * MXU_DIM:(g + 1) * MXU_DIM], wb_ref[g]) for g in range(n_grp)], axis=1)

    def body(first_col_tile):
        def project(s):
            rows = pl.ds(s * sub_rows, sub_rows)
            if first_col_tile:
                hn = _rms(x_ref[rows, :], g_ref[...]).astype(BF16)
                hn_ref[rows, :] = hn
            else:
                hn = hn_ref[rows, :]
            return _dot(hn, w_ref[...])

        seq_start = (i % tiles_per_seq) == 0
        prev8 = jnp.where(seq_start, 0.0, prev_ref[j])
        cw = cw_ref[...]
        n_sub = tm // sub_rows
        xm_next = project(0)
        for s in range(n_sub):
            rows = pl.ds(s * sub_rows, sub_rows)
            xm = xm_next
            if s + 1 < n_sub:
                xm_next = project(s + 1)
            conv = cb_ref[...] + xm * cw[C_CONV - 1:C_CONV]
            for kk in range(1, C_CONV):
                conv = conv + _shift_rows(xm, kk, prev8) * cw[C_CONV - 1 - kk:C_CONV - kk]
            prev8 = xm[sub_rows - SUBLANES:]
            xc = _silu(conv)
            xcb = xc.astype(BF16)
            xc_ref[rows, :] = xcb
            q = blockdiag(xcb, wq_ref)
            k = blockdiag(xcb, wk_ref)
            v = blockdiag(xm.astype(BF16), wv_ref)
            qb = q.astype(BF16)
            kb = k.astype(BF16)
            vb = v.astype(BF16)
            q_ref[rows, :] = qb
            k_ref[rows, :] = (k * (C_HEAD ** -0.5)).astype(BF16)
            v_ref[rows, :] = vb
            gate_part = _dot(qb, wif_ref[0]) + _dot(kb, wif_ref[1]) + _dot(vb, wif_ref[2])
            if first_col_tile:
                gates_ref[rows, :] = bif_ref[...] + gate_part
            else:
                gates_ref[rows, :] += gate_part
        prev_ref[j] = prev8

    pl.when(j == 0)(functools.partial(body, True))
    pl.when(j != 0)(functools.partial(body, False))


def _mlstm_in(x, g, w_in, cw, cb, wq, wk, wv, wif, bif, seq, tm, cblk, sub_rows):
    n, d = x.shape
    ncb = C_WIDTH // cblk
    gpb = cblk // MXU_DIM
    blk = lambda: pl.BlockSpec((tm, cblk), lambda i, j: (i, j))
    wspec = lambda: pl.BlockSpec((gpb, MXU_DIM, MXU_DIM), lambda i, j: (j, 0, 0))
    act = lambda dt: jax.ShapeDtypeStruct((n, C_WIDTH), dt)
    kern = functools.partial(_mlstm_in_kernel, tiles_per_seq=seq // tm, sub_rows=sub_rows)
    return pl.pallas_call(
        kern,
        grid=(n // tm, ncb),
        in_specs=[pl.BlockSpec((tm, d), lambda i, j: (i, 0)),
                  pl.BlockSpec((1, d), lambda i, j: (0, 0)),
                  pl.BlockSpec((d, cblk), lambda i, j: (0, j)),
                  pl.BlockSpec((C_CONV, cblk), lambda i, j: (0, j)),
                  pl.BlockSpec((1, cblk), lambda i, j: (0, j)),
                  wspec(), wspec(), wspec(),
                  pl.BlockSpec((3, cblk, 2 * C_HEADS), lambda i, j: (0, j, 0)),
                  pl.BlockSpec((1, 2 * C_HEADS), lambda i, j: (0, 0))],
        out_specs=[blk(), blk(), blk(), blk(),
                   pl.BlockSpec((tm, 2 * C_HEADS), lambda i, j: (i, 0))],
        out_shape=[act(BF16), act(BF16), act(BF16), act(BF16),
                   jax.ShapeDtypeStruct((n, 2 * C_HEADS), F32)],
        scratch_shapes=[pltpu.VMEM((tm, d), BF16), pltpu.VMEM((ncb, SUBLANES, cblk), F32)],
        compiler_params=_params(("arbitrary", "arbitrary")),
        name="mlstm_in",
    )(x, g, w_in, cw, cb, wq, wk, wv, wif, bif)


def _mlstm_gate_kernel(f_ref, b_ref):
    L = f_ref.shape[1]
    f = f_ref[...]
    lf = jnp.minimum(f, 0.0) - jnp.log1p(jnp.exp(-jnp.abs(f)))
    row = lax.broadcasted_iota(jnp.int32, (L, L), 0)
    col = lax.broadcasted_iota(jnp.int32, (L, L), 1)
    triu = jnp.where(row <= col, 1.0, 0.0).astype(BF16)
    b_ref[...] = sum(_dot(part, triu) for part in _split_bf16(lf, 3))


def _mlstm_gates(f_pre, chunk):
    rows, seq = f_pre.shape
    return pl.pallas_call(
        _mlstm_gate_kernel,
        grid=(seq // chunk,),
        in_specs=[pl.BlockSpec((rows, chunk), lambda c: (0, c))],
        out_specs=pl.BlockSpec((rows, chunk), lambda c: (0, c)),
        out_shape=jax.ShapeDtypeStruct((rows, seq), F32),
        compiler_params=_params(("arbitrary",)),
        name="mlstm_gates",
    )(f_pre)


def _mlstm_kernel(q_ref, k_ref, v_ref, xc_ref, za_ref, ir_ref, br_ref, lnw_ref, skip_ref,
                  o_ref, ct_ref, m_ref):
    c = pl.program_id(2)
    L = q_ref.shape[0]
    n_heads, hd = ct_ref.shape[0], ct_ref.shape[1]

    @pl.when(c == 0)
    def _():
        ct_ref[...] = jnp.zeros_like(ct_ref)
        m_ref[...] = jnp.zeros_like(m_ref)

    def lanes(x, width):
        return jnp.concatenate([x] * (width // LANES), axis=1)

    row = lax.broadcasted_iota(jnp.int32, (L, L), 0)
    col = lax.broadcasted_iota(jnp.int32, (L, L), 1)
    causal = row >= col
    eye = jnp.where(row == col, 1.0, 0.0).astype(BF16)

    def to_col(x_row):
        return sum(_dot_nt(eye, jnp.broadcast_to(part, (LANES, L))) for part in _split_bf16(x_row, 3))

    heads = range(n_heads)
    cols = [slice(i * hd, (i + 1) * hd) for i in heads]
    q = [q_ref[:, cols[i]] for i in heads]
    k = [k_ref[:, cols[i]] for i in heads]
    v_aug = [jnp.concatenate([v_ref[:, cols[i]], jnp.ones((L, LANES), BF16)], axis=1) for i in heads]
    li_row = [ir_ref[i] for i in heads]
    b_row = [br_ref[i] for i in heads]
    m_prev = [m_ref[i] for i in heads]

    qk = [_dot_nt(q[i], k[i]) for i in heads]
    q_ct = [_dot(q[i], ct_ref[i].astype(BF16)) for i in heads]
    b_col = [to_col(b_row[i]) for i in heads]
    li_col = [to_col(li_row[i]) for i in heads]
    b_last = [b_col[i][L - 1:L] for i in heads]

    d_log = [jnp.where(causal, lanes(b_col[i], L) - b_row[i] + li_row[i], -jnp.inf) for i in heads]
    inter = [b_col[i] + m_prev[i] for i in heads]
    m_t = [jnp.maximum(inter[i], jnp.max(d_log[i], axis=-1, keepdims=True)) for i in heads]
    s = [(qk[i] * jnp.exp(d_log[i] - lanes(m_t[i], L))).astype(BF16) for i in heads]
    sc = [jnp.exp(inter[i] - m_t[i]) for i in heads]
    num_den = [_dot(s[i], v_aug[i]) + lanes(sc[i], hd + LANES) * q_ct[i] for i in heads]

    g_log = [b_last[i] - b_col[i] + li_col[i] for i in heads]
    m_new = [jnp.maximum(b_last[i] + m_prev[i], jnp.max(g_log[i], axis=0, keepdims=True)) for i in heads]
    ke = [k[i] * lanes(jnp.exp(g_log[i] - m_new[i]).astype(BF16), hd) for i in heads]
    decay = [jnp.exp(b_last[i] + m_prev[i] - m_new[i]) for i in heads]
    for i in heads:
        ct_ref[i] = lanes(decay[i], hd + LANES) * ct_ref[i] + _dot_tn(ke[i], v_aug[i])
        m_ref[i] = m_new[i]

    for i in heads:
        inv = 1.0 / jnp.maximum(jnp.abs(num_den[i][:, hd:]), jnp.exp(-m_t[i]))
        h = num_den[i][:, :hd] * lanes(inv, hd)
        hc = h - jnp.mean(h, axis=-1, keepdims=True)
        hn = hc * lax.rsqrt(jnp.mean(hc * hc, axis=-1, keepdims=True) + EPS) * lnw_ref[:, cols[i]]
        hs = hn + skip_ref[:, cols[i]] * xc_ref[:, cols[i]].astype(F32)
        o_ref[:, cols[i]] = (hs * za_ref[:, cols[i]].astype(F32)).astype(o_ref.dtype)


def _mlstm(q, k, v, xc, z_act, i_row, b_row, lnw, skip, bsz, seq):
    n = q.shape[0]
    L = MLSTM_CHUNK
    nc = seq // L
    hd = C_HEAD
    hp = MLSTM_HEADS_PER_STEP
    blk = lambda: pl.BlockSpec((L, hp * hd), lambda b, h, c: (b * nc + c, h))
    rowspec = lambda: pl.BlockSpec((None, hp, 1, L), lambda b, h, c: (b, h, 0, c))
    vec = lambda: pl.BlockSpec((1, hp * hd), lambda b, h, c: (0, h))
    return pl.pallas_call(
        _mlstm_kernel,
        grid=(bsz, C_HEADS // hp, nc),
        in_specs=[blk(), blk(), blk(), blk(), blk(), rowspec(), rowspec(), vec(), vec()],
        out_specs=blk(),
        out_shape=jax.ShapeDtypeStruct((n, C_WIDTH), BF16),
        scratch_shapes=[pltpu.VMEM((hp, hd, hd + LANES), F32), pltpu.VMEM((hp, 1, LANES), F32)],
        compiler_params=_params(("arbitrary", "arbitrary", "arbitrary")),
        name="mlstm",
    )(q, k, v, xc, z_act, i_row, b_row, lnw, skip)


def _pack_block_diag(w, tile):
    g, bs, _ = w.shape
    per = tile // bs
    w = w.reshape(g // per, per, bs, bs)
    eye = jnp.eye(per, dtype=w.dtype)
    dense = jnp.einsum("npij,pq->npiqj", w, eye)
    return dense.reshape(g // per, tile, tile)


def _row(v):
    return v.reshape(1, -1)


def _even_layer(x, bsz, seq, norm, w_in, a_conv_w, a_conv_b, a_w_r, a_b_r, a_w_i, a_b_i, a_lambda,
                b_mu, b_w0, b_w_up, b_a0, b_a_up, b_g_up, b_k_k, b_k_a, b_r_k, b_ln_w, b_ln_b, w_out):
    main_w = 2 * A_WIDTH + 3 * B_WIDTH
    n_small = B_DECAY_RANK + B_AAA_RANK + B_GATE_RANK
    pad = B_SMALL - n_small
    assert main_w % B_SMALL == 0
    w_all = _column_tiles(jnp.pad(w_in, ((0, 0), (0, pad))).astype(BF16), PROJ_TN)
    p = _norm_matmul(x, _row(norm), w_all, PROJ_TM, "even_in")

    ya = _rglru(p, a_conv_w, _row(a_conv_b),
                _pack_block_diag(a_w_r, MXU_DIM).astype(BF16), _row(a_b_r),
                _pack_block_diag(a_w_i, MXU_DIM).astype(BF16), _row(a_b_i), _row(a_lambda),
                bsz, seq, RGLRU_ROWS)

    mur, muk, muv = (_row(b_mu[i * B_WIDTH:(i + 1) * B_WIDTH]) for i in range(3))
    mus = _row(jnp.pad(b_mu[3 * B_WIDTH:], (0, pad)))

    def rows_at(w, start):
        out = jnp.zeros((B_SMALL_K, B_WIDTH), F32)
        return lax.dynamic_update_slice(out, w, (start, 0)).astype(BF16)

    wup = rows_at(b_w_up, 0)
    aup = rows_at(b_a_up, B_DECAY_RANK)
    gup = rows_at(b_g_up, B_DECAY_RANK + B_AAA_RANK)
    yb = _rwkv(p, mur, muk, muv, mus, _row(b_w0), wup, _row(b_a0), aup, gup, _row(b_k_k), _row(b_k_a),
               _row(b_r_k), _row(b_ln_w), _row(b_ln_b), bsz, seq)

    wo = w_out.astype(BF16)
    return _resid_matmul(x, [ya, yb], [wo[:A_WIDTH], wo[A_WIDTH:]], PROJ_TM, PROJ_TN_WIDE, "even_out")


def _odd_layer(x, bsz, seq, norm, w_in, conv_w, conv_b, w_q, w_k, w_v, w_if, b_if, ln_w, skip, w_out):
    w_in_b = w_in.astype(BF16)
    g = _row(norm)
    z_act = _norm_matmul(x, g, _column_tiles(w_in_b[:, C_WIDTH:], PROJ_TN_WIDE), PROJ_TM, "odd_in_z",
                         silu_bf16=True)
    q, k, v, xc, gates = _mlstm_in(
        x, g, w_in_b, conv_w, _row(conv_b),
        _pack_block_diag(w_q, MXU_DIM).astype(BF16), _pack_block_diag(w_k, MXU_DIM).astype(BF16),
        _pack_block_diag(w_v, MXU_DIM).astype(BF16), w_if.astype(BF16), _row(b_if), seq,
        PROJ_TM, MLSTM_IN_COLS, MLSTM_IN_SUB)
    gt = jnp.transpose(gates.reshape(bsz, seq, 2, C_HEADS), (2, 0, 3, 1))
    i_pre = gt[0]
    b_cum = _mlstm_gates(gt[1].reshape(bsz * C_HEADS, seq), MLSTM_CHUNK).reshape(bsz, C_HEADS, seq)
    hs = _mlstm(q, k, v, xc, z_act, i_pre[:, :, None, :], b_cum[:, :, None, :], _row(ln_w), _row(skip),
                bsz, seq)
    return _resid_matmul(x, [hs], [w_out.astype(BF16)], PROJ_TM, PROJ_TN_WIDE, "odd_out")


def kernel(x, even_norm, even_w_in, a_conv_w, a_conv_b, a_w_r, a_b_r, a_w_i, a_b_i, a_lambda, b_mu, b_w0, b_w_up, b_a0, b_a_up, b_g_up, b_k_k, b_k_a, b_r_k, b_ln_w, b_ln_b, even_w_out, odd_norm, odd_w_in, c_conv_w, c_conv_b, c_w_q, c_w_k, c_w_v, c_w_if, c_b_if, c_ln_w, c_skip, odd_w_out, ffn_norm, ffn_w_gate, ffn_w_up, ffn_conv_w, ffn_conv_b, ffn_w_down, final_norm):
    bsz, seq, d = x.shape
    depth = ffn_norm.shape[0]
    h = x.reshape(bsz * seq, d)
    for layer in range(depth):
        if layer % 2 == 0:
            e = layer // 2
            h = _even_layer(h, bsz, seq, even_norm[e], even_w_in[e], a_conv_w[e], a_conv_b[e], a_w_r[e],
                            a_b_r[e], a_w_i[e], a_b_i[e], a_lambda[e], b_mu[e], b_w0[e], b_w_up[e], b_a0[e],
                            b_a_up[e], b_g_up[e], b_k_k[e], b_k_a[e], b_r_k[e].reshape(-1), b_ln_w[e],
                            b_ln_b[e], even_w_out[e])
        else:
            o = layer // 2
            h = _odd_layer(h, bsz, seq, odd_norm[o], odd_w_in[o], c_conv_w[o], c_conv_b[o], c_w_q[o], c_w_k[o],
                           c_w_v[o], c_w_if[o], c_b_if[o], c_ln_w[o], c_skip[o], odd_w_out[o])
        h = _ffn(h, _row(ffn_norm[layer]), ffn_w_gate[layer].astype(BF16), ffn_w_up[layer].astype(BF16),
                 ffn_conv_w[layer], _row(ffn_conv_b[layer]), ffn_w_down[layer].astype(BF16),
                 _row(final_norm), seq, FFN_TM, FFN_TF, layer == depth - 1, "ffn%d" % layer)
    return h.reshape(bsz, seq, d)
```

```python
import functools
import math

import jax
import jax.numpy as jnp
from jax import lax
from jax.experimental import pallas as pl
from jax.experimental.pallas import tpu as pltpu

F32 = jnp.float32
BF16 = jnp.bfloat16

EPS = 1e-6
D_MODEL = 2048
A_WIDTH = 1024
A_BLOCKS = 8
A_CONV = 4
LRU_C = 8.0
B_WIDTH = 1024
B_HEAD = 64
B_DECAY_RANK = 64
B_AAA_RANK = 64
B_GATE_RANK = 160
B_SMALL = 512
B_SMALL_K = 384
B_LN_EPS = 64e-5
C_WIDTH = 4096
C_HEADS = 8
C_HEAD = 512
C_QKV_BLOCK = 4
C_CONV = 4
D_FF = 5632
FFN_CONV = 3

SUBLANES = 8
LANES = 128
MXU_DIM = 256
VMEM_LIMIT = 56 * 1024 * 1024

RWKV_CHUNK = 64
RWKV_GROUP = 4 * B_HEAD
RWKV_ROWS = 256
MLSTM_CHUNK = 256
MLSTM_HEADS_PER_STEP = 4
PROJ_TM, PROJ_TN = 1024, 512
NORM_SUB_ROWS = 256
PROJ_TN_WIDE = 1024
FFN_TM, FFN_TF = 512, 512
RGLRU_GROUP = MXU_DIM
MLSTM_IN_COLS, MLSTM_IN_SUB = 512, 256


def _params(sem):
    return pltpu.CompilerParams(dimension_semantics=sem, vmem_limit_bytes=VMEM_LIMIT)


def _dot(a, b):
    return jnp.dot(a, b, preferred_element_type=F32)


def _dot_nt(a, b):
    return lax.dot_general(a, b, (((1,), (1,)), ((), ())), preferred_element_type=F32)


def _dot_tn(a, b):
    return lax.dot_general(a, b, (((0,), (0,)), ((), ())), preferred_element_type=F32)


def _split_bf16(x, terms):
    parts = []
    for _ in range(terms):
        p = x.astype(BF16)
        parts.append(p)
        x = x - p.astype(F32)
    return parts


def _sigmoid(x):
    return 1.0 / (1.0 + jnp.exp(-x))


def _silu(x):
    return x * _sigmoid(x)


def _rms(x, g):
    return x * lax.rsqrt(jnp.mean(x * x, axis=-1, keepdims=True) + EPS) * g


def _shift_rows(x, k, prev8):
    r = pltpu.roll(x, k, 0)
    fix = pltpu.roll(prev8, k, 0)
    row = lax.broadcasted_iota(jnp.int32, (SUBLANES, x.shape[1]), 0)
    head = jnp.where(row < k, fix, r[:SUBLANES])
    return jnp.concatenate([head, r[SUBLANES:]], axis=0)


def _norm_matmul_kernel(x_ref, g_ref, w_ref, o_ref, hn_ref, *, silu):
    j = pl.program_id(1)
    tm = x_ref.shape[0]

    def store(rows, y):
        o_ref[rows, :] = (_silu(y) if silu else y).astype(o_ref.dtype)

    @pl.when(j == 0)
    def _():
        for s in range(tm // NORM_SUB_ROWS):
            rows = pl.ds(s * NORM_SUB_ROWS, NORM_SUB_ROWS)
            hn = _rms(x_ref[rows, :], g_ref[...]).astype(BF16)
            hn_ref[rows, :] = hn
            store(rows, _dot(hn, w_ref[...]))

    @pl.when(j != 0)
    def _():
        store(pl.ds(0, tm), _dot(hn_ref[...], w_ref[...]))


def _norm_matmul(x, g, w, tm, tn, name, col_start=0, silu_bf16=False):
    n, d = x.shape
    nout = w.shape[1] - col_start
    off = col_start // tn
    assert col_start % tn == 0 and nout % tn == 0
    return pl.pallas_call(
        functools.partial(_norm_matmul_kernel, silu=silu_bf16),
        grid=(n // tm, nout // tn),
        in_specs=[pl.BlockSpec((tm, d), lambda i, j: (i, 0)),
                  pl.BlockSpec((1, d), lambda i, j: (0, 0)),
                  pl.BlockSpec((d, tn), lambda i, j: (0, j + off))],
        out_specs=pl.BlockSpec((tm, tn), lambda i, j: (i, j)),
        out_shape=jax.ShapeDtypeStruct((n, nout), BF16 if silu_bf16 else F32),
        scratch_shapes=[pltpu.VMEM((tm, d), BF16)],
        compiler_params=_params(("arbitrary", "arbitrary")),
        name=name,
    )(x, g, w)


def _resid_matmul_kernel(*refs, n_in):
    x_ref = refs[0]
    a_refs = refs[1:1 + n_in]
    w_refs = refs[1 + n_in:1 + 2 * n_in]
    o_ref = refs[1 + 2 * n_in]
    acc = x_ref[...]
    for a_ref, w_ref in zip(a_refs, w_refs):
        acc = acc + _dot(a_ref[...], w_ref[...])
    o_ref[...] = acc


def _resid_matmul(x, acts, ws, tm, tn, name):
    n, d = x.shape
    n_in = len(acts)
    in_specs = [pl.BlockSpec((tm, tn), lambda i, j: (i, j))]
    in_specs += [pl.BlockSpec((tm, a.shape[1]), lambda i, j: (i, 0)) for a in acts]
    in_specs += [pl.BlockSpec((w.shape[0], tn), lambda i, j: (0, j)) for w in ws]
    return pl.pallas_call(
        functools.partial(_resid_matmul_kernel, n_in=n_in),
        grid=(n // tm, d // tn),
        in_specs=in_specs,
        out_specs=pl.BlockSpec((tm, tn), lambda i, j: (i, j)),
        out_shape=jax.ShapeDtypeStruct((n, d), F32),
        compiler_params=_params(("arbitrary", "arbitrary")),
        name=name,
    )(x, *acts, *ws)


def _ffn_kernel(x_ref, g_ref, wg_ref, wu_ref, cw_ref, cb_ref, wd_ref, fg_ref, o_ref, hn_ref, carry_ref,
                *, tiles_per_seq, final_norm):
    i = pl.program_id(0)
    j = pl.program_id(1)
    tm = x_ref.shape[0]

    @pl.when(j == 0)
    def _():
        x = x_ref[...]
        hn_ref[...] = _rms(x, g_ref[...]).astype(BF16)
        o_ref[...] = x

    hn = hn_ref[...]
    gate = _dot(hn, wg_ref[...])
    up = _dot(hn, wu_ref[...])
    seq_start = (i % tiles_per_seq) == 0
    prev8 = jnp.where(seq_start, 0.0, carry_ref[j])
    carry_ref[j] = gate[tm - SUBLANES:]
    cw = cw_ref[...]
    conv = (cb_ref[...] + gate * cw[2:3] + _shift_rows(gate, 1, prev8) * cw[1:2]
            + _shift_rows(gate, 2, prev8) * cw[0:1])
    u = (_silu(conv) * up).astype(BF16)
    o_ref[...] += _dot(u, wd_ref[...])

    if final_norm:
        @pl.when(j == pl.num_programs(1) - 1)
        def _():
            o_ref[...] = _rms(o_ref[...], fg_ref[...])


def _ffn(x, g, wg, wu, cw, cb, wd, fg, seq, tm, tf, final_norm, name):
    n, d = x.shape
    f = wg.shape[1]
    nf = f // tf
    kern = functools.partial(_ffn_kernel, tiles_per_seq=seq // tm, final_norm=final_norm)
    return pl.pallas_call(
        kern,
        grid=(n // tm, nf),
        in_specs=[pl.BlockSpec((tm, d), lambda i, j: (i, 0)),
                  pl.BlockSpec((1, d), lambda i, j: (0, 0)),
                  pl.BlockSpec((d, tf), lambda i, j: (0, j)),
                  pl.BlockSpec((d, tf), lambda i, j: (0, j)),
                  pl.BlockSpec((FFN_CONV, tf), lambda i, j: (0, j)),
                  pl.BlockSpec((1, tf), lambda i, j: (0, j)),
                  pl.BlockSpec((tf, d), lambda i, j: (j, 0)),
                  pl.BlockSpec((1, d), lambda i, j: (0, 0))],
        out_specs=pl.BlockSpec((tm, d), lambda i, j: (i, 0)),
        out_shape=jax.ShapeDtypeStruct((n, d), F32),
        scratch_shapes=[pltpu.VMEM((tm, d), BF16), pltpu.VMEM((nf, SUBLANES, tf), F32)],
        compiler_params=_params(("arbitrary", "arbitrary")),
        name=name,
    )(x, g, wg, wu, cw, cb, wd, fg)


def _rglru_gates(x, prev8, cw, cb, wr, wi):
    xc = cb + x * cw[A_CONV - 1:A_CONV]
    for k in range(1, A_CONV):
        xc = xc + _shift_rows(x, k, prev8) * cw[A_CONV - 1 - k:A_CONV - k]
    xb = xc.astype(BF16)
    return xc, _dot(xb, wr), _dot(xb, wi)


def _rglru_scan(xc, r_pre, i_pre, ga, h0, br, bi, lam):
    rows, width = xc.shape
    r = _sigmoid(r_pre + br)
    ig = _sigmoid(i_pre + bi)
    neg_lam = -lam
    softplus = jnp.maximum(neg_lam, 0.0) + jnp.log1p(jnp.exp(-jnp.abs(neg_lam)))
    log_a = (-LRU_C) * r * softplus
    a = jnp.exp(log_a)
    u = jnp.sqrt(1.0 - jnp.exp(2.0 * log_a)) * (ig * xc)

    n_sub = rows // SUBLANES
    a3 = a.reshape(n_sub, SUBLANES, width)
    u3 = u.reshape(n_sub, SUBLANES, width)
    sub = lax.broadcasted_iota(jnp.int32, (n_sub, SUBLANES, width), 1)
    s = 1
    while s < SUBLANES:
        keep = sub >= s
        a_prev = pltpu.roll(a3, s, 1)
        u_prev = pltpu.roll(u3, s, 1)
        u3 = jnp.where(keep, a3 * u_prev + u3, u3)
        a3 = jnp.where(keep, a3 * a_prev, a3)
        s *= 2
    carry = h0
    groups = []
    for gi in range(n_sub):
        hg = u3[gi] + a3[gi] * carry
        groups.append(hg)
        carry = hg[SUBLANES - 1:]
    h = jnp.concatenate(groups, axis=0)
    gelu = 0.5 * ga * (1.0 + jnp.tanh(math.sqrt(2.0 / math.pi) * (ga + 0.044715 * (ga * ga * ga))))
    return (h * gelu).astype(BF16), carry


def _even_in_kernel(x_ref, g_ref, w_ref, cw_ref, cb_ref, wr_ref, br_ref, wi_ref, bi_ref, lam_ref,
                    p_ref, ya_ref, hn_ref, act_ref, prev_ref, h_ref, *, tiles_per_seq):
    i = pl.program_id(0)
    j = pl.program_id(1)
    tm = x_ref.shape[0]
    tn = w_ref.shape[1]
    grp = RGLRU_GROUP
    n_units = A_WIDTH // grp
    n_act = 2 * A_WIDTH // tn
    per_tile = tn // grp

    def stash(rows, y, tile):
        for part in range(per_tile):
            act_ref[per_tile * tile + part, rows, :] = y[:, part * grp:(part + 1) * grp]

    @pl.when(j == 0)
    def _():
        for s in range(tm // NORM_SUB_ROWS):
            rows = pl.ds(s * NORM_SUB_ROWS, NORM_SUB_ROWS)
            hn = _rms(x_ref[rows, :], g_ref[...]).astype(BF16)
            hn_ref[rows, :] = hn
            stash(rows, _dot(hn, w_ref[...]), 0)

    @pl.when((j > 0) & (j < n_act))
    def _():
        stash(pl.ds(0, tm), _dot(hn_ref[...], w_ref[...]), j)

    @pl.when((j >= n_act) & (j < n_act + n_units))
    def _():
        unit = j - n_act
        seq_start = (i % tiles_per_seq) == 0
        prev8 = jnp.where(seq_start, 0.0, prev_ref[unit])
        h_last = jnp.where(seq_start, 0.0, h_ref[unit])
        gate_w = (cw_ref[unit], cb_ref[unit], wr_ref[unit], wi_ref[unit])
        scan_w = (br_ref[unit], bi_ref[unit], lam_ref[unit])
        sub = lambda s: pl.ds(s * NORM_SUB_ROWS, NORM_SUB_ROWS)
        for s in range(tm // NORM_SUB_ROWS):
            x = act_ref[unit, sub(s), :]
            gated = _rglru_gates(x, prev8, *gate_w)
            prev8 = x[NORM_SUB_ROWS - SUBLANES:]
            projected = _dot(hn_ref[sub(s), :], w_ref[...])
            y, h_last = _rglru_scan(*gated, act_ref[n_units + unit, sub(s), :], h_last, *scan_w)
            p_ref[sub(s), :] = projected
            ya_ref[sub(s), :] = y
        prev_ref[unit] = prev8
        h_ref[unit] = h_last

    @pl.when(j >= n_act + n_units)
    def _():
        p_ref[...] = _dot(hn_ref[...], w_ref[...])


def _even_in(x, g, w_all, cw, cb, wr, br, wi, bi, lam, seq, tm, tn):
    n, d = x.shape
    grp = RGLRU_GROUP
    n_units = A_WIDTH // grp
    n_tiles = w_all.shape[1] // tn
    n_act = 2 * A_WIDTH // tn
    assert (2 * A_WIDTH) % tn == 0 and tn % grp == 0 and n_tiles >= n_act + n_units
    group_vec = lambda v: v.reshape(n_units, 1, grp)
    whole = lambda a: pl.BlockSpec(a.shape, lambda i, j: (0,) * a.ndim)
    params = [jnp.transpose(cw.reshape(A_CONV, n_units, grp), (1, 0, 2)), group_vec(cb), wr, group_vec(br),
              wi, group_vec(bi), group_vec(lam)]
    return pl.pallas_call(
        functools.partial(_even_in_kernel, tiles_per_seq=seq // tm),
        grid=(n // tm, n_tiles),
        in_specs=[pl.BlockSpec((tm, d), lambda i, j: (i, 0)),
                  pl.BlockSpec((1, d), lambda i, j: (0, 0)),
                  pl.BlockSpec((d, tn), lambda i, j: (0, j))] + [whole(a) for a in params],
        out_specs=[pl.BlockSpec((tm, tn), lambda i, j: (i, jnp.maximum(j - n_act, 0))),
                   pl.BlockSpec((tm, grp), lambda i, j: (i, jnp.clip(j - n_act, 0, n_units - 1)))],
        out_shape=[jax.ShapeDtypeStruct((n, w_all.shape[1] - 2 * A_WIDTH), F32),
                   jax.ShapeDtypeStruct((n, A_WIDTH), BF16)],
        scratch_shapes=[pltpu.VMEM((tm, d), BF16), pltpu.VMEM((2 * n_units, tm, grp), F32),
                        pltpu.VMEM((n_units, SUBLANES, grp), F32), pltpu.VMEM((n_units, 1, grp), F32)],
        compiler_params=_params(("arbitrary", "arbitrary")),
        name="even_in",
    )(x, g, w_all, *params)


def _alternate(*stage_streams):
    results = [None] * len(stage_streams)
    live = list(range(len(stage_streams)))
    while live:
        for idx in list(live):
            try:
                next(stage_streams[idx])
            except StopIteration as stop:
                results[idx] = stop.value
                live.remove(idx)
    return results


def _rwkv_kernel(r_ref, k_ref, v_ref, sm_ref, mur_ref, muk_ref, muv_ref, mus_ref, w0_ref, wup_ref, a0_ref,
                 aup_ref, gup_ref, kkw_ref, kaw_ref, rkw_ref, lnw_ref, lnb_ref, o_ref,
                 s_ref, pr_ref, pk_ref, pv_ref, ps_ref):
    tt = r_ref.shape[0]
    L = RWKV_CHUNK
    gw = RWKV_GROUP
    n_grp = B_WIDTH // gw
    hr = tt // 2

    @pl.when(pl.program_id(1) == 0)
    def _():
        s_ref[...] = jnp.zeros_like(s_ref)
        pr_ref[...] = jnp.zeros_like(pr_ref)
        pk_ref[...] = jnp.zeros_like(pk_ref)
        pv_ref[...] = jnp.zeros_like(pv_ref)
        ps_ref[...] = jnp.zeros_like(ps_ref)

    def lerp(x_ref, p_ref, mu_ref):
        x = x_ref[...]
        xs = _shift_rows(x, 1, p_ref[...])
        p_ref[...] = x[tt - SUBLANES:]
        return x + (xs - x) * mu_ref[...]

    r_all = lerp(r_ref, pr_ref, mur_ref)
    k_all = lerp(k_ref, pk_ref, muk_ref)
    v_all = lerp(v_ref, pv_ref, muv_ref)
    sm_all = lerp(sm_ref, ps_ref, mus_ref)[:, :B_SMALL_K]

    row_g = lax.broadcasted_iota(jnp.int32, (gw, gw), 0)
    col_g = lax.broadcasted_iota(jnp.int32, (gw, gw), 1)
    same_head = (row_g // B_HEAD) == (col_g // B_HEAD)
    head_mask = jnp.where(same_head, 1.0, 0.0)
    head_mask_bf = head_mask.astype(BF16)

    def head_sum(x, terms):
        n = x.shape[0]
        xs = jnp.concatenate([x[:, gi * gw:(gi + 1) * gw] for gi in range(n_grp)], axis=0)
        s = _dot(jnp.concatenate(_split_bf16(xs, terms), axis=0), head_mask_bf)
        s = sum(s[t * n_grp * n:(t + 1) * n_grp * n] for t in range(terms))
        return jnp.concatenate([s[gi * n:(gi + 1) * n] for gi in range(n_grp)], axis=1)

    def bd(x):
        return jnp.concatenate([x.astype(BF16)] * (gw // B_HEAD), axis=0) * head_mask_bf

    row_t = lax.broadcasted_iota(jnp.int32, (hr, hr), 0)
    col_t = lax.broadcasted_iota(jnp.int32, (hr, hr), 1)
    tril = jnp.where((row_t >= col_t) & ((row_t // L) == (col_t // L)), 1.0, 0.0).astype(BF16)
    row_p = lax.broadcasted_iota(jnp.int32, (L, gw), 0)
    src_p = lax.broadcasted_iota(jnp.int32, (L, gw), 1) % B_HEAD
    strict_lower = row_p > src_p
    lower = row_p >= src_p
    n_doublings = int(math.log2(L)) - 1
    groups = range(n_grp)
    half_units = [(c, gi) for c in range(hr // L) for gi in groups]

    def tile(arr, c, gi):
        return arr[c * L:(c + 1) * L, gi * gw:(gi + 1) * gw]

    def prepare(h):
        rows = slice(h * hr, (h + 1) * hr)
        r, k, v, sm = r_all[rows], k_all[rows], v_all[rows], sm_all[rows]
        z = w0_ref[...] + _dot(jnp.tanh(sm).astype(BF16), wup_ref[...])
        log_w = (-math.exp(-0.5)) * _sigmoid(z)
        yield
        a = _sigmoid(a0_ref[...] + _dot(sm.astype(BF16), aup_ref[...]))
        yield
        g = _dot(_sigmoid(sm).astype(BF16), gup_ref[...])
        kk = k * kkw_ref[...]
        yield
        kk = kk * lax.rsqrt(jnp.maximum(head_sum(kk * kk, 2), 1e-12))
        yield
        k2 = k * (1.0 + (a - 1.0) * kaw_ref[...])
        yield
        cum = sum(_dot(tril, part) for part in _split_bf16(log_w, 3))
        yield
        p_in = jnp.exp(cum)
        p_inv = jnp.exp(-cum)
        yield
        a_bar = (-kk) * jnp.exp(cum - log_w)
        r_bar = r * p_in
        yield
        b_bar = kk * a * p_inv
        k_bar = k2 * p_inv
        return dict(r=r, v=v, g=g, k2=k2, p_in=p_in, a_bar=a_bar, r_bar=r_bar, b_bar=b_bar, k_bar=k_bar)

    def products(pre, states):
        units = half_units
        ar = {u: jnp.concatenate([tile(pre["a_bar"], *u), tile(pre["r_bar"], *u)], axis=0).astype(BF16)
              for u in units}
        m_b = {u: _dot_nt(ar[u], bd(tile(pre["b_bar"], *u))) for u in units}
        yield
        m_k = {u: _dot_nt(ar[u], bd(tile(pre["k_bar"], *u))) for u in units}
        yield
        x = {u: jnp.where(strict_lower, m_b[u][:L], 0.0) for u in units}
        a_rb = {u: jnp.where(lower, m_b[u][L:], 0.0).astype(BF16) for u in units}
        akrk = {u: jnp.concatenate([jnp.where(strict_lower, m_k[u][:L], 0.0),
                                    jnp.where(lower, m_k[u][L:], 0.0)], axis=0).astype(BF16) for u in units}
        cy = {u: _dot(akrk[u], bd(tile(pre["v"], *u))) for u in units}
        yield
        n_inv = dict(x)
        x_pow = {u: _dot(x[u].astype(BF16), bd(x[u])) for u in units}
        yield
        for step in range(n_doublings):
            if step + 1 < n_doublings:
                both = {u: _dot(jnp.concatenate([x_pow[u], n_inv[u]], axis=0).astype(BF16), bd(x_pow[u]))
                        for u in units}
                n_inv = {u: n_inv[u] + x_pow[u] + both[u][L:] for u in units}
                x_pow = {u: both[u][:L] for u in units}
            else:
                n_inv = {u: n_inv[u] + x_pow[u] + _dot(n_inv[u].astype(BF16), bd(x_pow[u])) for u in units}
            yield
        n_inv = {u: n_inv[u].astype(BF16) for u in units}
        y_rows = []
        for c in range(hr // L):
            p_last = pre["p_in"][(c + 1) * L - 1:(c + 1) * L]
            pl_g = [p_last[:, gi * gw:(gi + 1) * gw] for gi in groups]
            bk = [jnp.concatenate([tile(pre["b_bar"], c, gi) * pl_g[gi], tile(pre["k_bar"], c, gi) * pl_g[gi]],
                                  axis=0).astype(BF16) for gi in groups]
            ah = [_dot_nt(ar[c, gi], states[gi].astype(BF16)) for gi in groups]
            yield
            rhs = [ah[gi][:L] + cy[c, gi][:L] for gi in groups]
            u_c = [rhs[gi] + _dot(n_inv[c, gi], bd(rhs[gi])) for gi in groups]
            yield
            ds = [_dot_tn(jnp.concatenate([u_c[gi], tile(pre["v"], c, gi)], axis=0).astype(BF16), bk[gi])
                  for gi in groups]
            states = [states[gi] * pl_g[gi] + head_mask * ds[gi] for gi in groups]
            y_rows.append(jnp.concatenate(
                [ah[gi][L:] + cy[c, gi][L:] + _dot(a_rb[c, gi], bd(u_c[gi])) for gi in groups], axis=1))
            yield
        return jnp.concatenate(y_rows, axis=0), states

    def finish(h, pre, y):
        inv_n = 1.0 / B_HEAD
        yc = y - head_sum(y, 1) * inv_n
        yield
        var = head_sum(yc * yc, 1) * inv_n
        yield
        yn = yc * lax.rsqrt(var + B_LN_EPS) * lnw_ref[...] + lnb_ref[...]
        yield
        bonus = head_sum(pre["r"] * pre["k2"] * rkw_ref[...], 1) * pre["v"]
        yield
        o_ref[pl.ds(h * hr, hr), :] = ((yn + bonus) * pre["g"]).astype(o_ref.dtype)

    states = [s_ref[gi] for gi in groups]
    (pre0,) = _alternate(prepare(0))
    (y0, states), pre1 = _alternate(products(pre0, states), prepare(1))
    (y1, states), _ = _alternate(products(pre1, states), finish(0, pre0, y0))
    _alternate(finish(1, pre1, y1))
    for gi in groups:
        s_ref[gi] = states[gi]


def _rwkv(p, mur, muk, muv, mus, w0, wup, a0, aup, gup, kkw, kaw, rkw, lnw, lnb, bsz, seq):
    n = p.shape[0]
    tt = RWKV_ROWS
    nt = seq // tt
    w = B_WIDTH
    vec = lambda: pl.BlockSpec((1, w), lambda b, t: (0, 0))
    mat = lambda: pl.BlockSpec((B_SMALL_K, w), lambda b, t: (0, 0))
    return pl.pallas_call(
        _rwkv_kernel,
        grid=(bsz, nt),
        in_specs=[pl.BlockSpec((tt, w), lambda b, t: (b * nt + t, 0)),
                  pl.BlockSpec((tt, w), lambda b, t: (b * nt + t, 1)),
                  pl.BlockSpec((tt, w), lambda b, t: (b * nt + t, 2)),
                  pl.BlockSpec((tt, B_SMALL), lambda b, t: (b * nt + t, 3 * w // B_SMALL)),
                  vec(), vec(), vec(), pl.BlockSpec((1, B_SMALL), lambda b, t: (0, 0)),
                  vec(), mat(), vec(), mat(), mat(), vec(), vec(), vec(), vec(), vec()],
        out_specs=pl.BlockSpec((tt, w), lambda b, t: (b * nt + t, 0)),
        out_shape=jax.ShapeDtypeStruct((n, w), BF16),
        scratch_shapes=[pltpu.VMEM((w // RWKV_GROUP, RWKV_GROUP, RWKV_GROUP), F32),
                        pltpu.VMEM((SUBLANES, w), F32), pltpu.VMEM((SUBLANES, w), F32),
                        pltpu.VMEM((SUBLANES, w), F32), pltpu.VMEM((SUBLANES, B_SMALL), F32)],
        compiler_params=_params(("arbitrary", "arbitrary")),
        name="rwkv7",
    )(p, p, p, p, mur, muk, muv, mus, w0, wup, a0, aup, gup, kkw, kaw, rkw, lnw, lnb)


def _mlstm_in_kernel(x_ref, g_ref, w_ref, cw_ref, cb_ref, wq_ref, wk_ref, wv_ref, wif_ref, bif_ref,
                     q_ref, k_ref, v_ref, xc_ref, gates_ref, hn_ref, prev_ref, *, tiles_per_seq, sub_rows):
    i = pl.program_id(0)
    j = pl.program_id(1)
    tm = x_ref.shape[0]
    cb = w_ref.shape[1]
    n_grp = cb // MXU_DIM

    def blockdiag(xb, wb_ref):
        return jnp.concatenate(
            [_dot(xb[:, g * MXU_DIM:(g + 1) * MXU_DIM], wb_ref[g]) for g in range(n_grp)], axis=1)

    def body(first_col_tile):
        def project(s):
            rows = pl.ds(s * sub_rows, sub_rows)
            if first_col_tile:
                hn = _rms(x_ref[rows, :], g_ref[...]).astype(BF16)
                hn_ref[rows, :] = hn
            else:
                hn = hn_ref[rows, :]
            return _dot(hn, w_ref[...])

        seq_start = (i % tiles_per_seq) == 0
        prev8 = jnp.where(seq_start, 0.0, prev_ref[j])
        cw = cw_ref[...]
        n_sub = tm // sub_rows
        xm_next = project(0)
        for s in range(n_sub):
            rows = pl.ds(s * sub_rows, sub_rows)
            xm = xm_next
            if s + 1 < n_sub:
                xm_next = project(s + 1)
            conv = cb_ref[...] + xm * cw[C_CONV - 1:C_CONV]
            for kk in range(1, C_CONV):
                conv = conv + _shift_rows(xm, kk, prev8) * cw[C_CONV - 1 - kk:C_CONV - kk]
            prev8 = xm[sub_rows - SUBLANES:]
            xc = _silu(conv)
            xcb = xc.astype(BF16)
            xc_ref[rows, :] = xcb
            q = blockdiag(xcb, wq_ref)
            k = blockdiag(xcb, wk_ref)
            v = blockdiag(xm.astype(BF16), wv_ref)
            qb = q.astype(BF16)
            kb = k.astype(BF16)
            vb = v.astype(BF16)
            q_ref[rows, :] = qb
            k_ref[rows, :] = (k * (C_HEAD ** -0.5)).astype(BF16)
            v_ref[rows, :] = vb
            gate_part = _dot(qb, wif_ref[0]) + _dot(kb, wif_ref[1]) + _dot(vb, wif_ref[2])
            if first_col_tile:
                gates_ref[rows, :] = bif_ref[...] + gate_part
            else:
                gates_ref[rows, :] += gate_part
        prev_ref[j] = prev8

    pl.when(j == 0)(functools.partial(body, True))
    pl.when(j != 0)(functools.partial(body, False))


def _mlstm_in(x, g, w_in, cw, cb, wq, wk, wv, wif, bif, seq, tm, cblk, sub_rows):
    n, d = x.shape
    ncb = C_WIDTH // cblk
    gpb = cblk // MXU_DIM
    blk = lambda: pl.BlockSpec((tm, cblk), lambda i, j: (i, j))
    wspec = lambda: pl.BlockSpec((gpb, MXU_DIM, MXU_DIM), lambda i, j: (j, 0, 0))
    act = lambda dt: jax.ShapeDtypeStruct((n, C_WIDTH), dt)
    kern = functools.partial(_mlstm_in_kernel, tiles_per_seq=seq // tm, sub_rows=sub_rows)
    return pl.pallas_call(
        kern,
        grid=(n // tm, ncb),
        in_specs=[pl.BlockSpec((tm, d), lambda i, j: (i, 0)),
                  pl.BlockSpec((1, d), lambda i, j: (0, 0)),
                  pl.BlockSpec((d, cblk), lambda i, j: (0, j)),
                  pl.BlockSpec((C_CONV, cblk), lambda i, j: (0, j)),
                  pl.BlockSpec((1, cblk), lambda i, j: (0, j)),
                  wspec(), wspec(), wspec(),
                  pl.BlockSpec((3, cblk, 2 * C_HEADS), lambda i, j: (0, j, 0)),
                  pl.BlockSpec((1, 2 * C_HEADS), lambda i, j: (0, 0))],
        out_specs=[blk(), blk(), blk(), blk(),
                   pl.BlockSpec((tm, 2 * C_HEADS), lambda i, j: (i, 0))],
        out_shape=[act(BF16), act(BF16), act(BF16), act(BF16),
                   jax.ShapeDtypeStruct((n, 2 * C_HEADS), F32)],
        scratch_shapes=[pltpu.VMEM((tm, d), BF16), pltpu.VMEM((ncb, SUBLANES, cblk), F32)],
        compiler_params=_params(("arbitrary", "arbitrary")),
        name="mlstm_in",
    )(x, g, w_in, cw, cb, wq, wk, wv, wif, bif)


def _mlstm_gate_kernel(f_ref, b_ref):
    L = f_ref.shape[1]
    f = f_ref[...]
    lf = jnp.minimum(f, 0.0) - jnp.log1p(jnp.exp(-jnp.abs(f)))
    row = lax.broadcasted_iota(jnp.int32, (L, L), 0)
    col = lax.broadcasted_iota(jnp.int32, (L, L), 1)
    triu = jnp.where(row <= col, 1.0, 0.0).astype(BF16)
    b_ref[...] = sum(_dot(part, triu) for part in _split_bf16(lf, 3))


def _mlstm_gates(f_pre, chunk):
    rows, seq = f_pre.shape
    return pl.pallas_call(
        _mlstm_gate_kernel,
        grid=(seq // chunk,),
        in_specs=[pl.BlockSpec((rows, chunk), lambda c: (0, c))],
        out_specs=pl.BlockSpec((rows, chunk), lambda c: (0, c)),
        out_shape=jax.ShapeDtypeStruct((rows, seq), F32),
        compiler_params=_params(("arbitrary",)),
        name="mlstm_gates",
    )(f_pre)


def _mlstm_kernel(q_ref, k_ref, v_ref, xc_ref, za_ref, ir_ref, br_ref, lnw_ref, skip_ref,
                  o_ref, ct_ref, m_ref):
    c = pl.program_id(2)
    L = q_ref.shape[0]
    n_heads, hd = ct_ref.shape[0], ct_ref.shape[1]

    @pl.when(c == 0)
    def _():
        ct_ref[...] = jnp.zeros_like(ct_ref)
        m_ref[...] = jnp.zeros_like(m_ref)

    def lanes(x, width):
        return jnp.concatenate([x] * (width // LANES), axis=1)

    row = lax.broadcasted_iota(jnp.int32, (L, L), 0)
    col = lax.broadcasted_iota(jnp.int32, (L, L), 1)
    causal = row >= col
    eye = jnp.where(row == col, 1.0, 0.0).astype(BF16)

    def to_col(x_row):
        return sum(_dot_nt(eye, jnp.broadcast_to(part, (LANES, L))) for part in _split_bf16(x_row, 3))

    heads = range(n_heads)
    cols = [slice(i * hd, (i + 1) * hd) for i in heads]
    q = [q_ref[:, cols[i]] for i in heads]
    k = [k_ref[:, cols[i]] for i in heads]
    v_aug = [jnp.concatenate([v_ref[:, cols[i]], jnp.ones((L, LANES), BF16)], axis=1) for i in heads]
    li_row = [ir_ref[i] for i in heads]
    b_row = [br_ref[i] for i in heads]
    m_prev = [m_ref[i] for i in heads]

    qk = [_dot_nt(q[i], k[i]) for i in heads]
    q_ct = [_dot(q[i], ct_ref[i].astype(BF16)) for i in heads]
    b_col = [to_col(b_row[i]) for i in heads]
    li_col = [to_col(li_row[i]) for i in heads]
    b_last = [b_col[i][L - 1:L] for i in heads]

    d_log = [jnp.where(causal, lanes(b_col[i], L) - b_row[i] + li_row[i], -jnp.inf) for i in heads]
    inter = [b_col[i] + m_prev[i] for i in heads]
    m_t = [jnp.maximum(inter[i], jnp.max(d_log[i], axis=-1, keepdims=True)) for i in heads]
    s = [(qk[i] * jnp.exp(d_log[i] - lanes(m_t[i], L))).astype(BF16) for i in heads]
    sc = [jnp.exp(inter[i] - m_t[i]) for i in heads]
    num_den = [_dot(s[i], v_aug[i]) + lanes(sc[i], hd + LANES) * q_ct[i] for i in heads]

    g_log = [b_last[i] - b_col[i] + li_col[i] for i in heads]
    m_new = [jnp.maximum(b_last[i] + m_prev[i], jnp.max(g_log[i], axis=0, keepdims=True)) for i in heads]
    ke = [k[i] * lanes(jnp.exp(g_log[i] - m_new[i]).astype(BF16), hd) for i in heads]
    decay = [jnp.exp(b_last[i] + m_prev[i] - m_new[i]) for i in heads]
    for i in heads:
        ct_ref[i] = lanes(decay[i], hd + LANES) * ct_ref[i] + _dot_tn(ke[i], v_aug[i])
        m_ref[i] = m_new[i]

    for i in heads:
        inv = 1.0 / jnp.maximum(jnp.abs(num_den[i][:, hd:]), jnp.exp(-m_t[i]))
        h = num_den[i][:, :hd] * lanes(inv, hd)
        hc = h - jnp.mean(h, axis=-1, keepdims=True)
        hn = hc * lax.rsqrt(jnp.mean(hc * hc, axis=-1, keepdims=True) + EPS) * lnw_ref[:, cols[i]]
        hs = hn + skip_ref[:, cols[i]] * xc_ref[:, cols[i]].astype(F32)
        o_ref[:, cols[i]] = (hs * za_ref[:, cols[i]].astype(F32)).astype(o_ref.dtype)


def _mlstm(q, k, v, xc, z_act, i_row, b_row, lnw, skip, bsz, seq):
    n = q.shape[0]
    L = MLSTM_CHUNK
    nc = seq // L
    hd = C_HEAD
    hp = MLSTM_HEADS_PER_STEP
    blk = lambda: pl.BlockSpec((L, hp * hd), lambda b, h, c: (b * nc + c, h))
    rowspec = lambda: pl.BlockSpec((None, hp, 1, L), lambda b, h, c: (b, h, 0, c))
    vec = lambda: pl.BlockSpec((1, hp * hd), lambda b, h, c: (0, h))
    return pl.pallas_call(
        _mlstm_kernel,
        grid=(bsz, C_HEADS // hp, nc),
        in_specs=[blk(), blk(), blk(), blk(), blk(), rowspec(), rowspec(), vec(), vec()],
        out_specs=blk(),
        out_shape=jax.ShapeDtypeStruct((n, C_WIDTH), BF16),
        scratch_shapes=[pltpu.VMEM((hp, hd, hd + LANES), F32), pltpu.VMEM((hp, 1, LANES), F32)],
        compiler_params=_params(("arbitrary", "arbitrary", "arbitrary")),
        name="mlstm",
    )(q, k, v, xc, z_act, i_row, b_row, lnw, skip)


def _pack_block_diag(w, tile):
    g, bs, _ = w.shape
    per = tile // bs
    w = w.reshape(g // per, per, bs, bs)
    eye = jnp.eye(per, dtype=w.dtype)
    dense = jnp.einsum("npij,pq->npiqj", w, eye)
    return dense.reshape(g // per, tile, tile)


def _row(v):
    return v.reshape(1, -1)


def _even_layer(x, bsz, seq, norm, w_in, a_conv_w, a_conv_b, a_w_r, a_b_r, a_w_i, a_b_i, a_lambda,
                b_mu, b_w0, b_w_up, b_a0, b_a_up, b_g_up, b_k_k, b_k_a, b_r_k, b_ln_w, b_ln_b, w_out):
    main_w = 2 * A_WIDTH + 3 * B_WIDTH
    n_small = B_DECAY_RANK + B_AAA_RANK + B_GATE_RANK
    pad = B_SMALL - n_small
    assert main_w % B_SMALL == 0
    w_all = jnp.pad(w_in, ((0, 0), (0, pad))).astype(BF16)
    p, ya = _even_in(x, _row(norm), w_all, a_conv_w, a_conv_b,
                     _pack_block_diag(a_w_r, RGLRU_GROUP).astype(BF16), a_b_r,
                     _pack_block_diag(a_w_i, RGLRU_GROUP).astype(BF16), a_b_i, a_lambda, seq, PROJ_TM, PROJ_TN)

    mur, muk, muv = (_row(b_mu[i * B_WIDTH:(i + 1) * B_WIDTH]) for i in range(3))
    mus = _row(jnp.pad(b_mu[3 * B_WIDTH:], (0, pad)))

    def rows_at(w, start):
        out = jnp.zeros((B_SMALL_K, B_WIDTH), F32)
        return lax.dynamic_update_slice(out, w, (start, 0)).astype(BF16)

    wup = rows_at(b_w_up, 0)
    aup = rows_at(b_a_up, B_DECAY_RANK)
    gup = rows_at(b_g_up, B_DECAY_RANK + B_AAA_RANK)
    yb = _rwkv(p, mur, muk, muv, mus, _row(b_w0), wup, _row(b_a0), aup, gup, _row(b_k_k), _row(b_k_a),
               _row(b_r_k), _row(b_ln_w), _row(b_ln_b), bsz, seq)

    wo = w_out.astype(BF16)
    return _resid_matmul(x, [ya, yb], [wo[:A_WIDTH], wo[A_WIDTH:]], PROJ_TM, PROJ_TN_WIDE, "even_out")


def _odd_layer(x, bsz, seq, norm, w_in, conv_w, conv_b, w_q, w_k, w_v, w_if, b_if, ln_w, skip, w_out):
    w_in_b = w_in.astype(BF16)
    g = _row(norm)
    z_act = _norm_matmul(x, g, w_in_b, PROJ_TM, PROJ_TN_WIDE, "odd_in_z", col_start=C_WIDTH, silu_bf16=True)
    q, k, v, xc, gates = _mlstm_in(
        x, g, w_in_b, conv_w, _row(conv_b),
        _pack_block_diag(w_q, MXU_DIM).astype(BF16), _pack_block_diag(w_k, MXU_DIM).astype(BF16),
        _pack_block_diag(w_v, MXU_DIM).astype(BF16), w_if.astype(BF16), _row(b_if), seq,
        PROJ_TM, MLSTM_IN_COLS, MLSTM_IN_SUB)
    gt = jnp.transpose(gates.reshape(bsz, seq, 2, C_HEADS), (2, 0, 3, 1))
    i_pre = gt[0]
    b_cum = _mlstm_gates(gt[1].reshape(bsz * C_HEADS, seq), MLSTM_CHUNK).reshape(bsz, C_HEADS, seq)
    hs = _mlstm(q, k, v, xc, z_act, i_pre[:, :, None, :], b_cum[:, :, None, :], _row(ln_w), _row(skip),
                bsz, seq)
    return _resid_matmul(x, [hs], [w_out.astype(BF16)], PROJ_TM, PROJ_TN_WIDE, "odd_out")


def kernel(x, even_norm, even_w_in, a_conv_w, a_conv_b, a_w_r, a_b_r, a_w_i, a_b_i, a_lambda, b_mu, b_w0, b_w_up, b_a0, b_a_up, b_g_up, b_k_k, b_k_a, b_r_k, b_ln_w, b_ln_b, even_w_out, odd_norm, odd_w_in, c_conv_w, c_conv_b, c_w_q, c_w_k, c_w_v, c_w_if, c_b_if, c_ln_w, c_skip, odd_w_out, ffn_norm, ffn_w_gate, ffn_w_up, ffn_conv_w, ffn_conv_b, ffn_w_down, final_norm):
    bsz, seq, d = x.shape
    depth = ffn_norm.shape[0]
    h = x.reshape(bsz * seq, d)
    for layer in range(depth):
        if layer % 2 == 0:
            e = layer // 2
            h = _even_layer(h, bsz, seq, even_norm[e], even_w_in[e], a_conv_w[e], a_conv_b[e], a_w_r[e],
                            a_b_r[e], a_w_i[e], a_b_i[e], a_lambda[e], b_mu[e], b_w0[e], b_w_up[e], b_a0[e],
                            b_a_up[e], b_g_up[e], b_k_k[e], b_k_a[e], b_r_k[e].reshape(-1), b_ln_w[e],
                            b_ln_b[e], even_w_out[e])
        else:
            o = layer // 2
            h = _odd_layer(h, bsz, seq, odd_norm[o], odd_w_in[o], c_conv_w[o], c_conv_b[o], c_w_q[o], c_w_k[o],
                           c_w_v[o], c_w_if[o], c_b_if[o], c_ln_w[o], c_skip[o], odd_w_out[o])
        h = _ffn(h, _row(ffn_norm[layer]), ffn_w_gate[layer].astype(BF16), ffn_w_up[layer].astype(BF16),
                 ffn_conv_w[layer], _row(ffn_conv_b[layer]), ffn_w_down[layer].astype(BF16),
                 _row(final_norm), seq, FFN_TM, FFN_TF, layer == depth - 1, "ffn%d" % layer)
    return h.reshape(bsz, seq, d)
```

```python
import functools
import math

import jax
import jax.numpy as jnp
from jax import lax
from jax.experimental import pallas as pl
from jax.experimental.pallas import tpu as pltpu

F32 = jnp.float32
BF16 = jnp.bfloat16

EPS = 1e-6
D_MODEL = 2048
A_WIDTH = 1024
A_BLOCKS = 8
A_CONV = 4
LRU_C = 8.0
B_WIDTH = 1024
B_HEAD = 64
B_DECAY_RANK = 64
B_AAA_RANK = 64
B_GATE_RANK = 160
B_SMALL = 512
B_SMALL_K = 384
B_LN_EPS = 64e-5
C_WIDTH = 4096
C_HEADS = 8
C_HEAD = 512
C_QKV_BLOCK = 4
C_CONV = 4
D_FF = 5632
FFN_CONV = 3

SUBLANES = 8
LANES = 128
MXU_DIM = 256
VMEM_LIMIT = 56 * 1024 * 1024

RWKV_CHUNK = 64
RWKV_GROUP = 4 * B_HEAD
RWKV_ROWS = 512
MLSTM_CHUNK = 256
MLSTM_HEADS_PER_STEP = 8
PROJ_TM, PROJ_TN = 1024, 512
NORM_SUB_ROWS = 256
PROJ_TN_WIDE = 1024
FFN_TM, FFN_TF = 512, 512
RGLRU_GROUP = MXU_DIM
MLSTM_IN_COLS, MLSTM_IN_SUB = 512, 256


def _params(sem):
    return pltpu.CompilerParams(dimension_semantics=sem, vmem_limit_bytes=VMEM_LIMIT)


def _dot(a, b):
    return jnp.dot(a, b, preferred_element_type=F32)


def _dot_nt(a, b):
    return lax.dot_general(a, b, (((1,), (1,)), ((), ())), preferred_element_type=F32)


def _dot_tn(a, b):
    return lax.dot_general(a, b, (((0,), (0,)), ((), ())), preferred_element_type=F32)


def _split_bf16(x, terms):
    parts = []
    for _ in range(terms):
        p = x.astype(BF16)
        parts.append(p)
        x = x - p.astype(F32)
    return parts


def _sigmoid(x):
    return 1.0 / (1.0 + jnp.exp(-x))


def _silu(x):
    return x * _sigmoid(x)


def _rms(x, g):
    return x * lax.rsqrt(jnp.mean(x * x, axis=-1, keepdims=True) + EPS) * g


def _shift_rows(x, k, prev8):
    r = pltpu.roll(x, k, 0)
    fix = pltpu.roll(prev8, k, 0)
    row = lax.broadcasted_iota(jnp.int32, (SUBLANES, x.shape[1]), 0)
    head = jnp.where(row < k, fix, r[:SUBLANES])
    return jnp.concatenate([head, r[SUBLANES:]], axis=0)


def _norm_matmul_kernel(x_ref, g_ref, w_ref, o_ref, hn_ref, *, silu):
    j = pl.program_id(1)
    tm = x_ref.shape[0]

    def store(rows, y):
        o_ref[rows, :] = (_silu(y) if silu else y).astype(o_ref.dtype)

    @pl.when(j == 0)
    def _():
        for s in range(tm // NORM_SUB_ROWS):
            rows = pl.ds(s * NORM_SUB_ROWS, NORM_SUB_ROWS)
            hn = _rms(x_ref[rows, :], g_ref[...]).astype(BF16)
            hn_ref[rows, :] = hn
            store(rows, _dot(hn, w_ref[...]))

    @pl.when(j != 0)
    def _():
        store(pl.ds(0, tm), _dot(hn_ref[...], w_ref[...]))


def _norm_matmul(x, g, w, tm, tn, name, col_start=0, silu_bf16=False):
    n, d = x.shape
    nout = w.shape[1] - col_start
    off = col_start // tn
    assert col_start % tn == 0 and nout % tn == 0
    return pl.pallas_call(
        functools.partial(_norm_matmul_kernel, silu=silu_bf16),
        grid=(n // tm, nout // tn),
        in_specs=[pl.BlockSpec((tm, d), lambda i, j: (i, 0)),
                  pl.BlockSpec((1, d), lambda i, j: (0, 0)),
                  pl.BlockSpec((d, tn), lambda i, j: (0, j + off))],
        out_specs=pl.BlockSpec((tm, tn), lambda i, j: (i, j)),
        out_shape=jax.ShapeDtypeStruct((n, nout), BF16 if silu_bf16 else F32),
        scratch_shapes=[pltpu.VMEM((tm, d), BF16)],
        compiler_params=_params(("arbitrary", "arbitrary")),
        name=name,
    )(x, g, w)


def _resid_matmul_kernel(*refs, n_in):
    x_ref = refs[0]
    a_refs = refs[1:1 + n_in]
    w_refs = refs[1 + n_in:1 + 2 * n_in]
    o_ref = refs[1 + 2 * n_in]
    acc = x_ref[...]
    for a_ref, w_ref in zip(a_refs, w_refs):
        acc = acc + _dot(a_ref[...], w_ref[...])
    o_ref[...] = acc


def _resid_matmul(x, acts, ws, tm, tn, name):
    n, d = x.shape
    n_in = len(acts)
    in_specs = [pl.BlockSpec((tm, tn), lambda i, j: (i, j))]
    in_specs += [pl.BlockSpec((tm, a.shape[1]), lambda i, j: (i, 0)) for a in acts]
    in_specs += [pl.BlockSpec((w.shape[0], tn), lambda i, j: (0, j)) for w in ws]
    return pl.pallas_call(
        functools.partial(_resid_matmul_kernel, n_in=n_in),
        grid=(n // tm, d // tn),
        in_specs=in_specs,
        out_specs=pl.BlockSpec((tm, tn), lambda i, j: (i, j)),
        out_shape=jax.ShapeDtypeStruct((n, d), F32),
        compiler_params=_params(("arbitrary", "arbitrary")),
        name=name,
    )(x, *acts, *ws)


def _ffn_kernel(x_ref, g_ref, wg_ref, wu_ref, cw_ref, cb_ref, wd_ref, fg_ref, o_ref, hn_ref, carry_ref,
                *, tiles_per_seq, final_norm):
    i = pl.program_id(0)
    j = pl.program_id(1)
    tm = x_ref.shape[0]

    @pl.when(j == 0)
    def _():
        x = x_ref[...]
        hn_ref[...] = _rms(x, g_ref[...]).astype(BF16)
        o_ref[...] = x

    hn = hn_ref[...]
    gate = _dot(hn, wg_ref[...])
    up = _dot(hn, wu_ref[...])
    seq_start = (i % tiles_per_seq) == 0
    prev8 = jnp.where(seq_start, 0.0, carry_ref[j])
    carry_ref[j] = gate[tm - SUBLANES:]
    cw = cw_ref[...]
    conv = (cb_ref[...] + gate * cw[2:3] + _shift_rows(gate, 1, prev8) * cw[1:2]
            + _shift_rows(gate, 2, prev8) * cw[0:1])
    u = (_silu(conv) * up).astype(BF16)
    o_ref[...] += _dot(u, wd_ref[...])

    if final_norm:
        @pl.when(j == pl.num_programs(1) - 1)
        def _():
            o_ref[...] = _rms(o_ref[...], fg_ref[...])


def _ffn(x, g, wg, wu, cw, cb, wd, fg, seq, tm, tf, final_norm, name):
    n, d = x.shape
    f = wg.shape[1]
    nf = f // tf
    kern = functools.partial(_ffn_kernel, tiles_per_seq=seq // tm, final_norm=final_norm)
    return pl.pallas_call(
        kern,
        grid=(n // tm, nf),
        in_specs=[pl.BlockSpec((tm, d), lambda i, j: (i, 0)),
                  pl.BlockSpec((1, d), lambda i, j: (0, 0)),
                  pl.BlockSpec((d, tf), lambda i, j: (0, j)),
                  pl.BlockSpec((d, tf), lambda i, j: (0, j)),
                  pl.BlockSpec((FFN_CONV, tf), lambda i, j: (0, j)),
                  pl.BlockSpec((1, tf), lambda i, j: (0, j)),
                  pl.BlockSpec((tf, d), lambda i, j: (j, 0)),
                  pl.BlockSpec((1, d), lambda i, j: (0, 0))],
        out_specs=pl.BlockSpec((tm, d), lambda i, j: (i, 0)),
        out_shape=jax.ShapeDtypeStruct((n, d), F32),
        scratch_shapes=[pltpu.VMEM((tm, d), BF16), pltpu.VMEM((nf, SUBLANES, tf), F32)],
        compiler_params=_params(("arbitrary", "arbitrary")),
        name=name,
    )(x, g, wg, wu, cw, cb, wd, fg)


def _rglru_gates(x, prev8, cw, cb, wr, wi):
    xc = cb + x * cw[A_CONV - 1:A_CONV]
    for k in range(1, A_CONV):
        xc = xc + _shift_rows(x, k, prev8) * cw[A_CONV - 1 - k:A_CONV - k]
    xb = xc.astype(BF16)
    return xc, _dot(xb, wr), _dot(xb, wi)


def _rglru_scan(xc, r_pre, i_pre, ga, h0, br, bi, lam):
    rows, width = xc.shape
    r = _sigmoid(r_pre + br)
    ig = _sigmoid(i_pre + bi)
    neg_lam = -lam
    softplus = jnp.maximum(neg_lam, 0.0) + jnp.log1p(jnp.exp(-jnp.abs(neg_lam)))
    log_a = (-LRU_C) * r * softplus
    a = jnp.exp(log_a)
    u = jnp.sqrt(1.0 - jnp.exp(2.0 * log_a)) * (ig * xc)

    n_sub = rows // SUBLANES
    a3 = a.reshape(n_sub, SUBLANES, width)
    u3 = u.reshape(n_sub, SUBLANES, width)
    sub = lax.broadcasted_iota(jnp.int32, (n_sub, SUBLANES, width), 1)
    s = 1
    while s < SUBLANES:
        keep = sub >= s
        a_prev = pltpu.roll(a3, s, 1)
        u_prev = pltpu.roll(u3, s, 1)
        u3 = jnp.where(keep, a3 * u_prev + u3, u3)
        a3 = jnp.where(keep, a3 * a_prev, a3)
        s *= 2
    carry = h0
    groups = []
    for gi in range(n_sub):
        hg = u3[gi] + a3[gi] * carry
        groups.append(hg)
        carry = hg[SUBLANES - 1:]
    h = jnp.concatenate(groups, axis=0)
    gelu = 0.5 * ga * (1.0 + jnp.tanh(math.sqrt(2.0 / math.pi) * (ga + 0.044715 * (ga * ga * ga))))
    return (h * gelu).astype(BF16), carry


def _even_in_kernel(x_ref, g_ref, w_ref, cw_ref, cb_ref, wr_ref, br_ref, wi_ref, bi_ref, lam_ref,
                    p_ref, ya_ref, hn_ref, act_ref, prev_ref, h_ref, *, tiles_per_seq):
    i = pl.program_id(0)
    j = pl.program_id(1)
    tm = x_ref.shape[0]
    tn = w_ref.shape[1]
    grp = RGLRU_GROUP
    n_units = A_WIDTH // grp
    n_act = 2 * A_WIDTH // tn
    per_tile = tn // grp

    def stash(rows, y, tile):
        for part in range(per_tile):
            act_ref[per_tile * tile + part, rows, :] = y[:, part * grp:(part + 1) * grp]

    @pl.when(j == 0)
    def _():
        for s in range(tm // NORM_SUB_ROWS):
            rows = pl.ds(s * NORM_SUB_ROWS, NORM_SUB_ROWS)
            hn = _rms(x_ref[rows, :], g_ref[...]).astype(BF16)
            hn_ref[rows, :] = hn
            stash(rows, _dot(hn, w_ref[...]), 0)

    @pl.when((j > 0) & (j < n_act))
    def _():
        stash(pl.ds(0, tm), _dot(hn_ref[...], w_ref[...]), j)

    @pl.when((j >= n_act) & (j < n_act + n_units))
    def _():
        unit = j - n_act
        seq_start = (i % tiles_per_seq) == 0
        prev8 = jnp.where(seq_start, 0.0, prev_ref[unit])
        h_last = jnp.where(seq_start, 0.0, h_ref[unit])
        gate_w = (cw_ref[unit], cb_ref[unit], wr_ref[unit], wi_ref[unit])
        scan_w = (br_ref[unit], bi_ref[unit], lam_ref[unit])
        sub = lambda s: pl.ds(s * NORM_SUB_ROWS, NORM_SUB_ROWS)
        for s in range(tm // NORM_SUB_ROWS):
            x = act_ref[unit, sub(s), :]
            gated = _rglru_gates(x, prev8, *gate_w)
            prev8 = x[NORM_SUB_ROWS - SUBLANES:]
            projected = _dot(hn_ref[sub(s), :], w_ref[...])
            y, h_last = _rglru_scan(*gated, act_ref[n_units + unit, sub(s), :], h_last, *scan_w)
            p_ref[sub(s), :] = projected
            ya_ref[sub(s), :] = y
        prev_ref[unit] = prev8
        h_ref[unit] = h_last

    @pl.when(j >= n_act + n_units)
    def _():
        p_ref[...] = _dot(hn_ref[...], w_ref[...])


def _even_in(x, g, w_all, cw, cb, wr, br, wi, bi, lam, seq, tm, tn):
    n, d = x.shape
    grp = RGLRU_GROUP
    n_units = A_WIDTH // grp
    n_tiles = w_all.shape[1] // tn
    n_act = 2 * A_WIDTH // tn
    assert (2 * A_WIDTH) % tn == 0 and tn % grp == 0 and n_tiles >= n_act + n_units
    group_vec = lambda v: v.reshape(n_units, 1, grp)
    whole = lambda a: pl.BlockSpec(a.shape, lambda i, j: (0,) * a.ndim)
    params = [jnp.transpose(cw.reshape(A_CONV, n_units, grp), (1, 0, 2)), group_vec(cb), wr, group_vec(br),
              wi, group_vec(bi), group_vec(lam)]
    return pl.pallas_call(
        functools.partial(_even_in_kernel, tiles_per_seq=seq // tm),
        grid=(n // tm, n_tiles),
        in_specs=[pl.BlockSpec((tm, d), lambda i, j: (i, 0)),
                  pl.BlockSpec((1, d), lambda i, j: (0, 0)),
                  pl.BlockSpec((d, tn), lambda i, j: (0, j))] + [whole(a) for a in params],
        out_specs=[pl.BlockSpec((tm, tn), lambda i, j: (i, jnp.maximum(j - n_act, 0))),
                   pl.BlockSpec((tm, grp), lambda i, j: (i, jnp.clip(j - n_act, 0, n_units - 1)))],
        out_shape=[jax.ShapeDtypeStruct((n, w_all.shape[1] - 2 * A_WIDTH), F32),
                   jax.ShapeDtypeStruct((n, A_WIDTH), BF16)],
        scratch_shapes=[pltpu.VMEM((tm, d), BF16), pltpu.VMEM((2 * n_units, tm, grp), F32),
                        pltpu.VMEM((n_units, SUBLANES, grp), F32), pltpu.VMEM((n_units, 1, grp), F32)],
        compiler_params=_params(("arbitrary", "arbitrary")),
        name="even_in",
    )(x, g, w_all, *params)


def _alternate(*stage_streams):
    results = [None] * len(stage_streams)
    live = list(range(len(stage_streams)))
    while live:
        for idx in list(live):
            try:
                next(stage_streams[idx])
            except StopIteration as stop:
                results[idx] = stop.value
                live.remove(idx)
    return results


def _rwkv_kernel(r_ref, k_ref, v_ref, sm_ref, mur_ref, muk_ref, muv_ref, mus_ref, w0_ref, wup_ref, a0_ref,
                 aup_ref, gup_ref, kkw_ref, kaw_ref, rkw_ref, lnw_ref, lnb_ref, o_ref,
                 s_ref, pr_ref, pk_ref, pv_ref, ps_ref):
    tt = r_ref.shape[0]
    L = RWKV_CHUNK
    gw = RWKV_GROUP
    n_grp = B_WIDTH // gw
    hr = tt // 2

    @pl.when(pl.program_id(1) == 0)
    def _():
        s_ref[...] = jnp.zeros_like(s_ref)
        pr_ref[...] = jnp.zeros_like(pr_ref)
        pk_ref[...] = jnp.zeros_like(pk_ref)
        pv_ref[...] = jnp.zeros_like(pv_ref)
        ps_ref[...] = jnp.zeros_like(ps_ref)

    def lerp(x_ref, p_ref, mu_ref):
        x = x_ref[...]
        xs = _shift_rows(x, 1, p_ref[...])
        p_ref[...] = x[tt - SUBLANES:]
        return x + (xs - x) * mu_ref[...]

    r_all = lerp(r_ref, pr_ref, mur_ref)
    k_all = lerp(k_ref, pk_ref, muk_ref)
    v_all = lerp(v_ref, pv_ref, muv_ref)
    sm_all = lerp(sm_ref, ps_ref, mus_ref)[:, :B_SMALL_K]

    row_g = lax.broadcasted_iota(jnp.int32, (gw, gw), 0)
    col_g = lax.broadcasted_iota(jnp.int32, (gw, gw), 1)
    same_head = (row_g // B_HEAD) == (col_g // B_HEAD)
    head_mask = jnp.where(same_head, 1.0, 0.0)
    head_mask_bf = head_mask.astype(BF16)

    def head_sum(x, terms):
        n = x.shape[0]
        xs = jnp.concatenate([x[:, gi * gw:(gi + 1) * gw] for gi in range(n_grp)], axis=0)
        s = _dot(jnp.concatenate(_split_bf16(xs, terms), axis=0), head_mask_bf)
        s = sum(s[t * n_grp * n:(t + 1) * n_grp * n] for t in range(terms))
        return jnp.concatenate([s[gi * n:(gi + 1) * n] for gi in range(n_grp)], axis=1)

    def bd(x):
        return jnp.concatenate([x.astype(BF16)] * (gw // B_HEAD), axis=0) * head_mask_bf

    row_t = lax.broadcasted_iota(jnp.int32, (hr, hr), 0)
    col_t = lax.broadcasted_iota(jnp.int32, (hr, hr), 1)
    tril = jnp.where((row_t >= col_t) & ((row_t // L) == (col_t // L)), 1.0, 0.0).astype(BF16)
    row_p = lax.broadcasted_iota(jnp.int32, (L, gw), 0)
    src_p = lax.broadcasted_iota(jnp.int32, (L, gw), 1) % B_HEAD
    strict_lower = row_p > src_p
    lower = row_p >= src_p
    n_doublings = int(math.log2(L)) - 1
    groups = range(n_grp)
    half_units = [(c, gi) for c in range(hr // L) for gi in groups]

    def tile(arr, c, gi):
        return arr[c * L:(c + 1) * L, gi * gw:(gi + 1) * gw]

    def prepare(h):
        rows = slice(h * hr, (h + 1) * hr)
        r, k, v, sm = r_all[rows], k_all[rows], v_all[rows], sm_all[rows]
        z = w0_ref[...] + _dot(jnp.tanh(sm).astype(BF16), wup_ref[...])
        log_w = (-math.exp(-0.5)) * _sigmoid(z)
        yield
        a = _sigmoid(a0_ref[...] + _dot(sm.astype(BF16), aup_ref[...]))
        yield
        g = _dot(_sigmoid(sm).astype(BF16), gup_ref[...])
        kk = k * kkw_ref[...]
        yield
        kk = kk * lax.rsqrt(jnp.maximum(head_sum(kk * kk, 2), 1e-12))
        yield
        k2 = k * (1.0 + (a - 1.0) * kaw_ref[...])
        yield
        cum = sum(_dot(tril, part) for part in _split_bf16(log_w, 3))
        yield
        p_in = jnp.exp(cum)
        p_inv = jnp.exp(-cum)
        yield
        a_bar = (-kk) * jnp.exp(cum - log_w)
        r_bar = r * p_in
        yield
        b_bar = kk * a * p_inv
        k_bar = k2 * p_inv
        return dict(r=r, v=v, g=g, k2=k2, p_in=p_in, a_bar=a_bar, r_bar=r_bar, b_bar=b_bar, k_bar=k_bar)

    def products(pre, states):
        units = half_units
        ar = {u: jnp.concatenate([tile(pre["a_bar"], *u), tile(pre["r_bar"], *u)], axis=0).astype(BF16)
              for u in units}
        m_b = {u: _dot_nt(ar[u], bd(tile(pre["b_bar"], *u))) for u in units}
        yield
        m_k = {u: _dot_nt(ar[u], bd(tile(pre["k_bar"], *u))) for u in units}
        yield
        x = {u: jnp.where(strict_lower, m_b[u][:L], 0.0) for u in units}
        a_rb = {u: jnp.where(lower, m_b[u][L:], 0.0).astype(BF16) for u in units}
        akrk = {u: jnp.concatenate([jnp.where(strict_lower, m_k[u][:L], 0.0),
                                    jnp.where(lower, m_k[u][L:], 0.0)], axis=0).astype(BF16) for u in units}
        cy = {u: _dot(akrk[u], bd(tile(pre["v"], *u))) for u in units}
        yield
        n_inv = dict(x)
        x_pow = {u: _dot(x[u].astype(BF16), bd(x[u])) for u in units}
        yield
        for step in range(n_doublings):
            if step + 1 < n_doublings:
                both = {u: _dot(jnp.concatenate([x_pow[u], n_inv[u]], axis=0).astype(BF16), bd(x_pow[u]))
                        for u in units}
                n_inv = {u: n_inv[u] + x_pow[u] + both[u][L:] for u in units}
                x_pow = {u: both[u][:L] for u in units}
            else:
                n_inv = {u: n_inv[u] + x_pow[u] + _dot(n_inv[u].astype(BF16), bd(x_pow[u])) for u in units}
            yield
        n_inv = {u: n_inv[u].astype(BF16) for u in units}
        y_rows = []
        for c in range(hr // L):
            p_last = pre["p_in"][(c + 1) * L - 1:(c + 1) * L]
            pl_g = [p_last[:, gi * gw:(gi + 1) * gw] for gi in groups]
            bk = [jnp.concatenate([tile(pre["b_bar"], c, gi) * pl_g[gi], tile(pre["k_bar"], c, gi) * pl_g[gi]],
                                  axis=0).astype(BF16) for gi in groups]
            ah = [_dot_nt(ar[c, gi], states[gi].astype(BF16)) for gi in groups]
            yield
            rhs = [ah[gi][:L] + cy[c, gi][:L] for gi in groups]
            u_c = [rhs[gi] + _dot(n_inv[c, gi], bd(rhs[gi])) for gi in groups]
            yield
            ds = [_dot_tn(jnp.concatenate([u_c[gi], tile(pre["v"], c, gi)], axis=0).astype(BF16), bk[gi])
                  for gi in groups]
            states = [states[gi] * pl_g[gi] + head_mask * ds[gi] for gi in groups]
            y_rows.append(jnp.concatenate(
                [ah[gi][L:] + cy[c, gi][L:] + _dot(a_rb[c, gi], bd(u_c[gi])) for gi in groups], axis=1))
            yield
        return jnp.concatenate(y_rows, axis=0), states

    def finish(h, pre, y):
        inv_n = 1.0 / B_HEAD
        yc = y - head_sum(y, 1) * inv_n
        yield
        var = head_sum(yc * yc, 1) * inv_n
        yield
        yn = yc * lax.rsqrt(var + B_LN_EPS) * lnw_ref[...] + lnb_ref[...]
        yield
        bonus = head_sum(pre["r"] * pre["k2"] * rkw_ref[...], 1) * pre["v"]
        yield
        o_ref[pl.ds(h * hr, hr), :] = ((yn + bonus) * pre["g"]).astype(o_ref.dtype)

    states = [s_ref[gi] for gi in groups]
    (pre0,) = _alternate(prepare(0))
    (y0, states), pre1 = _alternate(products(pre0, states), prepare(1))
    (y1, states), _ = _alternate(products(pre1, states), finish(0, pre0, y0))
    _alternate(finish(1, pre1, y1))
    for gi in groups:
        s_ref[gi] = states[gi]


def _rwkv(p, mur, muk, muv, mus, w0, wup, a0, aup, gup, kkw, kaw, rkw, lnw, lnb, bsz, seq):
    n = p.shape[0]
    tt = RWKV_ROWS
    nt = seq // tt
    w = B_WIDTH
    vec = lambda: pl.BlockSpec((1, w), lambda b, t: (0, 0))
    mat = lambda: pl.BlockSpec((B_SMALL_K, w), lambda b, t: (0, 0))
    return pl.pallas_call(
        _rwkv_kernel,
        grid=(bsz, nt),
        in_specs=[pl.BlockSpec((tt, w), lambda b, t: (b * nt + t, 0)),
                  pl.BlockSpec((tt, w), lambda b, t: (b * nt + t, 1)),
                  pl.BlockSpec((tt, w), lambda b, t: (b * nt + t, 2)),
                  pl.BlockSpec((tt, B_SMALL), lambda b, t: (b * nt + t, 3 * w // B_SMALL)),
                  vec(), vec(), vec(), pl.BlockSpec((1, B_SMALL), lambda b, t: (0, 0)),
                  vec(), mat(), vec(), mat(), mat(), vec(), vec(), vec(), vec(), vec()],
        out_specs=pl.BlockSpec((tt, w), lambda b, t: (b * nt + t, 0)),
        out_shape=jax.ShapeDtypeStruct((n, w), BF16),
        scratch_shapes=[pltpu.VMEM((w // RWKV_GROUP, RWKV_GROUP, RWKV_GROUP), F32),
                        pltpu.VMEM((SUBLANES, w), F32), pltpu.VMEM((SUBLANES, w), F32),
                        pltpu.VMEM((SUBLANES, w), F32), pltpu.VMEM((SUBLANES, B_SMALL), F32)],
        compiler_params=_params(("arbitrary", "arbitrary")),
        name="rwkv7",
    )(p, p, p, p, mur, muk, muv, mus, w0, wup, a0, aup, gup, kkw, kaw, rkw, lnw, lnb)


def _mlstm_in_kernel(x_ref, g_ref, w_ref, cw_ref, cb_ref, wq_ref, wk_ref, wv_ref, wif_ref, bif_ref,
                     q_ref, k_ref, v_ref, xc_ref, gates_ref, hn_ref, prev_ref, *, tiles_per_seq, sub_rows):
    i = pl.program_id(0)
    j = pl.program_id(1)
    tm = x_ref.shape[0]
    cb = w_ref.shape[1]
    n_grp = cb // MXU_DIM

    def blockdiag(xb, wb_ref):
        return jnp.concatenate(
            [_dot(xb[:, g * MXU_DIM:(g + 1) * MXU_DIM], wb_ref[g]) for g in range(n_grp)], axis=1)

    def body(first_col_tile):
        def project(s):
            rows = pl.ds(s * sub_rows, sub_rows)
            if first_col_tile:
                hn = _rms(x_ref[rows, :], g_ref[...]).astype(BF16)
                hn_ref[rows, :] = hn
            else:
                hn = hn_ref[rows, :]
            return _dot(hn, w_ref[...])

        seq_start = (i % tiles_per_seq) == 0
        prev8 = jnp.where(seq_start, 0.0, prev_ref[j])
        cw = cw_ref[...]
        n_sub = tm // sub_rows
        xm_next = project(0)
        for s in range(n_sub):
            rows = pl.ds(s * sub_rows, sub_rows)
            xm = xm_next
            if s + 1 < n_sub:
                xm_next = project(s + 1)
            conv = cb_ref[...] + xm * cw[C_CONV - 1:C_CONV]
            for kk in range(1, C_CONV):
                conv = conv + _shift_rows(xm, kk, prev8) * cw[C_CONV - 1 - kk:C_CONV - kk]
            prev8 = xm[sub_rows - SUBLANES:]
            xc = _silu(conv)
            xcb = xc.astype(BF16)
            xc_ref[rows, :] = xcb
            q = blockdiag(xcb, wq_ref)
            k = blockdiag(xcb, wk_ref)
            v = blockdiag(xm.astype(BF16), wv_ref)
            qb = q.astype(BF16)
            kb = k.astype(BF16)
            vb = v.astype(BF16)
            q_ref[rows, :] = qb
            k_ref[rows, :] = (k * (C_HEAD ** -0.5)).astype(BF16)
            v_ref[rows, :] = vb
            gate_part = _dot(qb, wif_ref[0]) + _dot(kb, wif_ref[1]) + _dot(vb, wif_ref[2])
            if first_col_tile:
                gates_ref[rows, :] = bif_ref[...] + gate_part
            else:
                gates_ref[rows, :] += gate_part
        prev_ref[j] = prev8

    pl.when(j == 0)(functools.partial(body, True))
    pl.when(j != 0)(functools.partial(body, False))


def _mlstm_in(x, g, w_in, cw, cb, wq, wk, wv, wif, bif, seq, tm, cblk, sub_rows):
    n, d = x.shape
    ncb = C_WIDTH // cblk
    gpb = cblk // MXU_DIM
    blk = lambda: pl.BlockSpec((tm, cblk), lambda i, j: (i, j))
    wspec = lambda: pl.BlockSpec((gpb, MXU_DIM, MXU_DIM), lambda i, j: (j, 0, 0))
    act = lambda dt: jax.ShapeDtypeStruct((n, C_WIDTH), dt)
    kern = functools.partial(_mlstm_in_kernel, tiles_per_seq=seq // tm, sub_rows=sub_rows)
    return pl.pallas_call(
        kern,
        grid=(n // tm, ncb),
        in_specs=[pl.BlockSpec((tm, d), lambda i, j: (i, 0)),
                  pl.BlockSpec((1, d), lambda i, j: (0, 0)),
                  pl.BlockSpec((d, cblk), lambda i, j: (0, j)),
                  pl.BlockSpec((C_CONV, cblk), lambda i, j: (0, j)),
                  pl.BlockSpec((1, cblk), lambda i, j: (0, j)),
                  wspec(), wspec(), wspec(),
                  pl.BlockSpec((3, cblk, 2 * C_HEADS), lambda i, j: (0, j, 0)),
                  pl.BlockSpec((1, 2 * C_HEADS), lambda i, j: (0, 0))],
        out_specs=[blk(), blk(), blk(), blk(),
                   pl.BlockSpec((tm, 2 * C_HEADS), lambda i, j: (i, 0))],
        out_shape=[act(BF16), act(BF16), act(BF16), act(BF16),
                   jax.ShapeDtypeStruct((n, 2 * C_HEADS), F32)],
        scratch_shapes=[pltpu.VMEM((tm, d), BF16), pltpu.VMEM((ncb, SUBLANES, cblk), F32)],
        compiler_params=_params(("arbitrary", "arbitrary")),
        name="mlstm_in",
    )(x, g, w_in, cw, cb, wq, wk, wv, wif, bif)


def _mlstm_gate_kernel(f_ref, b_ref):
    L = f_ref.shape[1]
    f = f_ref[...]
    lf = jnp.minimum(f, 0.0) - jnp.log1p(jnp.exp(-jnp.abs(f)))
    row = lax.broadcasted_iota(jnp.int32, (L, L), 0)
    col = lax.broadcasted_iota(jnp.int32, (L, L), 1)
    triu = jnp.where(row <= col, 1.0, 0.0).astype(BF16)
    b_ref[...] = sum(_dot(part, triu) for part in _split_bf16(lf, 3))


def _mlstm_gates(f_pre, chunk):
    rows, seq = f_pre.shape
    return pl.pallas_call(
        _mlstm_gate_kernel,
        grid=(seq // chunk,),
        in_specs=[pl.BlockSpec((rows, chunk), lambda c: (0, c))],
        out_specs=pl.BlockSpec((rows, chunk), lambda c: (0, c)),
        out_shape=jax.ShapeDtypeStruct((rows, seq), F32),
        compiler_params=_params(("arbitrary",)),
        name="mlstm_gates",
    )(f_pre)


def _mlstm_kernel(q_ref, k_ref, v_ref, xc_ref, za_ref, ir_ref, br_ref, lnw_ref, skip_ref,
                  o_ref, ct_ref, m_ref):
    c = pl.program_id(2)
    L = q_ref.shape[0]
    n_heads, hd = ct_ref.shape[0], ct_ref.shape[1]

    @pl.when(c == 0)
    def _():
        ct_ref[...] = jnp.zeros_like(ct_ref)
        m_ref[...] = jnp.zeros_like(m_ref)

    def lanes(x, width):
        return jnp.concatenate([x] * (width // LANES), axis=1)

    row = lax.broadcasted_iota(jnp.int32, (L, L), 0)
    col = lax.broadcasted_iota(jnp.int32, (L, L), 1)
    causal = row >= col
    eye = jnp.where(row == col, 1.0, 0.0).astype(BF16)

    def to_col(x_row):
        return sum(_dot_nt(eye, jnp.broadcast_to(part, (LANES, L))) for part in _split_bf16(x_row, 3))

    heads = range(n_heads)
    cols = [slice(i * hd, (i + 1) * hd) for i in heads]
    q = [q_ref[:, cols[i]] for i in heads]
    k = [k_ref[:, cols[i]] for i in heads]
    v_aug = [jnp.concatenate([v_ref[:, cols[i]], jnp.ones((L, LANES), BF16)], axis=1) for i in heads]
    li_row = [ir_ref[i] for i in heads]
    b_row = [br_ref[i] for i in heads]
    m_prev = [m_ref[i] for i in heads]

    qk = [_dot_nt(q[i], k[i]) for i in heads]
    q_ct = [_dot(q[i], ct_ref[i].astype(BF16)) for i in heads]
    b_col = [to_col(b_row[i]) for i in heads]
    li_col = [to_col(li_row[i]) for i in heads]
    b_last = [b_col[i][L - 1:L] for i in heads]

    d_log = [jnp.where(causal, lanes(b_col[i], L) - b_row[i] + li_row[i], -jnp.inf) for i in heads]
    inter = [b_col[i] + m_prev[i] for i in heads]
    m_t = [jnp.maximum(inter[i], jnp.max(d_log[i], axis=-1, keepdims=True)) for i in heads]
    s = [(qk[i] * jnp.exp(d_log[i] - lanes(m_t[i], L))).astype(BF16) for i in heads]
    sc = [jnp.exp(inter[i] - m_t[i]) for i in heads]
    num_den = [_dot(s[i], v_aug[i]) + lanes(sc[i], hd + LANES) * q_ct[i] for i in heads]

    g_log = [b_last[i] - b_col[i] + li_col[i] for i in heads]
    m_new = [jnp.maximum(b_last[i] + m_prev[i], jnp.max(g_log[i], axis=0, keepdims=True)) for i in heads]
    ke = [k[i] * lanes(jnp.exp(g_log[i] - m_new[i]).astype(BF16), hd) for i in heads]
    decay = [jnp.exp(b_last[i] + m_prev[i] - m_new[i]) for i in heads]
    for i in heads:
        ct_ref[i] = lanes(decay[i], hd + LANES) * ct_ref[i] + _dot_tn(ke[i], v_aug[i])
        m_ref[i] = m_new[i]

    for i in heads:
        inv = 1.0 / jnp.maximum(jnp.abs(num_den[i][:, hd:]), jnp.exp(-m_t[i]))
        h = num_den[i][:, :hd] * lanes(inv, hd)
        hc = h - jnp.mean(h, axis=-1, keepdims=True)
        hn = hc * lax.rsqrt(jnp.mean(hc * hc, axis=-1, keepdims=True) + EPS) * lnw_ref[:, cols[i]]
        hs = hn + skip_ref[:, cols[i]] * xc_ref[:, cols[i]].astype(F32)
        o_ref[:, cols[i]] = (hs * za_ref[:, cols[i]].astype(F32)).astype(o_ref.dtype)


def _mlstm(q, k, v, xc, z_act, i_row, b_row, lnw, skip, bsz, seq):
    n = q.shape[0]
    L = MLSTM_CHUNK
    nc = seq // L
    hd = C_HEAD
    hp = MLSTM_HEADS_PER_STEP
    blk = lambda: pl.BlockSpec((L, hp * hd), lambda b, h, c: (b * nc + c, h))
    rowspec = lambda: pl.BlockSpec((None, hp, 1, L), lambda b, h, c: (b, h, 0, c))
    vec = lambda: pl.BlockSpec((1, hp * hd), lambda b, h, c: (0, h))
    return pl.pallas_call(
        _mlstm_kernel,
        grid=(bsz, C_HEADS // hp, nc),
        in_specs=[blk(), blk(), blk(), blk(), blk(), rowspec(), rowspec(), vec(), vec()],
        out_specs=blk(),
        out_shape=jax.ShapeDtypeStruct((n, C_WIDTH), BF16),
        scratch_shapes=[pltpu.VMEM((hp, hd, hd + LANES), F32), pltpu.VMEM((hp, 1, LANES), F32)],
        compiler_params=_params(("arbitrary", "arbitrary", "arbitrary")),
        name="mlstm",
    )(q, k, v, xc, z_act, i_row, b_row, lnw, skip)


def _pack_block_diag(w, tile):
    g, bs, _ = w.shape
    per = tile // bs
    w = w.reshape(g // per, per, bs, bs)
    eye = jnp.eye(per, dtype=w.dtype)
    dense = jnp.einsum("npij,pq->npiqj", w, eye)
    return dense.reshape(g // per, tile, tile)


def _row(v):
    return v.reshape(1, -1)


def _even_layer(x, bsz, seq, norm, w_in, a_conv_w, a_conv_b, a_w_r, a_b_r, a_w_i, a_b_i, a_lambda,
                b_mu, b_w0, b_w_up, b_a0, b_a_up, b_g_up, b_k_k, b_k_a, b_r_k, b_ln_w, b_ln_b, w_out):
    main_w = 2 * A_WIDTH + 3 * B_WIDTH
    n_small = B_DECAY_RANK + B_AAA_RANK + B_GATE_RANK
    pad = B_SMALL - n_small
    assert main_w % B_SMALL == 0
    w_all = jnp.pad(w_in, ((0, 0), (0, pad))).astype(BF16)
    p, ya = _even_in(x, _row(norm), w_all, a_conv_w, a_conv_b,
                     _pack_block_diag(a_w_r, RGLRU_GROUP).astype(BF16), a_b_r,
                     _pack_block_diag(a_w_i, RGLRU_GROUP).astype(BF16), a_b_i, a_lambda, seq, PROJ_TM, PROJ_TN)

    mur, muk, muv = (_row(b_mu[i * B_WIDTH:(i + 1) * B_WIDTH]) for i in range(3))
    mus = _row(jnp.pad(b_mu[3 * B_WIDTH:], (0, pad)))

    def rows_at(w, start):
        out = jnp.zeros((B_SMALL_K, B_WIDTH), F32)
        return lax.dynamic_update_slice(out, w, (start, 0)).astype(BF16)

    wup = rows_at(b_w_up, 0)
    aup = rows_at(b_a_up, B_DECAY_RANK)
    gup = rows_at(b_g_up, B_DECAY_RANK + B_AAA_RANK)
    yb = _rwkv(p, mur, muk, muv, mus, _row(b_w0), wup, _row(b_a0), aup, gup, _row(b_k_k), _row(b_k_a),
               _row(b_r_k), _row(b_ln_w), _row(b_ln_b), bsz, seq)

    wo = w_out.astype(BF16)
    return _resid_matmul(x, [ya, yb], [wo[:A_WIDTH], wo[A_WIDTH:]], PROJ_TM, PROJ_TN_WIDE, "even_out")


def _odd_layer(x, bsz, seq, norm, w_in, conv_w, conv_b, w_q, w_k, w_v, w_if, b_if, ln_w, skip, w_out):
    w_in_b = w_in.astype(BF16)
    g = _row(norm)
    z_act = _norm_matmul(x, g, w_in_b, PROJ_TM, PROJ_TN_WIDE, "odd_in_z", col_start=C_WIDTH, silu_bf16=True)
    q, k, v, xc, gates = _mlstm_in(
        x, g, w_in_b, conv_w, _row(conv_b),
        _pack_block_diag(w_q, MXU_DIM).astype(BF16), _pack_block_diag(w_k, MXU_DIM).astype(BF16),
        _pack_block_diag(w_v, MXU_DIM).astype(BF16), w_if.astype(BF16), _row(b_if), seq,
        PROJ_TM, MLSTM_IN_COLS, MLSTM_IN_SUB)
    gt = jnp.transpose(gates.reshape(bsz, seq, 2, C_HEADS), (2, 0, 3, 1))
    i_pre = gt[0]
    b_cum = _mlstm_gates(gt[1].reshape(bsz * C_HEADS, seq), MLSTM_CHUNK).reshape(bsz, C_HEADS, seq)
    hs = _mlstm(q, k, v, xc, z_act, i_pre[:, :, None, :], b_cum[:, :, None, :], _row(ln_w), _row(skip),
                bsz, seq)
    return _resid_matmul(x, [hs], [w_out.astype(BF16)], PROJ_TM, PROJ_TN_WIDE, "odd_out")


def kernel(x, even_norm, even_w_in, a_conv_w, a_conv_b, a_w_r, a_b_r, a_w_i, a_b_i, a_lambda, b_mu, b_w0, b_w_up, b_a0, b_a_up, b_g_up, b_k_k, b_k_a, b_r_k, b_ln_w, b_ln_b, even_w_out, odd_norm, odd_w_in, c_conv_w, c_conv_b, c_w_q, c_w_k, c_w_v, c_w_if, c_b_if, c_ln_w, c_skip, odd_w_out, ffn_norm, ffn_w_gate, ffn_w_up, ffn_conv_w, ffn_conv_b, ffn_w_down, final_norm):
    bsz, seq, d = x.shape
    depth = ffn_norm.shape[0]
    h = x.reshape(bsz * seq, d)
    for layer in range(depth):
        if layer % 2 == 0:
            e = layer // 2
            h = _even_layer(h, bsz, seq, even_norm[e], even_w_in[e], a_conv_w[e], a_conv_b[e], a_w_r[e],
                            a_b_r[e], a_w_i[e], a_b_i[e], a_lambda[e], b_mu[e], b_w0[e], b_w_up[e], b_a0[e],
                            b_a_up[e], b_g_up[e], b_k_k[e], b_k_a[e], b_r_k[e].reshape(-1), b_ln_w[e],
                            b_ln_b[e], even_w_out[e])
        else:
            o = layer // 2
            h = _odd_layer(h, bsz, seq, odd_norm[o], odd_w_in[o], c_conv_w[o], c_conv_b[o], c_w_q[o], c_w_k[o],
                           c_w_v[o], c_w_if[o], c_b_if[o], c_ln_w[o], c_skip[o], odd_w_out[o])
        h = _ffn(h, _row(ffn_norm[layer]), ffn_w_gate[layer].astype(BF16), ffn_w_up[layer].astype(BF16),
                 ffn_conv_w[layer], _row(ffn_conv_b[layer]), ffn_w_down[layer].astype(BF16),
                 _row(final_norm), seq, FFN_TM, FFN_TF, layer == depth - 1, "ffn%d" % layer)
    return h.reshape(bsz, seq, d)
```

```python
import functools
import math

import jax
import jax.numpy as jnp
from jax import lax
from jax.experimental import pallas as pl
from jax.experimental.pallas import tpu as pltpu

F32 = jnp.float32
BF16 = jnp.bfloat16

EPS = 1e-6
D_MODEL = 2048
A_WIDTH = 1024
A_BLOCKS = 8
A_CONV = 4
LRU_C = 8.0
B_WIDTH = 1024
B_HEAD = 64
B_DECAY_RANK = 64
B_AAA_RANK = 64
B_GATE_RANK = 160
B_SMALL = 512
B_SMALL_K = 384
B_LN_EPS = 64e-5
C_WIDTH = 4096
C_HEADS = 8
C_HEAD = 512
C_QKV_BLOCK = 4
C_CONV = 4
D_FF = 5632
FFN_CONV = 3

SUBLANES = 8
LANES = 128
MXU_DIM = 256
VMEM_LIMIT = 56 * 1024 * 1024

RWKV_CHUNK = 64
RWKV_GROUP = 4 * B_HEAD
RWKV_ROWS = 512
MLSTM_CHUNK = 256
MLSTM_HEADS_PER_STEP = 8
PROJ_TM, PROJ_TN = 1024, 512
NORM_SUB_ROWS = 256
PROJ_TN_WIDE = 1024
FFN_TM, FFN_TF = 512, 512
RGLRU_GROUP = MXU_DIM
MLSTM_IN_COLS, MLSTM_IN_SUB = 512, 256


def _params(sem):
    return pltpu.CompilerParams(dimension_semantics=sem, vmem_limit_bytes=VMEM_LIMIT)


def _dot(a, b):
    return jnp.dot(a, b, preferred_element_type=F32)


def _dot_nt(a, b):
    return lax.dot_general(a, b, (((1,), (1,)), ((), ())), preferred_element_type=F32)


def _dot_tn(a, b):
    return lax.dot_general(a, b, (((0,), (0,)), ((), ())), preferred_element_type=F32)


def _split_bf16(x, terms):
    parts = []
    for _ in range(terms):
        p = x.astype(BF16)
        parts.append(p)
        x = x - p.astype(F32)
    return parts


def _sigmoid(x):
    return 1.0 / (1.0 + jnp.exp(-x))


def _silu(x):
    return x * _sigmoid(x)


def _rms(x, g):
    return x * lax.rsqrt(jnp.mean(x * x, axis=-1, keepdims=True) + EPS) * g


def _shift_rows(x, k, prev8):
    r = pltpu.roll(x, k, 0)
    fix = pltpu.roll(prev8, k, 0)
    row = lax.broadcasted_iota(jnp.int32, (SUBLANES, x.shape[1]), 0)
    head = jnp.where(row < k, fix, r[:SUBLANES])
    return jnp.concatenate([head, r[SUBLANES:]], axis=0)


def _norm_matmul_kernel(x_ref, g_ref, w_ref, o_ref, hn_ref, *, silu):
    j = pl.program_id(1)
    tm = x_ref.shape[0]

    def store(rows, y):
        o_ref[rows, :] = (_silu(y) if silu else y).astype(o_ref.dtype)

    @pl.when(j == 0)
    def _():
        for s in range(tm // NORM_SUB_ROWS):
            rows = pl.ds(s * NORM_SUB_ROWS, NORM_SUB_ROWS)
            hn = _rms(x_ref[rows, :], g_ref[...]).astype(BF16)
            hn_ref[rows, :] = hn
            store(rows, _dot(hn, w_ref[...]))

    @pl.when(j != 0)
    def _():
        store(pl.ds(0, tm), _dot(hn_ref[...], w_ref[...]))


def _norm_matmul(x, g, w, tm, tn, name, col_start=0, silu_bf16=False):
    n, d = x.shape
    nout = w.shape[1] - col_start
    off = col_start // tn
    assert col_start % tn == 0 and nout % tn == 0
    return pl.pallas_call(
        functools.partial(_norm_matmul_kernel, silu=silu_bf16),
        grid=(n // tm, nout // tn),
        in_specs=[pl.BlockSpec((tm, d), lambda i, j: (i, 0)),
                  pl.BlockSpec((1, d), lambda i, j: (0, 0)),
                  pl.BlockSpec((d, tn), lambda i, j: (0, j + off))],
        out_specs=pl.BlockSpec((tm, tn), lambda i, j: (i, j)),
        out_shape=jax.ShapeDtypeStruct((n, nout), BF16 if silu_bf16 else F32),
        scratch_shapes=[pltpu.VMEM((tm, d), BF16)],
        compiler_params=_params(("arbitrary", "arbitrary")),
        name=name,
    )(x, g, w)


def _resid_matmul_kernel(*refs, n_in):
    x_ref = refs[0]
    a_refs = refs[1:1 + n_in]
    w_refs = refs[1 + n_in:1 + 2 * n_in]
    o_ref = refs[1 + 2 * n_in]
    acc = x_ref[...]
    for a_ref, w_ref in zip(a_refs, w_refs):
        acc = acc + _dot(a_ref[...], w_ref[...])
    o_ref[...] = acc


def _resid_matmul(x, acts, ws, tm, tn, name):
    n, d = x.shape
    n_in = len(acts)
    in_specs = [pl.BlockSpec((tm, tn), lambda i, j: (i, j))]
    in_specs += [pl.BlockSpec((tm, a.shape[1]), lambda i, j: (i, 0)) for a in acts]
    in_specs += [pl.BlockSpec((w.shape[0], tn), lambda i, j: (0, j)) for w in ws]
    return pl.pallas_call(
        functools.partial(_resid_matmul_kernel, n_in=n_in),
        grid=(n // tm, d // tn),
        in_specs=in_specs,
        out_specs=pl.BlockSpec((tm, tn), lambda i, j: (i, j)),
        out_shape=jax.ShapeDtypeStruct((n, d), F32),
        compiler_params=_params(("arbitrary", "arbitrary")),
        name=name,
    )(x, *acts, *ws)


def _ffn_kernel(x_ref, g_ref, wg_ref, wu_ref, cw_ref, cb_ref, wd_ref, fg_ref, o_ref, hn_ref, carry_ref,
                *, tiles_per_seq, final_norm):
    i = pl.program_id(0)
    j = pl.program_id(1)
    tm = x_ref.shape[0]

    @pl.when(j == 0)
    def _():
        x = x_ref[...]
        hn_ref[...] = _rms(x, g_ref[...]).astype(BF16)
        o_ref[...] = x

    hn = hn_ref[...]
    gate = _dot(hn, wg_ref[...])
    up = _dot(hn, wu_ref[...])
    seq_start = (i % tiles_per_seq) == 0
    prev8 = jnp.where(seq_start, 0.0, carry_ref[j])
    carry_ref[j] = gate[tm - SUBLANES:]
    cw = cw_ref[...]
    conv = (cb_ref[...] + gate * cw[2:3] + _shift_rows(gate, 1, prev8) * cw[1:2]
            + _shift_rows(gate, 2, prev8) * cw[0:1])
    u = (_silu(conv) * up).astype(BF16)
    o_ref[...] += _dot(u, wd_ref[...])

    if final_norm:
        @pl.when(j == pl.num_programs(1) - 1)
        def _():
            o_ref[...] = _rms(o_ref[...], fg_ref[...])


def _ffn(x, g, wg, wu, cw, cb, wd, fg, layer, seq, tm, tf, final_norm, name):
    n, d = x.shape
    f = wg.shape[2]
    nf = f // tf
    kern = functools.partial(_ffn_kernel, tiles_per_seq=seq // tm, final_norm=final_norm)
    return pl.pallas_call(
        kern,
        grid=(n // tm, nf),
        in_specs=[pl.BlockSpec((tm, d), lambda i, j: (i, 0)),
                  pl.BlockSpec((1, d), lambda i, j: (0, 0)),
                  pl.BlockSpec((None, d, tf), lambda i, j: (layer, 0, j)),
                  pl.BlockSpec((None, d, tf), lambda i, j: (layer, 0, j)),
                  pl.BlockSpec((FFN_CONV, tf), lambda i, j: (0, j)),
                  pl.BlockSpec((1, tf), lambda i, j: (0, j)),
                  pl.BlockSpec((None, tf, d), lambda i, j: (layer, j, 0)),
                  pl.BlockSpec((1, d), lambda i, j: (0, 0))],
        out_specs=pl.BlockSpec((tm, d), lambda i, j: (i, 0)),
        out_shape=jax.ShapeDtypeStruct((n, d), F32),
        scratch_shapes=[pltpu.VMEM((tm, d), BF16), pltpu.VMEM((nf, SUBLANES, tf), F32)],
        compiler_params=_params(("arbitrary", "arbitrary")),
        name=name,
    )(x, g, wg, wu, cw, cb, wd, fg)


def _rglru_gates(x, prev8, cw, cb, wr, wi):
    xc = cb + x * cw[A_CONV - 1:A_CONV]
    for k in range(1, A_CONV):
        xc = xc + _shift_rows(x, k, prev8) * cw[A_CONV - 1 - k:A_CONV - k]
    xb = xc.astype(BF16)
    return xc, _dot(xb, wr), _dot(xb, wi)


def _rglru_scan(xc, r_pre, i_pre, ga, h0, br, bi, lam):
    rows, width = xc.shape
    r = _sigmoid(r_pre + br)
    ig = _sigmoid(i_pre + bi)
    neg_lam = -lam
    softplus = jnp.maximum(neg_lam, 0.0) + jnp.log1p(jnp.exp(-jnp.abs(neg_lam)))
    log_a = (-LRU_C) * r * softplus
    a = jnp.exp(log_a)
    u = jnp.sqrt(jnp.maximum(1.0 - a * a, 0.0)) * (ig * xc)

    n_sub = rows // SUBLANES
    a3 = a.reshape(n_sub, SUBLANES, width)
    u3 = u.reshape(n_sub, SUBLANES, width)
    sub = lax.broadcasted_iota(jnp.int32, (n_sub, SUBLANES, width), 1)
    s = 1
    while s < SUBLANES:
        keep = sub >= s
        a_prev = pltpu.roll(a3, s, 1)
        u_prev = pltpu.roll(u3, s, 1)
        u3 = jnp.where(keep, a3 * u_prev + u3, u3)
        a3 = jnp.where(keep, a3 * a_prev, a3)
        s *= 2
    carry = h0
    groups = []
    for gi in range(n_sub):
        hg = u3[gi] + a3[gi] * carry
        groups.append(hg)
        carry = hg[SUBLANES - 1:]
    h = jnp.concatenate(groups, axis=0)
    gelu = 0.5 * ga * (1.0 + jnp.tanh(math.sqrt(2.0 / math.pi) * (ga + 0.044715 * (ga * ga * ga))))
    return (h * gelu).astype(BF16), carry


def _even_in_kernel(x_ref, g_ref, w_ref, cw_ref, cb_ref, wr_ref, br_ref, wi_ref, bi_ref, lam_ref,
                    p_ref, ya_ref, hn_ref, act_ref, prev_ref, h_ref, *, tiles_per_seq):
    i = pl.program_id(0)
    j = pl.program_id(1)
    tm = x_ref.shape[0]
    tn = w_ref.shape[1]
    grp = RGLRU_GROUP
    n_units = A_WIDTH // grp
    n_act = 2 * A_WIDTH // tn
    per_tile = tn // grp

    def stash(rows, y, tile):
        for part in range(per_tile):
            act_ref[per_tile * tile + part, rows, :] = y[:, part * grp:(part + 1) * grp]

    @pl.when(j == 0)
    def _():
        for s in range(tm // NORM_SUB_ROWS):
            rows = pl.ds(s * NORM_SUB_ROWS, NORM_SUB_ROWS)
            hn = _rms(x_ref[rows, :], g_ref[...]).astype(BF16)
            hn_ref[rows, :] = hn
            stash(rows, _dot(hn, w_ref[...]), 0)

    @pl.when((j > 0) & (j < n_act))
    def _():
        stash(pl.ds(0, tm), _dot(hn_ref[...], w_ref[...]), j)

    @pl.when((j >= n_act) & (j < n_act + n_units))
    def _():
        unit = j - n_act
        seq_start = (i % tiles_per_seq) == 0
        prev8 = jnp.where(seq_start, 0.0, prev_ref[unit])
        h_last = jnp.where(seq_start, 0.0, h_ref[unit])
        gate_w = (cw_ref[unit], cb_ref[unit], wr_ref[unit], wi_ref[unit])
        scan_w = (br_ref[unit], bi_ref[unit], lam_ref[unit])
        sub = lambda s: pl.ds(s * NORM_SUB_ROWS, NORM_SUB_ROWS)
        for s in range(tm // NORM_SUB_ROWS):
            x = act_ref[unit, sub(s), :]
            gated = _rglru_gates(x, prev8, *gate_w)
            prev8 = x[NORM_SUB_ROWS - SUBLANES:]
            projected = _dot(hn_ref[sub(s), :], w_ref[...])
            y, h_last = _rglru_scan(*gated, act_ref[n_units + unit, sub(s), :], h_last, *scan_w)
            p_ref[sub(s), :] = projected
            ya_ref[sub(s), :] = y
        prev_ref[unit] = prev8
        h_ref[unit] = h_last

    @pl.when(j >= n_act + n_units)
    def _():
        p_ref[...] = _dot(hn_ref[...], w_ref[...])


def _even_in(x, g, w_all, cw, cb, wr, br, wi, bi, lam, seq, tm, tn):
    n, d = x.shape
    grp = RGLRU_GROUP
    n_units = A_WIDTH // grp
    n_tiles = w_all.shape[1] // tn
    n_act = 2 * A_WIDTH // tn
    assert (2 * A_WIDTH) % tn == 0 and tn % grp == 0 and n_tiles >= n_act + n_units
    group_vec = lambda v: v.reshape(n_units, 1, grp)
    whole = lambda a: pl.BlockSpec(a.shape, lambda i, j: (0,) * a.ndim)
    params = [jnp.transpose(cw.reshape(A_CONV, n_units, grp), (1, 0, 2)), group_vec(cb), wr, group_vec(br),
              wi, group_vec(bi), group_vec(lam)]
    return pl.pallas_call(
        functools.partial(_even_in_kernel, tiles_per_seq=seq // tm),
        grid=(n // tm, n_tiles),
        in_specs=[pl.BlockSpec((tm, d), lambda i, j: (i, 0)),
                  pl.BlockSpec((1, d), lambda i, j: (0, 0)),
                  pl.BlockSpec((d, tn), lambda i, j: (0, j))] + [whole(a) for a in params],
        out_specs=[pl.BlockSpec((tm, tn), lambda i, j: (i, jnp.maximum(j - n_act, 0))),
                   pl.BlockSpec((tm, grp), lambda i, j: (i, jnp.clip(j - n_act, 0, n_units - 1)))],
        out_shape=[jax.ShapeDtypeStruct((n, w_all.shape[1] - 2 * A_WIDTH), F32),
                   jax.ShapeDtypeStruct((n, A_WIDTH), BF16)],
        scratch_shapes=[pltpu.VMEM((tm, d), BF16), pltpu.VMEM((2 * n_units, tm, grp), F32),
                        pltpu.VMEM((n_units, SUBLANES, grp), F32), pltpu.VMEM((n_units, 1, grp), F32)],
        compiler_params=_params(("arbitrary", "arbitrary")),
        name="even_in",
    )(x, g, w_all, *params)


def _alternate(*stage_streams):
    results = [None] * len(stage_streams)
    live = list(range(len(stage_streams)))
    while live:
        for idx in list(live):
            try:
                next(stage_streams[idx])
            except StopIteration as stop:
                results[idx] = stop.value
                live.remove(idx)
    return results


def _rwkv_kernel(r_ref, k_ref, v_ref, sm_ref, mur_ref, muk_ref, muv_ref, mus_ref, w0_ref, wup_ref, a0_ref,
                 aup_ref, gup_ref, kkw_ref, kaw_ref, rkw_ref, lnw_ref, lnb_ref, o_ref,
                 s_ref, pr_ref, pk_ref, pv_ref, ps_ref):
    tt = r_ref.shape[0]
    L = RWKV_CHUNK
    gw = RWKV_GROUP
    n_grp = B_WIDTH // gw
    hr = tt // 2

    @pl.when(pl.program_id(1) == 0)
    def _():
        s_ref[...] = jnp.zeros_like(s_ref)
        pr_ref[...] = jnp.zeros_like(pr_ref)
        pk_ref[...] = jnp.zeros_like(pk_ref)
        pv_ref[...] = jnp.zeros_like(pv_ref)
        ps_ref[...] = jnp.zeros_like(ps_ref)

    def lerp(x_ref, p_ref, mu_ref):
        x = x_ref[...]
        xs = _shift_rows(x, 1, p_ref[...])
        p_ref[...] = x[tt - SUBLANES:]
        return x + (xs - x) * mu_ref[...]

    r_all = lerp(r_ref, pr_ref, mur_ref)
    k_all = lerp(k_ref, pk_ref, muk_ref)
    v_all = lerp(v_ref, pv_ref, muv_ref)
    sm_all = lerp(sm_ref, ps_ref, mus_ref)[:, :B_SMALL_K]

    row_g = lax.broadcasted_iota(jnp.int32, (gw, gw), 0)
    col_g = lax.broadcasted_iota(jnp.int32, (gw, gw), 1)
    same_head = (row_g // B_HEAD) == (col_g // B_HEAD)
    head_mask = jnp.where(same_head, 1.0, 0.0)
    head_mask_bf = head_mask.astype(BF16)

    def head_sum(x, terms):
        n = x.shape[0]
        xs = jnp.concatenate([x[:, gi * gw:(gi + 1) * gw] for gi in range(n_grp)], axis=0)
        s = _dot(jnp.concatenate(_split_bf16(xs, terms), axis=0), head_mask_bf)
        s = sum(s[t * n_grp * n:(t + 1) * n_grp * n] for t in range(terms))
        return jnp.concatenate([s[gi * n:(gi + 1) * n] for gi in range(n_grp)], axis=1)

    def bd(x):
        return jnp.concatenate([x.astype(BF16)] * (gw // B_HEAD), axis=0) * head_mask_bf

    row_t = lax.broadcasted_iota(jnp.int32, (hr, hr), 0)
    col_t = lax.broadcasted_iota(jnp.int32, (hr, hr), 1)
    tril = jnp.where((row_t >= col_t) & ((row_t // L) == (col_t // L)), 1.0, 0.0).astype(BF16)
    row_p = lax.broadcasted_iota(jnp.int32, (L, gw), 0)
    src_p = lax.broadcasted_iota(jnp.int32, (L, gw), 1) % B_HEAD
    strict_lower = row_p > src_p
    lower = row_p >= src_p
    n_doublings = int(math.log2(L)) - 1
    groups = range(n_grp)
    half_units = [(c, gi) for c in range(hr // L) for gi in groups]

    def tile(arr, c, gi):
        return arr[c * L:(c + 1) * L, gi * gw:(gi + 1) * gw]

    def prepare(h):
        rows = slice(h * hr, (h + 1) * hr)
        r, k, v, sm = r_all[rows], k_all[rows], v_all[rows], sm_all[rows]
        z = w0_ref[...] + _dot(jnp.tanh(sm).astype(BF16), wup_ref[...])
        log_w = (-math.exp(-0.5)) * _sigmoid(z)
        yield
        a = _sigmoid(a0_ref[...] + _dot(sm.astype(BF16), aup_ref[...]))
        yield
        g = _dot(_sigmoid(sm).astype(BF16), gup_ref[...])
        kk = k * kkw_ref[...]
        yield
        kk = kk * lax.rsqrt(jnp.maximum(head_sum(kk * kk, 2), 1e-12))
        yield
        k2 = k * (1.0 + (a - 1.0) * kaw_ref[...])
        yield
        cum = sum(_dot(tril, part) for part in _split_bf16(log_w, 3))
        yield
        p_in = jnp.exp(cum)
        p_inv = jnp.exp(-cum)
        yield
        a_bar = (-kk) * jnp.exp(cum - log_w)
        r_bar = r * p_in
        yield
        b_bar = kk * a * p_inv
        k_bar = k2 * p_inv
        return dict(r=r, v=v, g=g, k2=k2, p_in=p_in, a_bar=a_bar, r_bar=r_bar, b_bar=b_bar, k_bar=k_bar)

    def products(pre, states):
        units = half_units
        ar = {u: jnp.concatenate([tile(pre["a_bar"], *u), tile(pre["r_bar"], *u)], axis=0).astype(BF16)
              for u in units}
        m_b = {u: _dot_nt(ar[u], bd(tile(pre["b_bar"], *u))) for u in units}
        yield
        m_k = {u: _dot_nt(ar[u], bd(tile(pre["k_bar"], *u))) for u in units}
        yield
        x = {u: jnp.where(strict_lower, m_b[u][:L], 0.0) for u in units}
        a_rb = {u: jnp.where(lower, m_b[u][L:], 0.0).astype(BF16) for u in units}
        akrk = {u: jnp.concatenate([jnp.where(strict_lower, m_k[u][:L], 0.0),
                                    jnp.where(lower, m_k[u][L:], 0.0)], axis=0).astype(BF16) for u in units}
        cy = {u: _dot(akrk[u], bd(tile(pre["v"], *u))) for u in units}
        yield
        n_inv = dict(x)
        x_pow = {u: _dot(x[u].astype(BF16), bd(x[u])) for u in units}
        yield
        for step in range(n_doublings):
            if step + 1 < n_doublings:
                both = {u: _dot(jnp.concatenate([x_pow[u], n_inv[u]], axis=0).astype(BF16), bd(x_pow[u]))
                        for u in units}
                n_inv = {u: n_inv[u] + x_pow[u] + both[u][L:] for u in units}
                x_pow = {u: both[u][:L] for u in units}
            else:
                n_inv = {u: n_inv[u] + x_pow[u] + _dot(n_inv[u].astype(BF16), bd(x_pow[u])) for u in units}
            yield
        n_inv = {u: n_inv[u].astype(BF16) for u in units}
        y_rows = []
        for c in range(hr // L):
            p_last = pre["p_in"][(c + 1) * L - 1:(c + 1) * L]
            pl_g = [p_last[:, gi * gw:(gi + 1) * gw] for gi in groups]
            bk = [jnp.concatenate([tile(pre["b_bar"], c, gi) * pl_g[gi], tile(pre["k_bar"], c, gi) * pl_g[gi]],
                                  axis=0).astype(BF16) for gi in groups]
            ah = [_dot_nt(ar[c, gi], states[gi].astype(BF16)) for gi in groups]
            yield
            rhs = [ah[gi][:L] + cy[c, gi][:L] for gi in groups]
            u_c = [rhs[gi] + _dot(n_inv[c, gi], bd(rhs[gi])) for gi in groups]
            yield
            ds = [_dot_tn(jnp.concatenate([u_c[gi], tile(pre["v"], c, gi)], axis=0).astype(BF16), bk[gi])
                  for gi in groups]
            states = [states[gi] * pl_g[gi] + head_mask * ds[gi] for gi in groups]
            y_rows.append(jnp.concatenate(
                [ah[gi][L:] + cy[c, gi][L:] + _dot(a_rb[c, gi], bd(u_c[gi])) for gi in groups], axis=1))
            yield
        return jnp.concatenate(y_rows, axis=0), states

    def finish(h, pre, y):
        inv_n = 1.0 / B_HEAD
        yc = y - head_sum(y, 1) * inv_n
        yield
        var = head_sum(yc * yc, 1) * inv_n
        yield
        yn = yc * lax.rsqrt(var + B_LN_EPS) * lnw_ref[...] + lnb_ref[...]
        yield
        bonus = head_sum(pre["r"] * pre["k2"] * rkw_ref[...], 1) * pre["v"]
        yield
        o_ref[pl.ds(h * hr, hr), :] = ((yn + bonus) * pre["g"]).astype(o_ref.dtype)

    states = [s_ref[gi] for gi in groups]
    (pre0,) = _alternate(prepare(0))
    (y0, states), pre1 = _alternate(products(pre0, states), prepare(1))
    (y1, states), _ = _alternate(products(pre1, states), finish(0, pre0, y0))
    _alternate(finish(1, pre1, y1))
    for gi in groups:
        s_ref[gi] = states[gi]


def _rwkv(p, mur, muk, muv, mus, w0, wup, a0, aup, gup, kkw, kaw, rkw, lnw, lnb, bsz, seq):
    n = p.shape[0]
    tt = RWKV_ROWS
    nt = seq // tt
    w = B_WIDTH
    vec = lambda: pl.BlockSpec((1, w), lambda b, t: (0, 0))
    mat = lambda: pl.BlockSpec((B_SMALL_K, w), lambda b, t: (0, 0))
    return pl.pallas_call(
        _rwkv_kernel,
        grid=(bsz, nt),
        in_specs=[pl.BlockSpec((tt, w), lambda b, t: (b * nt + t, 0)),
                  pl.BlockSpec((tt, w), lambda b, t: (b * nt + t, 1)),
                  pl.BlockSpec((tt, w), lambda b, t: (b * nt + t, 2)),
                  pl.BlockSpec((tt, B_SMALL), lambda b, t: (b * nt + t, 3 * w // B_SMALL)),
                  vec(), vec(), vec(), pl.BlockSpec((1, B_SMALL), lambda b, t: (0, 0)),
                  vec(), mat(), vec(), mat(), mat(), vec(), vec(), vec(), vec(), vec()],
        out_specs=pl.BlockSpec((tt, w), lambda b, t: (b * nt + t, 0)),
        out_shape=jax.ShapeDtypeStruct((n, w), BF16),
        scratch_shapes=[pltpu.VMEM((w // RWKV_GROUP, RWKV_GROUP, RWKV_GROUP), F32),
                        pltpu.VMEM((SUBLANES, w), F32), pltpu.VMEM((SUBLANES, w), F32),
                        pltpu.VMEM((SUBLANES, w), F32), pltpu.VMEM((SUBLANES, B_SMALL), F32)],
        compiler_params=_params(("arbitrary", "arbitrary")),
        name="rwkv7",
    )(p, p, p, p, mur, muk, muv, mus, w0, wup, a0, aup, gup, kkw, kaw, rkw, lnw, lnb)


def _mlstm_in_kernel(x_ref, g_ref, w_ref, cw_ref, cb_ref, wq_ref, wk_ref, wv_ref, wif_ref, bif_ref,
                     q_ref, k_ref, v_ref, xc_ref, gates_ref, hn_ref, prev_ref, *, tiles_per_seq, sub_rows):
    i = pl.program_id(0)
    j = pl.program_id(1)
    tm = x_ref.shape[0]
    cb = w_ref.shape[1]
    n_grp = cb // MXU_DIM

    def blockdiag(xb, wb_ref):
        return jnp.concatenate(
            [_dot(xb[:, g * MXU_DIM:(g + 1) * MXU_DIM], wb_ref[g]) for g in range(n_grp)], axis=1)

    def body(first_col_tile):
        def project(s):
            rows = pl.ds(s * sub_rows, sub_rows)
            if first_col_tile:
                hn = _rms(x_ref[rows, :], g_ref[...]).astype(BF16)
                hn_ref[rows, :] = hn
            else:
                hn = hn_ref[rows, :]
            return _dot(hn, w_ref[...])

        seq_start = (i % tiles_per_seq) == 0
        prev8 = jnp.where(seq_start, 0.0, prev_ref[j])
        cw = cw_ref[...]
        n_sub = tm // sub_rows
        xm_next = project(0)
        for s in range(n_sub):
            rows = pl.ds(s * sub_rows, sub_rows)
            xm = xm_next
            if s + 1 < n_sub:
                xm_next = project(s + 1)
            conv = cb_ref[...] + xm * cw[C_CONV - 1:C_CONV]
            for kk in range(1, C_CONV):
                conv = conv + _shift_rows(xm, kk, prev8) * cw[C_CONV - 1 - kk:C_CONV - kk]
            prev8 = xm[sub_rows - SUBLANES:]
            xc = _silu(conv)
            xcb = xc.astype(BF16)
            xc_ref[rows, :] = xcb
            q = blockdiag(xcb, wq_ref)
            k = blockdiag(xcb, wk_ref)
            v = blockdiag(xm.astype(BF16), wv_ref)
            qb = q.astype(BF16)
            kb = k.astype(BF16)
            vb = v.astype(BF16)
            q_ref[rows, :] = qb
            k_ref[rows, :] = (k * (C_HEAD ** -0.5)).astype(BF16)
            v_ref[rows, :] = vb
            gate_part = _dot(qb, wif_ref[0]) + _dot(kb, wif_ref[1]) + _dot(vb, wif_ref[2])
            if first_col_tile:
                gates_ref[rows, :] = bif_ref[...] + gate_part
            else:
                gates_ref[rows, :] += gate_part
        prev_ref[j] = prev8

    pl.when(j == 0)(functools.partial(body, True))
    pl.when(j != 0)(functools.partial(body, False))


def _mlstm_in(x, g, w_in, cw, cb, wq, wk, wv, wif, bif, seq, tm, cblk, sub_rows):
    n, d = x.shape
    ncb = C_WIDTH // cblk
    gpb = cblk // MXU_DIM
    blk = lambda: pl.BlockSpec((tm, cblk), lambda i, j: (i, j))
    wspec = lambda: pl.BlockSpec((gpb, MXU_DIM, MXU_DIM), lambda i, j: (j, 0, 0))
    act = lambda dt: jax.ShapeDtypeStruct((n, C_WIDTH), dt)
    kern = functools.partial(_mlstm_in_kernel, tiles_per_seq=seq // tm, sub_rows=sub_rows)
    return pl.pallas_call(
        kern,
        grid=(n // tm, ncb),
        in_specs=[pl.BlockSpec((tm, d), lambda i, j: (i, 0)),
                  pl.BlockSpec((1, d), lambda i, j: (0, 0)),
                  pl.BlockSpec((d, cblk), lambda i, j: (0, j)),
                  pl.BlockSpec((C_CONV, cblk), lambda i, j: (0, j)),
                  pl.BlockSpec((1, cblk), lambda i, j: (0, j)),
                  wspec(), wspec(), wspec(),
                  pl.BlockSpec((3, cblk, 2 * C_HEADS), lambda i, j: (0, j, 0)),
                  pl.BlockSpec((1, 2 * C_HEADS), lambda i, j: (0, 0))],
        out_specs=[blk(), blk(), blk(), blk(),
                   pl.BlockSpec((tm, 2 * C_HEADS), lambda i, j: (i, 0))],
        out_shape=[act(BF16), act(BF16), act(BF16), act(BF16),
                   jax.ShapeDtypeStruct((n, 2 * C_HEADS), F32)],
        scratch_shapes=[pltpu.VMEM((tm, d), BF16), pltpu.VMEM((ncb, SUBLANES, cblk), F32)],
        compiler_params=_params(("arbitrary", "arbitrary")),
        name="mlstm_in",
    )(x, g, w_in, cw, cb, wq, wk, wv, wif, bif)


def _mlstm_gate_kernel(f_ref, b_ref):
    L = f_ref.shape[1]
    f = f_ref[...]
    lf = jnp.minimum(f, 0.0) - jnp.log1p(jnp.exp(-jnp.abs(f)))
    row = lax.broadcasted_iota(jnp.int32, (L, L), 0)
    col = lax.broadcasted_iota(jnp.int32, (L, L), 1)
    triu = jnp.where(row <= col, 1.0, 0.0).astype(BF16)
    b_ref[...] = sum(_dot(part, triu) for part in _split_bf16(lf, 3))


def _mlstm_gates(f_pre, chunk):
    rows, seq = f_pre.shape
    return pl.pallas_call(
        _mlstm_gate_kernel,
        grid=(seq // chunk,),
        in_specs=[pl.BlockSpec((rows, chunk), lambda c: (0, c))],
        out_specs=pl.BlockSpec((rows, chunk), lambda c: (0, c)),
        out_shape=jax.ShapeDtypeStruct((rows, seq), F32),
        compiler_params=_params(("arbitrary",)),
        name="mlstm_gates",
    )(f_pre)


def _mlstm_kernel(q_ref, k_ref, v_ref, xc_ref, za_ref, ir_ref, br_ref, lnw_ref, skip_ref,
                  o_ref, ct_ref, m_ref):
    c = pl.program_id(2)
    L = q_ref.shape[0]
    n_heads, hd = ct_ref.shape[0], ct_ref.shape[1]

    @pl.when(c == 0)
    def _():
        ct_ref[...] = jnp.zeros_like(ct_ref)
        m_ref[...] = jnp.zeros_like(m_ref)

    def lanes(x, width):
        return jnp.concatenate([x] * (width // LANES), axis=1)

    row = lax.broadcasted_iota(jnp.int32, (L, L), 0)
    col = lax.broadcasted_iota(jnp.int32, (L, L), 1)
    causal = row >= col
    eye = jnp.where(row == col, 1.0, 0.0).astype(BF16)

    def to_col(x_row):
        return sum(_dot_nt(eye, jnp.broadcast_to(part, (LANES, L))) for part in _split_bf16(x_row, 3))

    heads = range(n_heads)
    cols = [slice(i * hd, (i + 1) * hd) for i in heads]
    q = [q_ref[:, cols[i]] for i in heads]
    k = [k_ref[:, cols[i]] for i in heads]
    v_aug = [jnp.concatenate([v_ref[:, cols[i]], jnp.ones((L, LANES), BF16)], axis=1) for i in heads]
    li_row = [ir_ref[i] for i in heads]
    b_row = [br_ref[i] for i in heads]
    m_prev = [m_ref[i] for i in heads]

    qk = [_dot_nt(q[i], k[i]) for i in heads]
    q_ct = [_dot(q[i], ct_ref[i].astype(BF16)) for i in heads]
    b_col = [to_col(b_row[i]) for i in heads]
    li_col = [to_col(li_row[i]) for i in heads]
    b_last = [b_col[i][L - 1:L] for i in heads]

    d_log = [jnp.where(causal, lanes(b_col[i], L) - b_row[i] + li_row[i], -jnp.inf) for i in heads]
    inter = [b_col[i] + m_prev[i] for i in heads]
    m_t = [jnp.maximum(inter[i], jnp.max(d_log[i], axis=-1, keepdims=True)) for i in heads]
    s = [(qk[i] * jnp.exp(d_log[i] - lanes(m_t[i], L))).astype(BF16) for i in heads]
    sc = [jnp.exp(inter[i] - m_t[i]) for i in heads]
    num_den = [_dot(s[i], v_aug[i]) + lanes(sc[i], hd + LANES) * q_ct[i] for i in heads]

    g_log = [b_last[i] - b_col[i] + li_col[i] for i in heads]
    m_new = [jnp.maximum(b_last[i] + m_prev[i], jnp.max(g_log[i], axis=0, keepdims=True)) for i in heads]
    ke = [k[i] * lanes(jnp.exp(g_log[i] - m_new[i]).astype(BF16), hd) for i in heads]
    decay = [jnp.exp(b_last[i] + m_prev[i] - m_new[i]) for i in heads]
    for i in heads:
        ct_ref[i] = lanes(decay[i], hd + LANES) * ct_ref[i] + _dot_tn(ke[i], v_aug[i])
        m_ref[i] = m_new[i]

    for i in heads:
        inv = 1.0 / jnp.maximum(jnp.abs(num_den[i][:, hd:]), jnp.exp(-m_t[i]))
        h = num_den[i][:, :hd] * lanes(inv, hd)
        hc = h - jnp.mean(h, axis=-1, keepdims=True)
        hn = hc * lax.rsqrt(jnp.mean(hc * hc, axis=-1, keepdims=True) + EPS) * lnw_ref[:, cols[i]]
        hs = hn + skip_ref[:, cols[i]] * xc_ref[:, cols[i]].astype(F32)
        o_ref[:, cols[i]] = (hs * za_ref[:, cols[i]].astype(F32)).astype(o_ref.dtype)


def _mlstm(q, k, v, xc, z_act, i_row, b_row, lnw, skip, bsz, seq):
    n = q.shape[0]
    L = MLSTM_CHUNK
    nc = seq // L
    hd = C_HEAD
    hp = MLSTM_HEADS_PER_STEP
    blk = lambda: pl.BlockSpec((L, hp * hd), lambda b, h, c: (b * nc + c, h))
    rowspec = lambda: pl.BlockSpec((None, hp, 1, L), lambda b, h, c: (b, h, 0, c))
    vec = lambda: pl.BlockSpec((1, hp * hd), lambda b, h, c: (0, h))
    return pl.pallas_call(
        _mlstm_kernel,
        grid=(bsz, C_HEADS // hp, nc),
        in_specs=[blk(), blk(), blk(), blk(), blk(), rowspec(), rowspec(), vec(), vec()],
        out_specs=blk(),
        out_shape=jax.ShapeDtypeStruct((n, C_WIDTH), BF16),
        scratch_shapes=[pltpu.VMEM((hp, hd, hd + LANES), F32), pltpu.VMEM((hp, 1, LANES), F32)],
        compiler_params=_params(("arbitrary", "arbitrary", "arbitrary")),
        name="mlstm",
    )(q, k, v, xc, z_act, i_row, b_row, lnw, skip)


def _pack_block_diag(w, tile):
    g, bs, _ = w.shape
    per = tile // bs
    rows = jnp.tile(w.reshape(g // per, tile, bs), (1, 1, per))
    r = lax.broadcasted_iota(jnp.int32, (tile, tile), 0) // bs
    c = lax.broadcasted_iota(jnp.int32, (tile, tile), 1) // bs
    return jnp.where(r == c, rows, 0.0)


def _row(v):
    return v.reshape(1, -1)


def _even_layer(x, bsz, seq, norm, w_in, a_conv_w, a_conv_b, a_w_r, a_b_r, a_w_i, a_b_i, a_lambda,
                b_mu, b_w0, b_w_up, b_a0, b_a_up, b_g_up, b_k_k, b_k_a, b_r_k, b_ln_w, b_ln_b, w_out):
    main_w = 2 * A_WIDTH + 3 * B_WIDTH
    n_small = B_DECAY_RANK + B_AAA_RANK + B_GATE_RANK
    pad = B_SMALL - n_small
    assert main_w % B_SMALL == 0
    w_all = jnp.pad(w_in, ((0, 0), (0, pad))).astype(BF16)
    p, ya = _even_in(x, _row(norm), w_all, a_conv_w, a_conv_b,
                     _pack_block_diag(a_w_r, RGLRU_GROUP).astype(BF16), a_b_r,
                     _pack_block_diag(a_w_i, RGLRU_GROUP).astype(BF16), a_b_i, a_lambda, seq, PROJ_TM, PROJ_TN)

    mur, muk, muv = (_row(b_mu[i * B_WIDTH:(i + 1) * B_WIDTH]) for i in range(3))
    mus = _row(jnp.pad(b_mu[3 * B_WIDTH:], (0, pad)))

    def rows_at(w, start):
        out = jnp.zeros((B_SMALL_K, B_WIDTH), F32)
        return lax.dynamic_update_slice(out, w, (start, 0)).astype(BF16)

    wup = rows_at(b_w_up, 0)
    aup = rows_at(b_a_up, B_DECAY_RANK)
    gup = rows_at(b_g_up, B_DECAY_RANK + B_AAA_RANK)
    yb = _rwkv(p, mur, muk, muv, mus, _row(b_w0), wup, _row(b_a0), aup, gup, _row(b_k_k), _row(b_k_a),
               _row(b_r_k), _row(b_ln_w), _row(b_ln_b), bsz, seq)

    wo = w_out.astype(BF16)
    return _resid_matmul(x, [ya, yb], [wo[:A_WIDTH], wo[A_WIDTH:]], PROJ_TM, PROJ_TN_WIDE, "even_out")


def _odd_layer(x, bsz, seq, norm, w_in, conv_w, conv_b, w_q, w_k, w_v, w_if, b_if, ln_w, skip, w_out):
    w_in_b = w_in.astype(BF16)
    g = _row(norm)
    z_act = _norm_matmul(x, g, w_in_b, PROJ_TM, PROJ_TN_WIDE, "odd_in_z", col_start=C_WIDTH, silu_bf16=True)
    q, k, v, xc, gates = _mlstm_in(
        x, g, w_in_b, conv_w, _row(conv_b),
        _pack_block_diag(w_q, MXU_DIM).astype(BF16), _pack_block_diag(w_k, MXU_DIM).astype(BF16),
        _pack_block_diag(w_v, MXU_DIM).astype(BF16), w_if.astype(BF16), _row(b_if), seq,
        PROJ_TM, MLSTM_IN_COLS, MLSTM_IN_SUB)
    gt = jnp.transpose(gates.reshape(bsz, seq, 2, C_HEADS), (2, 0, 3, 1))
    i_pre = gt[0]
    b_cum = _mlstm_gates(gt[1].reshape(bsz * C_HEADS, seq), MLSTM_CHUNK).reshape(bsz, C_HEADS, seq)
    hs = _mlstm(q, k, v, xc, z_act, i_pre[:, :, None, :], b_cum[:, :, None, :], _row(ln_w), _row(skip),
                bsz, seq)
    return _resid_matmul(x, [hs], [w_out.astype(BF16)], PROJ_TM, PROJ_TN_WIDE, "odd_out")


def kernel(x, even_norm, even_w_in, a_conv_w, a_conv_b, a_w_r, a_b_r, a_w_i, a_b_i, a_lambda, b_mu, b_w0, b_w_up, b_a0, b_a_up, b_g_up, b_k_k, b_k_a, b_r_k, b_ln_w, b_ln_b, even_w_out, odd_norm, odd_w_in, c_conv_w, c_conv_b, c_w_q, c_w_k, c_w_v, c_w_if, c_b_if, c_ln_w, c_skip, odd_w_out, ffn_norm, ffn_w_gate, ffn_w_up, ffn_conv_w, ffn_conv_b, ffn_w_down, final_norm):
    bsz, seq, d = x.shape
    depth = ffn_norm.shape[0]
    h = x.reshape(bsz * seq, d)
    w_gate, w_up, w_down = ffn_w_gate.astype(BF16), ffn_w_up.astype(BF16), ffn_w_down.astype(BF16)
    for layer in range(depth):
        if layer % 2 == 0:
            e = layer // 2
            h = _even_layer(h, bsz, seq, even_norm[e], even_w_in[e], a_conv_w[e], a_conv_b[e], a_w_r[e],
                            a_b_r[e], a_w_i[e], a_b_i[e], a_lambda[e], b_mu[e], b_w0[e], b_w_up[e], b_a0[e],
                            b_a_up[e], b_g_up[e], b_k_k[e], b_k_a[e], b_r_k[e].reshape(-1), b_ln_w[e],
                            b_ln_b[e], even_w_out[e])
        else:
            o = layer // 2
            h = _odd_layer(h, bsz, seq, odd_norm[o], odd_w_in[o], c_conv_w[o], c_conv_b[o], c_w_q[o], c_w_k[o],
                           c_w_v[o], c_w_if[o], c_b_if[o], c_ln_w[o], c_skip[o], odd_w_out[o])
        h = _ffn(h, _row(ffn_norm[layer]), w_gate, w_up, ffn_conv_w[layer], _row(ffn_conv_b[layer]), w_down,
                 _row(final_norm), layer, seq, FFN_TM, FFN_TF, layer == depth - 1, "ffn%d" % layer)
    return h.reshape(bsz, seq, d)
```

```python
import functools
import math

import jax
import jax.numpy as jnp
from jax import lax
from jax.experimental import pallas as pl
from jax.experimental.pallas import tpu as pltpu

F32 = jnp.float32
BF16 = jnp.bfloat16

EPS = 1e-6
A_WIDTH = 1024
A_CONV = 4
LRU_C = 8.0
B_WIDTH = 1024
B_HEAD = 64
B_DECAY_RANK = 64
B_AAA_RANK = 64
B_GATE_RANK = 160
B_SMALL = 512
B_SMALL_K = 384
B_LN_EPS = 64e-5
B_KK_FLOOR = 1e-12
C_WIDTH = 4096
C_HEADS = 8
C_HEAD = 512
C_CONV = 4
FFN_CONV = 3

SUBLANES = 8
LANES = 128
MXU_DIM = 256
VMEM_LIMIT = 56 * 1024 * 1024

RWKV_CHUNK = 64
RWKV_GROUP = 4 * B_HEAD
RWKV_ROWS = 512
MLSTM_CHUNK = 256
MLSTM_HEADS_PER_STEP = 8
PROJ_TM, PROJ_TN = 1024, 512
NORM_SUB_ROWS = 256
PROJ_TN_WIDE = 1024
FFN_TM, FFN_TF = 512, 512
RGLRU_GROUP = MXU_DIM
MLSTM_IN_COLS, MLSTM_IN_SUB = 512, 256


def _params(sem):
    return pltpu.CompilerParams(dimension_semantics=sem, vmem_limit_bytes=VMEM_LIMIT)


def _dot(a, b):
    return jnp.dot(a, b, preferred_element_type=F32)


def _dot_nt(a, b):
    return lax.dot_general(a, b, (((1,), (1,)), ((), ())), preferred_element_type=F32)


def _dot_tn(a, b):
    return lax.dot_general(a, b, (((0,), (0,)), ((), ())), preferred_element_type=F32)


def _split_bf16(x, terms):
    parts = []
    for _ in range(terms):
        p = x.astype(BF16)
        parts.append(p)
        x = x - p.astype(F32)
    return parts


def _sigmoid(x):
    return 1.0 / (1.0 + jnp.exp(-x))


def _silu(x):
    return x * _sigmoid(x)


def _rms(x, g):
    return x * lax.rsqrt(jnp.mean(x * x, axis=-1, keepdims=True) + EPS) * g


def _shift_rows(x, k, prev8):
    r = pltpu.roll(x, k, 0)
    fix = pltpu.roll(prev8, k, 0)
    row = lax.broadcasted_iota(jnp.int32, (SUBLANES, x.shape[1]), 0)
    head = jnp.where(row < k, fix, r[:SUBLANES])
    return jnp.concatenate([head, r[SUBLANES:]], axis=0)


def _norm_matmul_kernel(x_ref, g_ref, w_ref, o_ref, hn_ref, *, silu):
    j = pl.program_id(1)
    tm = x_ref.shape[0]

    def store(rows, y):
        o_ref[rows, :] = (_silu(y) if silu else y).astype(o_ref.dtype)

    @pl.when(j == 0)
    def _():
        for s in range(tm // NORM_SUB_ROWS):
            rows = pl.ds(s * NORM_SUB_ROWS, NORM_SUB_ROWS)
            hn = _rms(x_ref[rows, :], g_ref[...]).astype(BF16)
            hn_ref[rows, :] = hn
            store(rows, _dot(hn, w_ref[...]))

    @pl.when(j != 0)
    def _():
        store(pl.ds(0, tm), _dot(hn_ref[...], w_ref[...]))


def _norm_matmul(x, g, w, tm, tn, name, col_start=0, silu_bf16=False):
    n, d = x.shape
    nout = w.shape[1] - col_start
    off = col_start // tn
    assert col_start % tn == 0 and nout % tn == 0
    return pl.pallas_call(
        functools.partial(_norm_matmul_kernel, silu=silu_bf16),
        grid=(n // tm, nout // tn),
        in_specs=[pl.BlockSpec((tm, d), lambda i, j: (i, 0)),
                  pl.BlockSpec((1, d), lambda i, j: (0, 0)),
                  pl.BlockSpec((d, tn), lambda i, j: (0, j + off))],
        out_specs=pl.BlockSpec((tm, tn), lambda i, j: (i, j)),
        out_shape=jax.ShapeDtypeStruct((n, nout), BF16 if silu_bf16 else F32),
        scratch_shapes=[pltpu.VMEM((tm, d), BF16)],
        compiler_params=_params(("arbitrary", "arbitrary")),
        name=name,
    )(x, g, w)


def _resid_matmul_kernel(*refs, n_in):
    x_ref = refs[0]
    a_refs = refs[1:1 + n_in]
    w_refs = refs[1 + n_in:1 + 2 * n_in]
    o_ref = refs[1 + 2 * n_in]
    acc = x_ref[...]
    for a_ref, w_ref in zip(a_refs, w_refs):
        acc = acc + _dot(a_ref[...], w_ref[...])
    o_ref[...] = acc


def _resid_matmul(x, acts, ws, tm, tn, name):
    n, d = x.shape
    n_in = len(acts)
    in_specs = [pl.BlockSpec((tm, tn), lambda i, j: (i, j))]
    in_specs += [pl.BlockSpec((tm, a.shape[1]), lambda i, j: (i, 0)) for a in acts]
    in_specs += [pl.BlockSpec((w.shape[0], tn), lambda i, j: (0, j)) for w in ws]
    return pl.pallas_call(
        functools.partial(_resid_matmul_kernel, n_in=n_in),
        grid=(n // tm, d // tn),
        in_specs=in_specs,
        out_specs=pl.BlockSpec((tm, tn), lambda i, j: (i, j)),
        out_shape=jax.ShapeDtypeStruct((n, d), F32),
        compiler_params=_params(("arbitrary", "arbitrary")),
        name=name,
    )(x, *acts, *ws)


def _ffn_kernel(x_ref, g_ref, wg_ref, wu_ref, cw_ref, cb_ref, wd_ref, fg_ref, o_ref, hn_ref, carry_ref,
                *, tiles_per_seq, final_norm):
    i = pl.program_id(0)
    j = pl.program_id(1)
    tm = x_ref.shape[0]

    @pl.when(j == 0)
    def _():
        x = x_ref[...]
        hn_ref[...] = _rms(x, g_ref[...]).astype(BF16)
        o_ref[...] = x

    hn = hn_ref[...]
    gate = _dot(hn, wg_ref[...])
    up = _dot(hn, wu_ref[...])
    seq_start = (i % tiles_per_seq) == 0
    prev8 = jnp.where(seq_start, 0.0, carry_ref[j])
    carry_ref[j] = gate[tm - SUBLANES:]
    cw = cw_ref[...]
    conv = (cb_ref[...] + gate * cw[2:3] + _shift_rows(gate, 1, prev8) * cw[1:2]
            + _shift_rows(gate, 2, prev8) * cw[0:1])
    u = (_silu(conv) * up).astype(BF16)
    o_ref[...] += _dot(u, wd_ref[...])

    if final_norm:
        @pl.when(j == pl.num_programs(1) - 1)
        def _():
            o_ref[...] = _rms(o_ref[...], fg_ref[...])


def _ffn(x, g, wg, wu, cw, cb, wd, fg, layer, seq, tm, tf, final_norm, name):
    n, d = x.shape
    f = wg.shape[2]
    nf = f // tf
    kern = functools.partial(_ffn_kernel, tiles_per_seq=seq // tm, final_norm=final_norm)
    return pl.pallas_call(
        kern,
        grid=(n // tm, nf),
        in_specs=[pl.BlockSpec((tm, d), lambda i, j: (i, 0)),
                  pl.BlockSpec((1, d), lambda i, j: (0, 0)),
                  pl.BlockSpec((None, d, tf), lambda i, j: (layer, 0, j)),
                  pl.BlockSpec((None, d, tf), lambda i, j: (layer, 0, j)),
                  pl.BlockSpec((FFN_CONV, tf), lambda i, j: (0, j)),
                  pl.BlockSpec((1, tf), lambda i, j: (0, j)),
                  pl.BlockSpec((None, tf, d), lambda i, j: (layer, j, 0)),
                  pl.BlockSpec((1, d), lambda i, j: (0, 0))],
        out_specs=pl.BlockSpec((tm, d), lambda i, j: (i, 0)),
        out_shape=jax.ShapeDtypeStruct((n, d), F32),
        scratch_shapes=[pltpu.VMEM((tm, d), BF16), pltpu.VMEM((nf, SUBLANES, tf), F32)],
        compiler_params=_params(("arbitrary", "arbitrary")),
        name=name,
    )(x, g, wg, wu, cw, cb, wd, fg)


def _rglru_gates(x, prev8, cw, cb, wr, wi):
    xc = cb + x * cw[A_CONV - 1:A_CONV]
    for k in range(1, A_CONV):
        xc = xc + _shift_rows(x, k, prev8) * cw[A_CONV - 1 - k:A_CONV - k]
    xb = xc.astype(BF16)
    return xc, _dot(xb, wr), _dot(xb, wi)


def _rglru_scan(xc, r_pre, i_pre, ga, h0, br, bi, lam):
    rows, width = xc.shape
    r = _sigmoid(r_pre + br)
    ig = _sigmoid(i_pre + bi)
    neg_lam = -lam
    softplus = jnp.maximum(neg_lam, 0.0) + jnp.log1p(jnp.exp(-jnp.abs(neg_lam)))
    log_a = (-LRU_C) * r * softplus
    a = jnp.exp(log_a)
    u = jnp.sqrt(jnp.maximum(1.0 - a * a, 0.0)) * (ig * xc)

    n_sub = rows // SUBLANES
    a3 = a.reshape(n_sub, SUBLANES, width)
    u3 = u.reshape(n_sub, SUBLANES, width)
    sub = lax.broadcasted_iota(jnp.int32, (n_sub, SUBLANES, width), 1)
    s = 1
    while s < SUBLANES:
        keep = sub >= s
        a_prev = pltpu.roll(a3, s, 1)
        u_prev = pltpu.roll(u3, s, 1)
        u3 = jnp.where(keep, a3 * u_prev + u3, u3)
        a3 = jnp.where(keep, a3 * a_prev, a3)
        s *= 2
    carry = h0
    groups = []
    for gi in range(n_sub):
        hg = u3[gi] + a3[gi] * carry
        groups.append(hg)
        carry = hg[SUBLANES - 1:]
    h = jnp.concatenate(groups, axis=0)
    gelu = 0.5 * ga * (1.0 + jnp.tanh(math.sqrt(2.0 / math.pi) * (ga + 0.044715 * (ga * ga * ga))))
    return (h * gelu).astype(BF16), carry


def _even_in_kernel(x_ref, g_ref, w_ref, cw_ref, cb_ref, wr_ref, br_ref, wi_ref, bi_ref, lam_ref,
                    p_ref, ya_ref, hn_ref, act_ref, prev_ref, h_ref, *, tiles_per_seq):
    i = pl.program_id(0)
    j = pl.program_id(1)
    tm = x_ref.shape[0]
    tn = w_ref.shape[1]
    grp = RGLRU_GROUP
    n_units = A_WIDTH // grp
    n_act = 2 * A_WIDTH // tn
    per_tile = tn // grp

    def stash(rows, y, tile):
        for part in range(per_tile):
            act_ref[per_tile * tile + part, rows, :] = y[:, part * grp:(part + 1) * grp]

    @pl.when(j == 0)
    def _():
        for s in range(tm // NORM_SUB_ROWS):
            rows = pl.ds(s * NORM_SUB_ROWS, NORM_SUB_ROWS)
            hn = _rms(x_ref[rows, :], g_ref[...]).astype(BF16)
            hn_ref[rows, :] = hn
            stash(rows, _dot(hn, w_ref[...]), 0)

    @pl.when((j > 0) & (j < n_act))
    def _():
        stash(pl.ds(0, tm), _dot(hn_ref[...], w_ref[...]), j)

    @pl.when((j >= n_act) & (j < n_act + n_units))
    def _():
        unit = j - n_act
        seq_start = (i % tiles_per_seq) == 0
        prev8 = jnp.where(seq_start, 0.0, prev_ref[unit])
        h_last = jnp.where(seq_start, 0.0, h_ref[unit])
        gate_w = (cw_ref[unit], cb_ref[unit], wr_ref[unit], wi_ref[unit])
        scan_w = (br_ref[unit], bi_ref[unit], lam_ref[unit])
        sub = lambda s: pl.ds(s * NORM_SUB_ROWS, NORM_SUB_ROWS)
        for s in range(tm // NORM_SUB_ROWS):
            x = act_ref[unit, sub(s), :]
            gated = _rglru_gates(x, prev8, *gate_w)
            prev8 = x[NORM_SUB_ROWS - SUBLANES:]
            projected = _dot(hn_ref[sub(s), :], w_ref[...])
            y, h_last = _rglru_scan(*gated, act_ref[n_units + unit, sub(s), :], h_last, *scan_w)
            p_ref[sub(s), :] = projected
            ya_ref[sub(s), :] = y
        prev_ref[unit] = prev8
        h_ref[unit] = h_last

    @pl.when(j >= n_act + n_units)
    def _():
        p_ref[...] = _dot(hn_ref[...], w_ref[...])


def _even_in(x, g, w_all, cw, cb, wr, br, wi, bi, lam, seq, tm, tn):
    n, d = x.shape
    grp = RGLRU_GROUP
    n_units = A_WIDTH // grp
    n_tiles = w_all.shape[1] // tn
    n_act = 2 * A_WIDTH // tn
    assert (2 * A_WIDTH) % tn == 0 and tn % grp == 0 and n_tiles >= n_act + n_units
    group_vec = lambda v: v.reshape(n_units, 1, grp)
    whole = lambda a: pl.BlockSpec(a.shape, lambda i, j: (0,) * a.ndim)
    params = [jnp.transpose(cw.reshape(A_CONV, n_units, grp), (1, 0, 2)), group_vec(cb), wr, group_vec(br),
              wi, group_vec(bi), group_vec(lam)]
    return pl.pallas_call(
        functools.partial(_even_in_kernel, tiles_per_seq=seq // tm),
        grid=(n // tm, n_tiles),
        in_specs=[pl.BlockSpec((tm, d), lambda i, j: (i, 0)),
                  pl.BlockSpec((1, d), lambda i, j: (0, 0)),
                  pl.BlockSpec((d, tn), lambda i, j: (0, j))] + [whole(a) for a in params],
        out_specs=[pl.BlockSpec((tm, tn), lambda i, j: (i, jnp.maximum(j - n_act, 0))),
                   pl.BlockSpec((tm, grp), lambda i, j: (i, jnp.clip(j - n_act, 0, n_units - 1)))],
        out_shape=[jax.ShapeDtypeStruct((n, w_all.shape[1] - 2 * A_WIDTH), F32),
                   jax.ShapeDtypeStruct((n, A_WIDTH), BF16)],
        scratch_shapes=[pltpu.VMEM((tm, d), BF16), pltpu.VMEM((2 * n_units, tm, grp), F32),
                        pltpu.VMEM((n_units, SUBLANES, grp), F32), pltpu.VMEM((n_units, 1, grp), F32)],
        compiler_params=_params(("arbitrary", "arbitrary")),
        name="even_in",
    )(x, g, w_all, *params)


def _alternate(*stage_streams):
    results = [None] * len(stage_streams)
    live = list(range(len(stage_streams)))
    while live:
        for idx in list(live):
            try:
                next(stage_streams[idx])
            except StopIteration as stop:
                results[idx] = stop.value
                live.remove(idx)
    return results


def _rwkv_kernel(r_ref, k_ref, v_ref, sm_ref, mur_ref, muk_ref, muv_ref, mus_ref, w0_ref, wup_ref, a0_ref,
                 aup_ref, gup_ref, kkw_ref, kaw_ref, rkw_ref, lnw_ref, lnb_ref, o_ref,
                 s_ref, pr_ref, pk_ref, pv_ref, ps_ref):
    tt = r_ref.shape[0]
    L = RWKV_CHUNK
    gw = RWKV_GROUP
    n_grp = B_WIDTH // gw
    hr = tt // 2

    @pl.when(pl.program_id(1) == 0)
    def _():
        s_ref[...] = jnp.zeros_like(s_ref)
        pr_ref[...] = jnp.zeros_like(pr_ref)
        pk_ref[...] = jnp.zeros_like(pk_ref)
        pv_ref[...] = jnp.zeros_like(pv_ref)
        ps_ref[...] = jnp.zeros_like(ps_ref)

    def lerp(x_ref, p_ref, mu_ref):
        x = x_ref[...]
        xs = _shift_rows(x, 1, p_ref[...])
        p_ref[...] = x[tt - SUBLANES:]
        return x + (xs - x) * mu_ref[...]

    r_all = lerp(r_ref, pr_ref, mur_ref)
    k_all = lerp(k_ref, pk_ref, muk_ref)
    v_all = lerp(v_ref, pv_ref, muv_ref)
    sm_all = lerp(sm_ref, ps_ref, mus_ref)[:, :B_SMALL_K]

    row_g = lax.broadcasted_iota(jnp.int32, (gw, gw), 0)
    col_g = lax.broadcasted_iota(jnp.int32, (gw, gw), 1)
    same_head = (row_g // B_HEAD) == (col_g // B_HEAD)
    head_mask = jnp.where(same_head, 1.0, 0.0)
    head_mask_bf = head_mask.astype(BF16)

    def head_sum(x, terms):
        n = x.shape[0]
        xs = jnp.concatenate([x[:, gi * gw:(gi + 1) * gw] for gi in range(n_grp)], axis=0)
        s = _dot(jnp.concatenate(_split_bf16(xs, terms), axis=0), head_mask_bf)
        s = sum(s[t * n_grp * n:(t + 1) * n_grp * n] for t in range(terms))
        return jnp.concatenate([s[gi * n:(gi + 1) * n] for gi in range(n_grp)], axis=1)

    def bd(x):
        return jnp.concatenate([x.astype(BF16)] * (gw // B_HEAD), axis=0) * head_mask_bf

    row_t = lax.broadcasted_iota(jnp.int32, (hr, hr), 0)
    col_t = lax.broadcasted_iota(jnp.int32, (hr, hr), 1)
    tril = jnp.where((row_t >= col_t) & ((row_t // L) == (col_t // L)), 1.0, 0.0).astype(BF16)
    row_p = lax.broadcasted_iota(jnp.int32, (L, gw), 0)
    src_p = lax.broadcasted_iota(jnp.int32, (L, gw), 1) % B_HEAD
    strict_lower = row_p > src_p
    lower = row_p >= src_p
    n_doublings = int(math.log2(L)) - 1
    groups = range(n_grp)
    half_units = [(c, gi) for c in range(hr // L) for gi in groups]

    def tile(arr, c, gi):
        return arr[c * L:(c + 1) * L, gi * gw:(gi + 1) * gw]

    def prepare(h):
        rows = slice(h * hr, (h + 1) * hr)
        r, k, v, sm = r_all[rows], k_all[rows], v_all[rows], sm_all[rows]
        z = w0_ref[...] + _dot(jnp.tanh(sm).astype(BF16), wup_ref[...])
        log_w = (-math.exp(-0.5)) * _sigmoid(z)
        yield
        a = _sigmoid(a0_ref[...] + _dot(sm.astype(BF16), aup_ref[...]))
        yield
        g = _dot(_sigmoid(sm).astype(BF16), gup_ref[...])
        kk = k * kkw_ref[...]
        yield
        kk = kk * lax.rsqrt(jnp.maximum(head_sum(kk * kk, 2), B_KK_FLOOR))
        yield
        k2 = k * (1.0 + (a - 1.0) * kaw_ref[...])
        yield
        cum = sum(_dot(tril, part) for part in _split_bf16(log_w, 3))
        yield
        p_in = jnp.exp(cum)
        p_inv = jnp.exp(-cum)
        yield
        a_bar = (-kk) * jnp.exp(cum - log_w)
        r_bar = r * p_in
        yield
        b_bar = kk * a * p_inv
        k_bar = k2 * p_inv
        return dict(r=r, v=v, g=g, k2=k2, p_in=p_in, a_bar=a_bar, r_bar=r_bar, b_bar=b_bar, k_bar=k_bar)

    def products(pre, states):
        units = half_units
        ar = {u: jnp.concatenate([tile(pre["a_bar"], *u), tile(pre["r_bar"], *u)], axis=0).astype(BF16)
              for u in units}
        m_b = {u: _dot_nt(ar[u], bd(tile(pre["b_bar"], *u))) for u in units}
        yield
        m_k = {u: _dot_nt(ar[u], bd(tile(pre["k_bar"], *u))) for u in units}
        yield
        x = {u: jnp.where(strict_lower, m_b[u][:L], 0.0) for u in units}
        a_rb = {u: jnp.where(lower, m_b[u][L:], 0.0).astype(BF16) for u in units}
        akrk = {u: jnp.concatenate([jnp.where(strict_lower, m_k[u][:L], 0.0),
                                    jnp.where(lower, m_k[u][L:], 0.0)], axis=0).astype(BF16) for u in units}
        cy = {u: _dot(akrk[u], bd(tile(pre["v"], *u))) for u in units}
        yield
        n_inv = dict(x)
        x_pow = {u: _dot(x[u].astype(BF16), bd(x[u])) for u in units}
        yield
        for step in range(n_doublings):
            if step + 1 < n_doublings:
                both = {u: _dot(jnp.concatenate([x_pow[u], n_inv[u]], axis=0).astype(BF16), bd(x_pow[u]))
                        for u in units}
                n_inv = {u: n_inv[u] + x_pow[u] + both[u][L:] for u in units}
                x_pow = {u: both[u][:L] for u in units}
            else:
                n_inv = {u: n_inv[u] + x_pow[u] + _dot(n_inv[u].astype(BF16), bd(x_pow[u])) for u in units}
            yield
        n_inv = {u: n_inv[u].astype(BF16) for u in units}
        y_rows = []
        for c in range(hr // L):
            p_last = pre["p_in"][(c + 1) * L - 1:(c + 1) * L]
            pl_g = [p_last[:, gi * gw:(gi + 1) * gw] for gi in groups]
            bk = [jnp.concatenate([tile(pre["b_bar"], c, gi) * pl_g[gi], tile(pre["k_bar"], c, gi) * pl_g[gi]],
                                  axis=0).astype(BF16) for gi in groups]
            ah = [_dot_nt(ar[c, gi], states[gi].astype(BF16)) for gi in groups]
            yield
            rhs = [ah[gi][:L] + cy[c, gi][:L] for gi in groups]
            u_c = [rhs[gi] + _dot(n_inv[c, gi], bd(rhs[gi])) for gi in groups]
            yield
            ds = [_dot_tn(jnp.concatenate([u_c[gi], tile(pre["v"], c, gi)], axis=0).astype(BF16), bk[gi])
                  for gi in groups]
            states = [states[gi] * pl_g[gi] + head_mask * ds[gi] for gi in groups]
            y_rows.append(jnp.concatenate(
                [ah[gi][L:] + cy[c, gi][L:] + _dot(a_rb[c, gi], bd(u_c[gi])) for gi in groups], axis=1))
            yield
        return jnp.concatenate(y_rows, axis=0), states

    def finish(h, pre, y):
        inv_n = 1.0 / B_HEAD
        yc = y - head_sum(y, 1) * inv_n
        yield
        var = head_sum(yc * yc, 1) * inv_n
        yield
        yn = yc * lax.rsqrt(var + B_LN_EPS) * lnw_ref[...] + lnb_ref[...]
        yield
        bonus = head_sum(pre["r"] * pre["k2"] * rkw_ref[...], 1) * pre["v"]
        yield
        o_ref[pl.ds(h * hr, hr), :] = ((yn + bonus) * pre["g"]).astype(o_ref.dtype)

    states = [s_ref[gi] for gi in groups]
    (pre0,) = _alternate(prepare(0))
    (y0, states), pre1 = _alternate(products(pre0, states), prepare(1))
    (y1, states), _ = _alternate(products(pre1, states), finish(0, pre0, y0))
    _alternate(finish(1, pre1, y1))
    for gi in groups:
        s_ref[gi] = states[gi]


def _rwkv(p, mur, muk, muv, mus, w0, wup, a0, aup, gup, kkw, kaw, rkw, lnw, lnb, bsz, seq):
    n = p.shape[0]
    tt = RWKV_ROWS
    nt = seq // tt
    w = B_WIDTH
    vec = lambda: pl.BlockSpec((1, w), lambda b, t: (0, 0))
    mat = lambda: pl.BlockSpec((B_SMALL_K, w), lambda b, t: (0, 0))
    return pl.pallas_call(
        _rwkv_kernel,
        grid=(bsz, nt),
        in_specs=[pl.BlockSpec((tt, w), lambda b, t: (b * nt + t, 0)),
                  pl.BlockSpec((tt, w), lambda b, t: (b * nt + t, 1)),
                  pl.BlockSpec((tt, w), lambda b, t: (b * nt + t, 2)),
                  pl.BlockSpec((tt, B_SMALL), lambda b, t: (b * nt + t, 3 * w // B_SMALL)),
                  vec(), vec(), vec(), pl.BlockSpec((1, B_SMALL), lambda b, t: (0, 0)),
                  vec(), mat(), vec(), mat(), mat(), vec(), vec(), vec(), vec(), vec()],
        out_specs=pl.BlockSpec((tt, w), lambda b, t: (b * nt + t, 0)),
        out_shape=jax.ShapeDtypeStruct((n, w), BF16),
        scratch_shapes=[pltpu.VMEM((w // RWKV_GROUP, RWKV_GROUP, RWKV_GROUP), F32),
                        pltpu.VMEM((SUBLANES, w), F32), pltpu.VMEM((SUBLANES, w), F32),
                        pltpu.VMEM((SUBLANES, w), F32), pltpu.VMEM((SUBLANES, B_SMALL), F32)],
        compiler_params=_params(("arbitrary", "arbitrary")),
        name="rwkv7",
    )(p, p, p, p, mur, muk, muv, mus, w0, wup, a0, aup, gup, kkw, kaw, rkw, lnw, lnb)


def _mlstm_in_kernel(x_ref, g_ref, w_ref, cw_ref, cb_ref, wq_ref, wk_ref, wv_ref, wif_ref, bif_ref,
                     q_ref, k_ref, v_ref, xc_ref, gates_ref, hn_ref, prev_ref, *, tiles_per_seq, sub_rows):
    i = pl.program_id(0)
    j = pl.program_id(1)
    tm = x_ref.shape[0]
    cb = w_ref.shape[1]
    n_grp = cb // MXU_DIM

    def blockdiag(xb, wb_ref):
        return jnp.concatenate(
            [_dot(xb[:, g * MXU_DIM:(g + 1) * MXU_DIM], wb_ref[g]) for g in range(n_grp)], axis=1)

    def body(first_col_tile):
        def project(s):
            rows = pl.ds(s * sub_rows, sub_rows)
            if first_col_tile:
                hn = _rms(x_ref[rows, :], g_ref[...]).astype(BF16)
                hn_ref[rows, :] = hn
            else:
                hn = hn_ref[rows, :]
            return _dot(hn, w_ref[...])

        seq_start = (i % tiles_per_seq) == 0
        prev8 = jnp.where(seq_start, 0.0, prev_ref[j])
        cw = cw_ref[...]
        n_sub = tm // sub_rows
        xm_next = project(0)
        for s in range(n_sub):
            rows = pl.ds(s * sub_rows, sub_rows)
            xm = xm_next
            if s + 1 < n_sub:
                xm_next = project(s + 1)
            conv = cb_ref[...] + xm * cw[C_CONV - 1:C_CONV]
            for kk in range(1, C_CONV):
                conv = conv + _shift_rows(xm, kk, prev8) * cw[C_CONV - 1 - kk:C_CONV - kk]
            prev8 = xm[sub_rows - SUBLANES:]
            xc = _silu(conv)
            xcb = xc.astype(BF16)
            xc_ref[rows, :] = xcb
            q = blockdiag(xcb, wq_ref)
            k = blockdiag(xcb, wk_ref)
            v = blockdiag(xm.astype(BF16), wv_ref)
            qb = q.astype(BF16)
            kb = k.astype(BF16)
            vb = v.astype(BF16)
            q_ref[rows, :] = qb
            k_ref[rows, :] = (k * (C_HEAD ** -0.5)).astype(BF16)
            v_ref[rows, :] = vb
            gate_part = _dot(qb, wif_ref[0]) + _dot(kb, wif_ref[1]) + _dot(vb, wif_ref[2])
            if first_col_tile:
                gates_ref[rows, :] = bif_ref[...] + gate_part
            else:
                gates_ref[rows, :] += gate_part
        prev_ref[j] = prev8

    pl.when(j == 0)(functools.partial(body, True))
    pl.when(j != 0)(functools.partial(body, False))


def _mlstm_in(x, g, w_in, cw, cb, wq, wk, wv, wif, bif, seq, tm, cblk, sub_rows):
    n, d = x.shape
    ncb = C_WIDTH // cblk
    gpb = cblk // MXU_DIM
    blk = lambda: pl.BlockSpec((tm, cblk), lambda i, j: (i, j))
    wspec = lambda: pl.BlockSpec((gpb, MXU_DIM, MXU_DIM), lambda i, j: (j, 0, 0))
    act = lambda dt: jax.ShapeDtypeStruct((n, C_WIDTH), dt)
    kern = functools.partial(_mlstm_in_kernel, tiles_per_seq=seq // tm, sub_rows=sub_rows)
    return pl.pallas_call(
        kern,
        grid=(n // tm, ncb),
        in_specs=[pl.BlockSpec((tm, d), lambda i, j: (i, 0)),
                  pl.BlockSpec((1, d), lambda i, j: (0, 0)),
                  pl.BlockSpec((d, cblk), lambda i, j: (0, j)),
                  pl.BlockSpec((C_CONV, cblk), lambda i, j: (0, j)),
                  pl.BlockSpec((1, cblk), lambda i, j: (0, j)),
                  wspec(), wspec(), wspec(),
                  pl.BlockSpec((3, cblk, 2 * C_HEADS), lambda i, j: (0, j, 0)),
                  pl.BlockSpec((1, 2 * C_HEADS), lambda i, j: (0, 0))],
        out_specs=[blk(), blk(), blk(), blk(),
                   pl.BlockSpec((tm, 2 * C_HEADS), lambda i, j: (i, 0))],
        out_shape=[act(BF16), act(BF16), act(BF16), act(BF16),
                   jax.ShapeDtypeStruct((n, 2 * C_HEADS), F32)],
        scratch_shapes=[pltpu.VMEM((tm, d), BF16), pltpu.VMEM((ncb, SUBLANES, cblk), F32)],
        compiler_params=_params(("arbitrary", "arbitrary")),
        name="mlstm_in",
    )(x, g, w_in, cw, cb, wq, wk, wv, wif, bif)


def _mlstm_gate_kernel(f_ref, b_ref):
    L = f_ref.shape[1]
    f = f_ref[...]
    lf = jnp.minimum(f, 0.0) - jnp.log1p(jnp.exp(-jnp.abs(f)))
    row = lax.broadcasted_iota(jnp.int32, (L, L), 0)
    col = lax.broadcasted_iota(jnp.int32, (L, L), 1)
    triu = jnp.where(row <= col, 1.0, 0.0).astype(BF16)
    b_ref[...] = sum(_dot(part, triu) for part in _split_bf16(lf, 3))


def _mlstm_gates(f_pre, chunk):
    rows, seq = f_pre.shape
    return pl.pallas_call(
        _mlstm_gate_kernel,
        grid=(seq // chunk,),
        in_specs=[pl.BlockSpec((rows, chunk), lambda c: (0, c))],
        out_specs=pl.BlockSpec((rows, chunk), lambda c: (0, c)),
        out_shape=jax.ShapeDtypeStruct((rows, seq), F32),
        compiler_params=_params(("arbitrary",)),
        name="mlstm_gates",
    )(f_pre)


def _mlstm_kernel(q_ref, k_ref, v_ref, xc_ref, za_ref, ir_ref, br_ref, lnw_ref, skip_ref,
                  o_ref, ct_ref, m_ref):
    c = pl.program_id(2)
    L = q_ref.shape[0]
    n_heads, hd = ct_ref.shape[0], ct_ref.shape[1]

    @pl.when(c == 0)
    def _():
        ct_ref[...] = jnp.zeros_like(ct_ref)
        m_ref[...] = jnp.zeros_like(m_ref)

    def lanes(x, width):
        return jnp.concatenate([x] * (width // LANES), axis=1)

    row = lax.broadcasted_iota(jnp.int32, (L, L), 0)
    col = lax.broadcasted_iota(jnp.int32, (L, L), 1)
    causal = row >= col
    eye = jnp.where(row == col, 1.0, 0.0).astype(BF16)

    def to_col(x_row):
        return sum(_dot_nt(eye, jnp.broadcast_to(part, (LANES, L))) for part in _split_bf16(x_row, 3))

    heads = range(n_heads)
    cols = [slice(i * hd, (i + 1) * hd) for i in heads]
    q = [q_ref[:, cols[i]] for i in heads]
    k = [k_ref[:, cols[i]] for i in heads]
    v_aug = [jnp.concatenate([v_ref[:, cols[i]], jnp.ones((L, LANES), BF16)], axis=1) for i in heads]
    li_row = [ir_ref[i] for i in heads]
    b_row = [br_ref[i] for i in heads]
    m_prev = [m_ref[i] for i in heads]

    qk = [_dot_nt(q[i], k[i]) for i in heads]
    q_ct = [_dot(q[i], ct_ref[i].astype(BF16)) for i in heads]
    b_col = [to_col(b_row[i]) for i in heads]
    li_col = [to_col(li_row[i]) for i in heads]
    b_last = [b_col[i][L - 1:L] for i in heads]

    d_log = [jnp.where(causal, lanes(b_col[i], L) - b_row[i] + li_row[i], -jnp.inf) for i in heads]
    inter = [b_col[i] + m_prev[i] for i in heads]
    m_t = [jnp.maximum(inter[i], jnp.max(d_log[i], axis=-1, keepdims=True)) for i in heads]
    s = [(qk[i] * jnp.exp(d_log[i] - lanes(m_t[i], L))).astype(BF16) for i in heads]
    sc = [jnp.exp(inter[i] - m_t[i]) for i in heads]
    num_den = [_dot(s[i], v_aug[i]) + lanes(sc[i], hd + LANES) * q_ct[i] for i in heads]

    g_log = [b_last[i] - b_col[i] + li_col[i] for i in heads]
    m_new = [jnp.maximum(b_last[i] + m_prev[i], jnp.max(g_log[i], axis=0, keepdims=True)) for i in heads]
    ke = [k[i] * lanes(jnp.exp(g_log[i] - m_new[i]).astype(BF16), hd) for i in heads]
    decay = [jnp.exp(b_last[i] + m_prev[i] - m_new[i]) for i in heads]
    for i in heads:
        ct_ref[i] = lanes(decay[i], hd + LANES) * ct_ref[i] + _dot_tn(ke[i], v_aug[i])
        m_ref[i] = m_new[i]

    for i in heads:
        inv = 1.0 / jnp.maximum(jnp.abs(num_den[i][:, hd:]), jnp.exp(-m_t[i]))
        h = num_den[i][:, :hd] * lanes(inv, hd)
        hc = h - jnp.mean(h, axis=-1, keepdims=True)
        hn = hc * lax.rsqrt(jnp.mean(hc * hc, axis=-1, keepdims=True) + EPS) * lnw_ref[:, cols[i]]
        hs = hn + skip_ref[:, cols[i]] * xc_ref[:, cols[i]].astype(F32)
        o_ref[:, cols[i]] = (hs * za_ref[:, cols[i]].astype(F32)).astype(o_ref.dtype)


def _mlstm(q, k, v, xc, z_act, i_row, b_row, lnw, skip, bsz, seq):
    n = q.shape[0]
    L = MLSTM_CHUNK
    nc = seq // L
    hd = C_HEAD
    hp = MLSTM_HEADS_PER_STEP
    blk = lambda: pl.BlockSpec((L, hp * hd), lambda b, h, c: (b * nc + c, h))
    rowspec = lambda: pl.BlockSpec((None, hp, 1, L), lambda b, h, c: (b, h, 0, c))
    vec = lambda: pl.BlockSpec((1, hp * hd), lambda b, h, c: (0, h))
    return pl.pallas_call(
        _mlstm_kernel,
        grid=(bsz, C_HEADS // hp, nc),
        in_specs=[blk(), blk(), blk(), blk(), blk(), rowspec(), rowspec(), vec(), vec()],
        out_specs=blk(),
        out_shape=jax.ShapeDtypeStruct((n, C_WIDTH), BF16),
        scratch_shapes=[pltpu.VMEM((hp, hd, hd + LANES), F32), pltpu.VMEM((hp, 1, LANES), F32)],
        compiler_params=_params(("arbitrary", "arbitrary", "arbitrary")),
        name="mlstm",
    )(q, k, v, xc, z_act, i_row, b_row, lnw, skip)


def _pack_block_diag(w, tile):
    g, bs, _ = w.shape
    per = tile // bs
    rows = jnp.tile(w.reshape(g // per, tile, bs), (1, 1, per))
    r = lax.broadcasted_iota(jnp.int32, (tile, tile), 0) // bs
    c = lax.broadcasted_iota(jnp.int32, (tile, tile), 1) // bs
    return jnp.where(r == c, rows, 0.0)


def _row(v):
    return v.reshape(1, -1)


def _even_layer(x, bsz, seq, norm, w_in, a_conv_w, a_conv_b, a_w_r, a_b_r, a_w_i, a_b_i, a_lambda,
                b_mu, b_w0, b_w_up, b_a0, b_a_up, b_g_up, b_k_k, b_k_a, b_r_k, b_ln_w, b_ln_b, w_out):
    main_w = 2 * A_WIDTH + 3 * B_WIDTH
    n_small = B_DECAY_RANK + B_AAA_RANK + B_GATE_RANK
    pad = B_SMALL - n_small
    assert main_w % B_SMALL == 0
    w_all = jnp.pad(w_in, ((0, 0), (0, pad))).astype(BF16)
    p, ya = _even_in(x, _row(norm), w_all, a_conv_w, a_conv_b,
                     _pack_block_diag(a_w_r, RGLRU_GROUP).astype(BF16), a_b_r,
                     _pack_block_diag(a_w_i, RGLRU_GROUP).astype(BF16), a_b_i, a_lambda, seq, PROJ_TM, PROJ_TN)

    mur, muk, muv = (_row(b_mu[i * B_WIDTH:(i + 1) * B_WIDTH]) for i in range(3))
    mus = _row(jnp.pad(b_mu[3 * B_WIDTH:], (0, pad)))

    def rows_at(w, start):
        out = jnp.zeros((B_SMALL_K, B_WIDTH), F32)
        return lax.dynamic_update_slice(out, w, (start, 0)).astype(BF16)

    wup = rows_at(b_w_up, 0)
    aup = rows_at(b_a_up, B_DECAY_RANK)
    gup = rows_at(b_g_up, B_DECAY_RANK + B_AAA_RANK)
    yb = _rwkv(p, mur, muk, muv, mus, _row(b_w0), wup, _row(b_a0), aup, gup, _row(b_k_k), _row(b_k_a),
               _row(b_r_k), _row(b_ln_w), _row(b_ln_b), bsz, seq)

    wo = w_out.astype(BF16)
    return _resid_matmul(x, [ya, yb], [wo[:A_WIDTH], wo[A_WIDTH:]], PROJ_TM, PROJ_TN_WIDE, "even_out")


def _odd_layer(x, bsz, seq, norm, w_in, conv_w, conv_b, w_q, w_k, w_v, w_if, b_if, ln_w, skip, w_out):
    w_in_b = w_in.astype(BF16)
    g = _row(norm)
    z_act = _norm_matmul(x, g, w_in_b, PROJ_TM, PROJ_TN_WIDE, "odd_in_z", col_start=C_WIDTH, silu_bf16=True)
    q, k, v, xc, gates = _mlstm_in(
        x, g, w_in_b, conv_w, _row(conv_b),
        _pack_block_diag(w_q, MXU_DIM).astype(BF16), _pack_block_diag(w_k, MXU_DIM).astype(BF16),
        _pack_block_diag(w_v, MXU_DIM).astype(BF16), w_if.astype(BF16), _row(b_if), seq,
        PROJ_TM, MLSTM_IN_COLS, MLSTM_IN_SUB)
    gt = jnp.transpose(gates.reshape(bsz, seq, 2, C_HEADS), (2, 0, 3, 1))
    i_pre = gt[0]
    b_cum = _mlstm_gates(gt[1].reshape(bsz * C_HEADS, seq), MLSTM_CHUNK).reshape(bsz, C_HEADS, seq)
    hs = _mlstm(q, k, v, xc, z_act, i_pre[:, :, None, :], b_cum[:, :, None, :], _row(ln_w), _row(skip),
                bsz, seq)
    return _resid_matmul(x, [hs], [w_out.astype(BF16)], PROJ_TM, PROJ_TN_WIDE, "odd_out")


def kernel(x, even_norm, even_w_in, a_conv_w, a_conv_b, a_w_r, a_b_r, a_w_i, a_b_i, a_lambda, b_mu, b_w0, b_w_up, b_a0, b_a_up, b_g_up, b_k_k, b_k_a, b_r_k, b_ln_w, b_ln_b, even_w_out, odd_norm, odd_w_in, c_conv_w, c_conv_b, c_w_q, c_w_k, c_w_v, c_w_if, c_b_if, c_ln_w, c_skip, odd_w_out, ffn_norm, ffn_w_gate, ffn_w_up, ffn_conv_w, ffn_conv_b, ffn_w_down, final_norm):
    bsz, seq, d = x.shape
    depth = ffn_norm.shape[0]
    h = x.reshape(bsz * seq, d)
    w_gate, w_up, w_down = ffn_w_gate.astype(BF16), ffn_w_up.astype(BF16), ffn_w_down.astype(BF16)
    for layer in range(depth):
        if layer % 2 == 0:
            e = layer // 2
            h = _even_layer(h, bsz, seq, even_norm[e], even_w_in[e], a_conv_w[e], a_conv_b[e], a_w_r[e],
                            a_b_r[e], a_w_i[e], a_b_i[e], a_lambda[e], b_mu[e], b_w0[e], b_w_up[e], b_a0[e],
                            b_a_up[e], b_g_up[e], b_k_k[e], b_k_a[e], b_r_k[e].reshape(-1), b_ln_w[e],
                            b_ln_b[e], even_w_out[e])
        else:
            o = layer // 2
            h = _odd_layer(h, bsz, seq, odd_norm[o], odd_w_in[o], c_conv_w[o], c_conv_b[o], c_w_q[o], c_w_k[o],
                           c_w_v[o], c_w_if[o], c_b_if[o], c_ln_w[o], c_skip[o], odd_w_out[o])
        h = _ffn(h, _row(ffn_norm[layer]), w_gate, w_up, ffn_conv_w[layer], _row(ffn_conv_b[layer]), w_down,
                 _row(final_norm), layer, seq, FFN_TM, FFN_TF, layer == depth - 1, "ffn%d" % layer)
    return h.reshape(bsz, seq, d)
```

```python
import functools
import math

import jax
import jax.numpy as jnp
from jax import lax
from jax.experimental import pallas as pl
from jax.experimental.pallas import tpu as pltpu

F32 = jnp.float32
BF16 = jnp.bfloat16

EPS = 1e-6
A_WIDTH = 1024
A_CONV = 4
LRU_C = 8.0
B_WIDTH = 1024
B_HEAD = 64
B_DECAY_RANK = 64
B_AAA_RANK = 64
B_GATE_RANK = 160
B_SMALL = 512
B_SMALL_K = 384
B_LN_EPS = 64e-5
B_KK_FLOOR = 1e-12
C_WIDTH = 4096
C_HEADS = 8
C_HEAD = 512
C_CONV = 4
FFN_CONV = 3

SUBLANES = 8
LANES = 128
MXU_DIM = 256
VMEM_LIMIT = 56 * 1024 * 1024

RWKV_CHUNK = 64
RWKV_GROUP = 4 * B_HEAD
RWKV_ROWS = 512
MLSTM_CHUNK = 256
MLSTM_HEADS_PER_STEP = 8
PROJ_TM, PROJ_TN = 1024, 512
NORM_SUB_ROWS = 256
EVEN_OUT_TM = 512
PROJ_TN_WIDE = 1024
FFN_TM, FFN_TF = 512, 512
RGLRU_GROUP = MXU_DIM
MLSTM_IN_COLS, MLSTM_IN_SUB = 512, 256


def _params(sem):
    return pltpu.CompilerParams(dimension_semantics=sem, vmem_limit_bytes=VMEM_LIMIT)


def _dot(a, b):
    return jnp.dot(a, b, preferred_element_type=F32)


def _dot_nt(a, b):
    return lax.dot_general(a, b, (((1,), (1,)), ((), ())), preferred_element_type=F32)


def _dot_tn(a, b):
    return lax.dot_general(a, b, (((0,), (0,)), ((), ())), preferred_element_type=F32)


def _split_bf16(x, terms):
    parts = []
    for _ in range(terms):
        p = x.astype(BF16)
        parts.append(p)
        x = x - p.astype(F32)
    return parts


def _sigmoid(x):
    return 1.0 / (1.0 + jnp.exp(-x))


def _silu(x):
    return x * _sigmoid(x)


def _rms(x, g):
    return x * lax.rsqrt(jnp.mean(x * x, axis=-1, keepdims=True) + EPS) * g


def _shift_rows(x, k, prev8):
    r = pltpu.roll(x, k, 0)
    fix = pltpu.roll(prev8, k, 0)
    row = lax.broadcasted_iota(jnp.int32, (SUBLANES, x.shape[1]), 0)
    head = jnp.where(row < k, fix, r[:SUBLANES])
    return jnp.concatenate([head, r[SUBLANES:]], axis=0)


def _norm_matmul_kernel(x_ref, g_ref, w_ref, o_ref, hn_ref, *, silu):
    j = pl.program_id(1)
    tm = x_ref.shape[0]

    def store(rows, y):
        o_ref[rows, :] = (_silu(y) if silu else y).astype(o_ref.dtype)

    @pl.when(j == 0)
    def _():
        for s in range(tm // NORM_SUB_ROWS):
            rows = pl.ds(s * NORM_SUB_ROWS, NORM_SUB_ROWS)
            hn = _rms(x_ref[rows, :], g_ref[...]).astype(BF16)
            hn_ref[rows, :] = hn
            store(rows, _dot(hn, w_ref[...]))

    @pl.when(j != 0)
    def _():
        store(pl.ds(0, tm), _dot(hn_ref[...], w_ref[...]))


def _norm_matmul(x, g, w, tm, tn, name, col_start=0, silu_bf16=False):
    n, d = x.shape
    nout = w.shape[1] - col_start
    off = col_start // tn
    assert col_start % tn == 0 and nout % tn == 0
    return pl.pallas_call(
        functools.partial(_norm_matmul_kernel, silu=silu_bf16),
        grid=(n // tm, nout // tn),
        in_specs=[pl.BlockSpec((tm, d), lambda i, j: (i, 0)),
                  pl.BlockSpec((1, d), lambda i, j: (0, 0)),
                  pl.BlockSpec((d, tn), lambda i, j: (0, j + off))],
        out_specs=pl.BlockSpec((tm, tn), lambda i, j: (i, j)),
        out_shape=jax.ShapeDtypeStruct((n, nout), BF16 if silu_bf16 else F32),
        scratch_shapes=[pltpu.VMEM((tm, d), BF16)],
        compiler_params=_params(("arbitrary", "arbitrary")),
        name=name,
    )(x, g, w)


def _resid_matmul_kernel(*refs, n_in):
    x_ref = refs[0]
    a_refs = refs[1:1 + n_in]
    w_refs = refs[1 + n_in:1 + 2 * n_in]
    o_ref = refs[1 + 2 * n_in]
    acc = x_ref[...]
    for a_ref, w_ref in zip(a_refs, w_refs):
        acc = acc + _dot(a_ref[...], w_ref[...])
    o_ref[...] = acc


def _resid_matmul(x, acts, ws, tm, tn, name):
    n, d = x.shape
    n_in = len(acts)
    in_specs = [pl.BlockSpec((tm, tn), lambda i, j: (i, j))]
    in_specs += [pl.BlockSpec((tm, a.shape[1]), lambda i, j: (i, 0)) for a in acts]
    in_specs += [pl.BlockSpec((w.shape[0], tn), lambda i, j: (0, j)) for w in ws]
    return pl.pallas_call(
        functools.partial(_resid_matmul_kernel, n_in=n_in),
        grid=(n // tm, d // tn),
        in_specs=in_specs,
        out_specs=pl.BlockSpec((tm, tn), lambda i, j: (i, j)),
        out_shape=jax.ShapeDtypeStruct((n, d), F32),
        compiler_params=_params(("arbitrary", "arbitrary")),
        name=name,
    )(x, *acts, *ws)


def _ffn_kernel(x_ref, g_ref, wg_ref, wu_ref, cw_ref, cb_ref, wd_ref, fg_ref, o_ref, hn_ref, carry_ref,
                *, tiles_per_seq, final_norm):
    i = pl.program_id(0)
    j = pl.program_id(1)
    tm = x_ref.shape[0]

    @pl.when(j == 0)
    def _():
        x = x_ref[...]
        hn_ref[...] = _rms(x, g_ref[...]).astype(BF16)
        o_ref[...] = x

    hn = hn_ref[...]
    gate = _dot(hn, wg_ref[...])
    up = _dot(hn, wu_ref[...])
    seq_start = (i % tiles_per_seq) == 0
    prev8 = jnp.where(seq_start, 0.0, carry_ref[j])
    carry_ref[j] = gate[tm - SUBLANES:]
    cw = cw_ref[...]
    conv = (cb_ref[...] + gate * cw[2:3] + _shift_rows(gate, 1, prev8) * cw[1:2]
            + _shift_rows(gate, 2, prev8) * cw[0:1])
    u = (_silu(conv) * up).astype(BF16)
    o_ref[...] += _dot(u, wd_ref[...])

    if final_norm:
        @pl.when(j == pl.num_programs(1) - 1)
        def _():
            o_ref[...] = _rms(o_ref[...], fg_ref[...])


def _ffn(x, g, wg, wu, cw, cb, wd, fg, layer, seq, tm, tf, final_norm, name):
    n, d = x.shape
    f = wg.shape[2]
    nf = f // tf
    kern = functools.partial(_ffn_kernel, tiles_per_seq=seq // tm, final_norm=final_norm)
    return pl.pallas_call(
        kern,
        grid=(n // tm, nf),
        in_specs=[pl.BlockSpec((tm, d), lambda i, j: (i, 0)),
                  pl.BlockSpec((1, d), lambda i, j: (0, 0)),
                  pl.BlockSpec((None, d, tf), lambda i, j: (layer, 0, j)),
                  pl.BlockSpec((None, d, tf), lambda i, j: (layer, 0, j)),
                  pl.BlockSpec((FFN_CONV, tf), lambda i, j: (0, j)),
                  pl.BlockSpec((1, tf), lambda i, j: (0, j)),
                  pl.BlockSpec((None, tf, d), lambda i, j: (layer, j, 0)),
                  pl.BlockSpec((1, d), lambda i, j: (0, 0))],
        out_specs=pl.BlockSpec((tm, d), lambda i, j: (i, 0)),
        out_shape=jax.ShapeDtypeStruct((n, d), F32),
        scratch_shapes=[pltpu.VMEM((tm, d), BF16), pltpu.VMEM((nf, SUBLANES, tf), F32)],
        compiler_params=_params(("arbitrary", "arbitrary")),
        name=name,
    )(x, g, wg, wu, cw, cb, wd, fg)


def _rglru_gates(x, prev8, cw, cb, wr, wi):
    xc = cb + x * cw[A_CONV - 1:A_CONV]
    for k in range(1, A_CONV):
        xc = xc + _shift_rows(x, k, prev8) * cw[A_CONV - 1 - k:A_CONV - k]
    xb = xc.astype(BF16)
    return xc, _dot(xb, wr), _dot(xb, wi)


def _rglru_scan(xc, r_pre, i_pre, ga, h0, br, bi, lam):
    rows, width = xc.shape
    r = _sigmoid(r_pre + br)
    ig = _sigmoid(i_pre + bi)
    neg_lam = -lam
    softplus = jnp.maximum(neg_lam, 0.0) + jnp.log1p(jnp.exp(-jnp.abs(neg_lam)))
    log_a = (-LRU_C) * r * softplus
    a = jnp.exp(log_a)
    u = jnp.sqrt(jnp.maximum(1.0 - a * a, 0.0)) * (ig * xc)

    n_sub = rows // SUBLANES
    a3 = a.reshape(n_sub, SUBLANES, width)
    u3 = u.reshape(n_sub, SUBLANES, width)
    sub = lax.broadcasted_iota(jnp.int32, (n_sub, SUBLANES, width), 1)
    s = 1
    while s < SUBLANES:
        keep = sub >= s
        a_prev = pltpu.roll(a3, s, 1)
        u_prev = pltpu.roll(u3, s, 1)
        u3 = jnp.where(keep, a3 * u_prev + u3, u3)
        a3 = jnp.where(keep, a3 * a_prev, a3)
        s *= 2
    carry = h0
    groups = []
    for gi in range(n_sub):
        hg = u3[gi] + a3[gi] * carry
        groups.append(hg)
        carry = hg[SUBLANES - 1:]
    h = jnp.concatenate(groups, axis=0)
    gelu = 0.5 * ga * (1.0 + jnp.tanh(math.sqrt(2.0 / math.pi) * (ga + 0.044715 * (ga * ga * ga))))
    return (h * gelu).astype(BF16), carry


def _even_in_kernel(x_ref, g_ref, w_ref, cw_ref, cb_ref, wr_ref, br_ref, wi_ref, bi_ref, lam_ref,
                    p_ref, ya_ref, hn_ref, act_ref, prev_ref, h_ref, *, tiles_per_seq):
    i = pl.program_id(0)
    j = pl.program_id(1)
    tm = x_ref.shape[0]
    tn = w_ref.shape[1]
    grp = RGLRU_GROUP
    n_units = A_WIDTH // grp
    n_act = 2 * A_WIDTH // tn
    per_tile = tn // grp

    def stash(rows, y, tile):
        for part in range(per_tile):
            act_ref[per_tile * tile + part, rows, :] = y[:, part * grp:(part + 1) * grp]

    @pl.when(j == 0)
    def _():
        for s in range(tm // NORM_SUB_ROWS):
            rows = pl.ds(s * NORM_SUB_ROWS, NORM_SUB_ROWS)
            hn = _rms(x_ref[rows, :], g_ref[...]).astype(BF16)
            hn_ref[rows, :] = hn
            stash(rows, _dot(hn, w_ref[...]), 0)

    @pl.when((j > 0) & (j < n_act))
    def _():
        stash(pl.ds(0, tm), _dot(hn_ref[...], w_ref[...]), j)

    @pl.when((j >= n_act) & (j < n_act + n_units))
    def _():
        unit = j - n_act
        seq_start = (i % tiles_per_seq) == 0
        prev8 = jnp.where(seq_start, 0.0, prev_ref[unit])
        h_last = jnp.where(seq_start, 0.0, h_ref[unit])
        gate_w = (cw_ref[unit], cb_ref[unit], wr_ref[unit], wi_ref[unit])
        scan_w = (br_ref[unit], bi_ref[unit], lam_ref[unit])
        sub = lambda s: pl.ds(s * NORM_SUB_ROWS, NORM_SUB_ROWS)
        for s in range(tm // NORM_SUB_ROWS):
            x = act_ref[unit, sub(s), :]
            gated = _rglru_gates(x, prev8, *gate_w)
            prev8 = x[NORM_SUB_ROWS - SUBLANES:]
            projected = _dot(hn_ref[sub(s), :], w_ref[...])
            y, h_last = _rglru_scan(*gated, act_ref[n_units + unit, sub(s), :], h_last, *scan_w)
            p_ref[sub(s), :] = projected
            ya_ref[sub(s), :] = y
        prev_ref[unit] = prev8
        h_ref[unit] = h_last

    @pl.when(j >= n_act + n_units)
    def _():
        p_ref[...] = _dot(hn_ref[...], w_ref[...])


def _even_in(x, g, w_all, cw, cb, wr, br, wi, bi, lam, seq, tm, tn):
    n, d = x.shape
    grp = RGLRU_GROUP
    n_units = A_WIDTH // grp
    n_tiles = w_all.shape[1] // tn
    n_act = 2 * A_WIDTH // tn
    assert (2 * A_WIDTH) % tn == 0 and tn % grp == 0 and n_tiles >= n_act + n_units
    group_vec = lambda v: v.reshape(n_units, 1, grp)
    whole = lambda a: pl.BlockSpec(a.shape, lambda i, j: (0,) * a.ndim)
    params = [jnp.transpose(cw.reshape(A_CONV, n_units, grp), (1, 0, 2)), group_vec(cb), wr, group_vec(br),
              wi, group_vec(bi), group_vec(lam)]
    return pl.pallas_call(
        functools.partial(_even_in_kernel, tiles_per_seq=seq // tm),
        grid=(n // tm, n_tiles),
        in_specs=[pl.BlockSpec((tm, d), lambda i, j: (i, 0)),
                  pl.BlockSpec((1, d), lambda i, j: (0, 0)),
                  pl.BlockSpec((d, tn), lambda i, j: (0, j))] + [whole(a) for a in params],
        out_specs=[pl.BlockSpec((tm, tn), lambda i, j: (i, jnp.maximum(j - n_act, 0))),
                   pl.BlockSpec((tm, grp), lambda i, j: (i, jnp.clip(j - n_act, 0, n_units - 1)))],
        out_shape=[jax.ShapeDtypeStruct((n, w_all.shape[1] - 2 * A_WIDTH), F32),
                   jax.ShapeDtypeStruct((n, A_WIDTH), BF16)],
        scratch_shapes=[pltpu.VMEM((tm, d), BF16), pltpu.VMEM((2 * n_units, tm, grp), F32),
                        pltpu.VMEM((n_units, SUBLANES, grp), F32), pltpu.VMEM((n_units, 1, grp), F32)],
        compiler_params=_params(("arbitrary", "arbitrary")),
        name="even_in",
    )(x, g, w_all, *params)


def _alternate(*stage_streams):
    results = [None] * len(stage_streams)
    live = list(range(len(stage_streams)))
    while live:
        for idx in list(live):
            try:
                next(stage_streams[idx])
            except StopIteration as stop:
                results[idx] = stop.value
                live.remove(idx)
    return results


def _rwkv_kernel(r_ref, k_ref, v_ref, sm_ref, mur_ref, muk_ref, muv_ref, mus_ref, w0_ref, wup_ref, a0_ref,
                 aup_ref, gup_ref, kkw_ref, kaw_ref, rkw_ref, lnw_ref, lnb_ref, o_ref,
                 s_ref, pr_ref, pk_ref, pv_ref, ps_ref):
    tt = r_ref.shape[0]
    L = RWKV_CHUNK
    gw = RWKV_GROUP
    n_grp = B_WIDTH // gw
    hr = tt // 2

    @pl.when(pl.program_id(1) == 0)
    def _():
        s_ref[...] = jnp.zeros_like(s_ref)
        pr_ref[...] = jnp.zeros_like(pr_ref)
        pk_ref[...] = jnp.zeros_like(pk_ref)
        pv_ref[...] = jnp.zeros_like(pv_ref)
        ps_ref[...] = jnp.zeros_like(ps_ref)

    def lerp(x_ref, p_ref, mu_ref):
        x = x_ref[...]
        xs = _shift_rows(x, 1, p_ref[...])
        p_ref[...] = x[tt - SUBLANES:]
        return x + (xs - x) * mu_ref[...]

    r_all = lerp(r_ref, pr_ref, mur_ref)
    k_all = lerp(k_ref, pk_ref, muk_ref)
    v_all = lerp(v_ref, pv_ref, muv_ref)
    sm_all = lerp(sm_ref, ps_ref, mus_ref)[:, :B_SMALL_K]

    row_g = lax.broadcasted_iota(jnp.int32, (gw, gw), 0)
    col_g = lax.broadcasted_iota(jnp.int32, (gw, gw), 1)
    same_head = (row_g // B_HEAD) == (col_g // B_HEAD)
    head_mask = jnp.where(same_head, 1.0, 0.0)
    head_mask_bf = head_mask.astype(BF16)

    def head_sum(x, terms):
        n = x.shape[0]
        xs = jnp.concatenate([x[:, gi * gw:(gi + 1) * gw] for gi in range(n_grp)], axis=0)
        s = _dot(jnp.concatenate(_split_bf16(xs, terms), axis=0), head_mask_bf)
        s = sum(s[t * n_grp * n:(t + 1) * n_grp * n] for t in range(terms))
        return jnp.concatenate([s[gi * n:(gi + 1) * n] for gi in range(n_grp)], axis=1)

    def bd(x):
        return jnp.concatenate([x.astype(BF16)] * (gw // B_HEAD), axis=0) * head_mask_bf

    row_t = lax.broadcasted_iota(jnp.int32, (hr, hr), 0)
    col_t = lax.broadcasted_iota(jnp.int32, (hr, hr), 1)
    tril = jnp.where((row_t >= col_t) & ((row_t // L) == (col_t // L)), 1.0, 0.0).astype(BF16)
    row_p = lax.broadcasted_iota(jnp.int32, (L, gw), 0)
    src_p = lax.broadcasted_iota(jnp.int32, (L, gw), 1) % B_HEAD
    strict_lower = row_p > src_p
    lower = row_p >= src_p
    n_doublings = int(math.log2(L)) - 1
    groups = range(n_grp)
    half_units = [(c, gi) for c in range(hr // L) for gi in groups]

    def tile(arr, c, gi):
        return arr[c * L:(c + 1) * L, gi * gw:(gi + 1) * gw]

    def prepare(h):
        rows = slice(h * hr, (h + 1) * hr)
        r, k, v, sm = r_all[rows], k_all[rows], v_all[rows], sm_all[rows]
        z = w0_ref[...] + _dot(jnp.tanh(sm).astype(BF16), wup_ref[...])
        log_w = (-math.exp(-0.5)) * _sigmoid(z)
        yield
        a = _sigmoid(a0_ref[...] + _dot(sm.astype(BF16), aup_ref[...]))
        yield
        g = _dot(_sigmoid(sm).astype(BF16), gup_ref[...])
        kk = k * kkw_ref[...]
        yield
        kk = kk * lax.rsqrt(jnp.maximum(head_sum(kk * kk, 2), B_KK_FLOOR))
        yield
        k2 = k * (1.0 + (a - 1.0) * kaw_ref[...])
        yield
        cum = sum(_dot(tril, part) for part in _split_bf16(log_w, 3))
        yield
        p_in = jnp.exp(cum)
        p_inv = jnp.exp(-cum)
        yield
        a_bar = (-kk) * jnp.exp(cum - log_w)
        r_bar = r * p_in
        yield
        b_bar = kk * a * p_inv
        k_bar = k2 * p_inv
        return dict(r=r, v=v, g=g, k2=k2, p_in=p_in, a_bar=a_bar, r_bar=r_bar, b_bar=b_bar, k_bar=k_bar)

    def products(pre, states):
        units = half_units
        ar = {u: jnp.concatenate([tile(pre["a_bar"], *u), tile(pre["r_bar"], *u)], axis=0).astype(BF16)
              for u in units}
        m_b = {u: _dot_nt(ar[u], bd(tile(pre["b_bar"], *u))) for u in units}
        yield
        m_k = {u: _dot_nt(ar[u], bd(tile(pre["k_bar"], *u))) for u in units}
        yield
        x = {u: jnp.where(strict_lower, m_b[u][:L], 0.0) for u in units}
        a_rb = {u: jnp.where(lower, m_b[u][L:], 0.0).astype(BF16) for u in units}
        akrk = {u: jnp.concatenate([jnp.where(strict_lower, m_k[u][:L], 0.0),
                                    jnp.where(lower, m_k[u][L:], 0.0)], axis=0).astype(BF16) for u in units}
        cy = {u: _dot(akrk[u], bd(tile(pre["v"], *u))) for u in units}
        yield
        n_inv = dict(x)
        x_pow = {u: _dot(x[u].astype(BF16), bd(x[u])) for u in units}
        yield
        for step in range(n_doublings):
            if step + 1 < n_doublings:
                both = {u: _dot(jnp.concatenate([x_pow[u], n_inv[u]], axis=0).astype(BF16), bd(x_pow[u]))
                        for u in units}
                n_inv = {u: n_inv[u] + x_pow[u] + both[u][L:] for u in units}
                x_pow = {u: both[u][:L] for u in units}
            else:
                n_inv = {u: n_inv[u] + x_pow[u] + _dot(n_inv[u].astype(BF16), bd(x_pow[u])) for u in units}
            yield
        n_inv = {u: n_inv[u].astype(BF16) for u in units}
        y_rows = []
        for c in range(hr // L):
            p_last = pre["p_in"][(c + 1) * L - 1:(c + 1) * L]
            pl_g = [p_last[:, gi * gw:(gi + 1) * gw] for gi in groups]
            bk = [jnp.concatenate([tile(pre["b_bar"], c, gi) * pl_g[gi], tile(pre["k_bar"], c, gi) * pl_g[gi]],
                                  axis=0).astype(BF16) for gi in groups]
            ah = [_dot_nt(ar[c, gi], states[gi].astype(BF16)) for gi in groups]
            yield
            rhs = [ah[gi][:L] + cy[c, gi][:L] for gi in groups]
            u_c = [rhs[gi] + _dot(n_inv[c, gi], bd(rhs[gi])) for gi in groups]
            yield
            ds = [_dot_tn(jnp.concatenate([u_c[gi], tile(pre["v"], c, gi)], axis=0).astype(BF16), bk[gi])
                  for gi in groups]
            states = [states[gi] * pl_g[gi] + head_mask * ds[gi] for gi in groups]
            y_rows.append(jnp.concatenate(
                [ah[gi][L:] + cy[c, gi][L:] + _dot(a_rb[c, gi], bd(u_c[gi])) for gi in groups], axis=1))
            yield
        return jnp.concatenate(y_rows, axis=0), states

    def finish(h, pre, y):
        inv_n = 1.0 / B_HEAD
        yc = y - head_sum(y, 1) * inv_n
        yield
        var = head_sum(yc * yc, 1) * inv_n
        yield
        yn = yc * lax.rsqrt(var + B_LN_EPS) * lnw_ref[...] + lnb_ref[...]
        yield
        bonus = head_sum(pre["r"] * pre["k2"] * rkw_ref[...], 1) * pre["v"]
        yield
        o_ref[pl.ds(h * hr, hr), :] = ((yn + bonus) * pre["g"]).astype(o_ref.dtype)

    states = [s_ref[gi] for gi in groups]
    (pre0,) = _alternate(prepare(0))
    (y0, states), pre1 = _alternate(products(pre0, states), prepare(1))
    (y1, states), _ = _alternate(products(pre1, states), finish(0, pre0, y0))
    _alternate(finish(1, pre1, y1))
    for gi in groups:
        s_ref[gi] = states[gi]


def _rwkv(p, mur, muk, muv, mus, w0, wup, a0, aup, gup, kkw, kaw, rkw, lnw, lnb, bsz, seq):
    n = p.shape[0]
    tt = RWKV_ROWS
    nt = seq // tt
    w = B_WIDTH
    vec = lambda: pl.BlockSpec((1, w), lambda b, t: (0, 0))
    mat = lambda: pl.BlockSpec((B_SMALL_K, w), lambda b, t: (0, 0))
    return pl.pallas_call(
        _rwkv_kernel,
        grid=(bsz, nt),
        in_specs=[pl.BlockSpec((tt, w), lambda b, t: (b * nt + t, 0)),
                  pl.BlockSpec((tt, w), lambda b, t: (b * nt + t, 1)),
                  pl.BlockSpec((tt, w), lambda b, t: (b * nt + t, 2)),
                  pl.BlockSpec((tt, B_SMALL), lambda b, t: (b * nt + t, 3 * w // B_SMALL)),
                  vec(), vec(), vec(), pl.BlockSpec((1, B_SMALL), lambda b, t: (0, 0)),
                  vec(), mat(), vec(), mat(), mat(), vec(), vec(), vec(), vec(), vec()],
        out_specs=pl.BlockSpec((tt, w), lambda b, t: (b * nt + t, 0)),
        out_shape=jax.ShapeDtypeStruct((n, w), BF16),
        scratch_shapes=[pltpu.VMEM((w // RWKV_GROUP, RWKV_GROUP, RWKV_GROUP), F32),
                        pltpu.VMEM((SUBLANES, w), F32), pltpu.VMEM((SUBLANES, w), F32),
                        pltpu.VMEM((SUBLANES, w), F32), pltpu.VMEM((SUBLANES, B_SMALL), F32)],
        compiler_params=_params(("arbitrary", "arbitrary")),
        name="rwkv7",
    )(p, p, p, p, mur, muk, muv, mus, w0, wup, a0, aup, gup, kkw, kaw, rkw, lnw, lnb)


def _mlstm_in_kernel(x_ref, g_ref, w_ref, cw_ref, cb_ref, wq_ref, wk_ref, wv_ref, wif_ref, bif_ref,
                     q_ref, k_ref, v_ref, xc_ref, gates_ref, hn_ref, prev_ref, *, tiles_per_seq, sub_rows):
    i = pl.program_id(0)
    j = pl.program_id(1)
    tm = x_ref.shape[0]
    cb = w_ref.shape[1]
    n_grp = cb // MXU_DIM

    def blockdiag(xb, wb_ref):
        return jnp.concatenate(
            [_dot(xb[:, g * MXU_DIM:(g + 1) * MXU_DIM], wb_ref[g]) for g in range(n_grp)], axis=1)

    def body(first_col_tile):
        def project(s):
            rows = pl.ds(s * sub_rows, sub_rows)
            if first_col_tile:
                hn = _rms(x_ref[rows, :], g_ref[...]).astype(BF16)
                hn_ref[rows, :] = hn
            else:
                hn = hn_ref[rows, :]
            return _dot(hn, w_ref[...])

        seq_start = (i % tiles_per_seq) == 0
        prev8 = jnp.where(seq_start, 0.0, prev_ref[j])
        cw = cw_ref[...]
        n_sub = tm // sub_rows
        xm_next = project(0)
        for s in range(n_sub):
            rows = pl.ds(s * sub_rows, sub_rows)
            xm = xm_next
            if s + 1 < n_sub:
                xm_next = project(s + 1)
            conv = cb_ref[...] + xm * cw[C_CONV - 1:C_CONV]
            for kk in range(1, C_CONV):
                conv = conv + _shift_rows(xm, kk, prev8) * cw[C_CONV - 1 - kk:C_CONV - kk]
            prev8 = xm[sub_rows - SUBLANES:]
            xc = _silu(conv)
            xcb = xc.astype(BF16)
            xc_ref[rows, :] = xcb
            q = blockdiag(xcb, wq_ref)
            k = blockdiag(xcb, wk_ref)
            v = blockdiag(xm.astype(BF16), wv_ref)
            qb = q.astype(BF16)
            kb = k.astype(BF16)
            vb = v.astype(BF16)
            q_ref[rows, :] = qb
            k_ref[rows, :] = (k * (C_HEAD ** -0.5)).astype(BF16)
            v_ref[rows, :] = vb
            gate_part = _dot(qb, wif_ref[0]) + _dot(kb, wif_ref[1]) + _dot(vb, wif_ref[2])
            if first_col_tile:
                gates_ref[rows, :] = bif_ref[...] + gate_part
            else:
                gates_ref[rows, :] += gate_part
        prev_ref[j] = prev8

    pl.when(j == 0)(functools.partial(body, True))
    pl.when(j != 0)(functools.partial(body, False))


def _mlstm_in(x, g, w_in, cw, cb, wq, wk, wv, wif, bif, seq, tm, cblk, sub_rows):
    n, d = x.shape
    ncb = C_WIDTH // cblk
    gpb = cblk // MXU_DIM
    blk = lambda: pl.BlockSpec((tm, cblk), lambda i, j: (i, j))
    wspec = lambda: pl.BlockSpec((gpb, MXU_DIM, MXU_DIM), lambda i, j: (j, 0, 0))
    act = lambda dt: jax.ShapeDtypeStruct((n, C_WIDTH), dt)
    kern = functools.partial(_mlstm_in_kernel, tiles_per_seq=seq // tm, sub_rows=sub_rows)
    return pl.pallas_call(
        kern,
        grid=(n // tm, ncb),
        in_specs=[pl.BlockSpec((tm, d), lambda i, j: (i, 0)),
                  pl.BlockSpec((1, d), lambda i, j: (0, 0)),
                  pl.BlockSpec((d, cblk), lambda i, j: (0, j)),
                  pl.BlockSpec((C_CONV, cblk), lambda i, j: (0, j)),
                  pl.BlockSpec((1, cblk), lambda i, j: (0, j)),
                  wspec(), wspec(), wspec(),
                  pl.BlockSpec((3, cblk, 2 * C_HEADS), lambda i, j: (0, j, 0)),
                  pl.BlockSpec((1, 2 * C_HEADS), lambda i, j: (0, 0))],
        out_specs=[blk(), blk(), blk(), blk(),
                   pl.BlockSpec((tm, 2 * C_HEADS), lambda i, j: (i, 0))],
        out_shape=[act(BF16), act(BF16), act(BF16), act(BF16),
                   jax.ShapeDtypeStruct((n, 2 * C_HEADS), F32)],
        scratch_shapes=[pltpu.VMEM((tm, d), BF16), pltpu.VMEM((ncb, SUBLANES, cblk), F32)],
        compiler_params=_params(("arbitrary", "arbitrary")),
        name="mlstm_in",
    )(x, g, w_in, cw, cb, wq, wk, wv, wif, bif)


def _mlstm_gate_kernel(f_ref, b_ref):
    L = f_ref.shape[1]
    f = f_ref[...]
    lf = jnp.minimum(f, 0.0) - jnp.log1p(jnp.exp(-jnp.abs(f)))
    row = lax.broadcasted_iota(jnp.int32, (L, L), 0)
    col = lax.broadcasted_iota(jnp.int32, (L, L), 1)
    triu = jnp.where(row <= col, 1.0, 0.0).astype(BF16)
    b_ref[...] = sum(_dot(part, triu) for part in _split_bf16(lf, 3))


def _mlstm_gates(f_pre, chunk):
    rows, seq = f_pre.shape
    return pl.pallas_call(
        _mlstm_gate_kernel,
        grid=(seq // chunk,),
        in_specs=[pl.BlockSpec((rows, chunk), lambda c: (0, c))],
        out_specs=pl.BlockSpec((rows, chunk), lambda c: (0, c)),
        out_shape=jax.ShapeDtypeStruct((rows, seq), F32),
        compiler_params=_params(("arbitrary",)),
        name="mlstm_gates",
    )(f_pre)


def _mlstm_kernel(q_ref, k_ref, v_ref, xc_ref, za_ref, ir_ref, br_ref, lnw_ref, skip_ref,
                  o_ref, ct_ref, m_ref):
    c = pl.program_id(2)
    L = q_ref.shape[0]
    n_heads, hd = ct_ref.shape[0], ct_ref.shape[1]

    @pl.when(c == 0)
    def _():
        ct_ref[...] = jnp.zeros_like(ct_ref)
        m_ref[...] = jnp.zeros_like(m_ref)

    def lanes(x, width):
        return jnp.concatenate([x] * (width // LANES), axis=1)

    row = lax.broadcasted_iota(jnp.int32, (L, L), 0)
    col = lax.broadcasted_iota(jnp.int32, (L, L), 1)
    causal = row >= col
    eye = jnp.where(row == col, 1.0, 0.0).astype(BF16)

    def to_col(x_row):
        return sum(_dot_nt(eye, jnp.broadcast_to(part, (LANES, L))) for part in _split_bf16(x_row, 3))

    heads = range(n_heads)
    cols = [slice(i * hd, (i + 1) * hd) for i in heads]
    q = [q_ref[:, cols[i]] for i in heads]
    k = [k_ref[:, cols[i]] for i in heads]
    v_aug = [jnp.concatenate([v_ref[:, cols[i]], jnp.ones((L, LANES), BF16)], axis=1) for i in heads]
    li_row = [ir_ref[i] for i in heads]
    b_row = [br_ref[i] for i in heads]
    m_prev = [m_ref[i] for i in heads]

    qk = [_dot_nt(q[i], k[i]) for i in heads]
    q_ct = [_dot(q[i], ct_ref[i].astype(BF16)) for i in heads]
    b_col = [to_col(b_row[i]) for i in heads]
    li_col = [to_col(li_row[i]) for i in heads]
    b_last = [b_col[i][L - 1:L] for i in heads]

    d_log = [jnp.where(causal, lanes(b_col[i], L) - b_row[i] + li_row[i], -jnp.inf) for i in heads]
    inter = [b_col[i] + m_prev[i] for i in heads]
    m_t = [jnp.maximum(inter[i], jnp.max(d_log[i], axis=-1, keepdims=True)) for i in heads]
    s = [(qk[i] * jnp.exp(d_log[i] - lanes(m_t[i], L))).astype(BF16) for i in heads]
    sc = [jnp.exp(inter[i] - m_t[i]) for i in heads]
    num_den = [_dot(s[i], v_aug[i]) + lanes(sc[i], hd + LANES) * q_ct[i] for i in heads]

    g_log = [b_last[i] - b_col[i] + li_col[i] for i in heads]
    m_new = [jnp.maximum(b_last[i] + m_prev[i], jnp.max(g_log[i], axis=0, keepdims=True)) for i in heads]
    ke = [k[i] * lanes(jnp.exp(g_log[i] - m_new[i]).astype(BF16), hd) for i in heads]
    decay = [jnp.exp(b_last[i] + m_prev[i] - m_new[i]) for i in heads]
    for i in heads:
        ct_ref[i] = lanes(decay[i], hd + LANES) * ct_ref[i] + _dot_tn(ke[i], v_aug[i])
        m_ref[i] = m_new[i]

    for i in heads:
        inv = 1.0 / jnp.maximum(jnp.abs(num_den[i][:, hd:]), jnp.exp(-m_t[i]))
        h = num_den[i][:, :hd] * lanes(inv, hd)
        hc = h - jnp.mean(h, axis=-1, keepdims=True)
        hn = hc * lax.rsqrt(jnp.mean(hc * hc, axis=-1, keepdims=True) + EPS) * lnw_ref[:, cols[i]]
        hs = hn + skip_ref[:, cols[i]] * xc_ref[:, cols[i]].astype(F32)
        o_ref[:, cols[i]] = (hs * za_ref[:, cols[i]].astype(F32)).astype(o_ref.dtype)


def _mlstm(q, k, v, xc, z_act, i_row, b_row, lnw, skip, bsz, seq):
    n = q.shape[0]
    L = MLSTM_CHUNK
    nc = seq // L
    hd = C_HEAD
    hp = MLSTM_HEADS_PER_STEP
    blk = lambda: pl.BlockSpec((L, hp * hd), lambda b, h, c: (b * nc + c, h))
    rowspec = lambda: pl.BlockSpec((None, hp, 1, L), lambda b, h, c: (b, h, 0, c))
    vec = lambda: pl.BlockSpec((1, hp * hd), lambda b, h, c: (0, h))
    return pl.pallas_call(
        _mlstm_kernel,
        grid=(bsz, C_HEADS // hp, nc),
        in_specs=[blk(), blk(), blk(), blk(), blk(), rowspec(), rowspec(), vec(), vec()],
        out_specs=blk(),
        out_shape=jax.ShapeDtypeStruct((n, C_WIDTH), BF16),
        scratch_shapes=[pltpu.VMEM((hp, hd, hd + LANES), F32), pltpu.VMEM((hp, 1, LANES), F32)],
        compiler_params=_params(("arbitrary", "arbitrary", "arbitrary")),
        name="mlstm",
    )(q, k, v, xc, z_act, i_row, b_row, lnw, skip)


def _pack_block_diag(w, tile):
    g, bs, _ = w.shape
    per = tile // bs
    rows = jnp.tile(w.reshape(g // per, tile, bs), (1, 1, per))
    r = lax.broadcasted_iota(jnp.int32, (tile, tile), 0) // bs
    c = lax.broadcasted_iota(jnp.int32, (tile, tile), 1) // bs
    return jnp.where(r == c, rows, 0.0)


def _row(v):
    return v.reshape(1, -1)


def _even_layer(x, bsz, seq, norm, w_in, a_conv_w, a_conv_b, a_w_r, a_b_r, a_w_i, a_b_i, a_lambda,
                b_mu, b_w0, b_w_up, b_a0, b_a_up, b_g_up, b_k_k, b_k_a, b_r_k, b_ln_w, b_ln_b, w_out):
    main_w = 2 * A_WIDTH + 3 * B_WIDTH
    n_small = B_DECAY_RANK + B_AAA_RANK + B_GATE_RANK
    pad = B_SMALL - n_small
    assert main_w % B_SMALL == 0
    w_all = jnp.pad(w_in, ((0, 0), (0, pad))).astype(BF16)
    p, ya = _even_in(x, _row(norm), w_all, a_conv_w, a_conv_b,
                     _pack_block_diag(a_w_r, RGLRU_GROUP).astype(BF16), a_b_r,
                     _pack_block_diag(a_w_i, RGLRU_GROUP).astype(BF16), a_b_i, a_lambda, seq, PROJ_TM, PROJ_TN)

    mur, muk, muv = (_row(b_mu[i * B_WIDTH:(i + 1) * B_WIDTH]) for i in range(3))
    mus = _row(jnp.pad(b_mu[3 * B_WIDTH:], (0, pad)))

    def rows_at(w, start):
        out = jnp.zeros((B_SMALL_K, B_WIDTH), F32)
        return lax.dynamic_update_slice(out, w, (start, 0)).astype(BF16)

    wup = rows_at(b_w_up, 0)
    aup = rows_at(b_a_up, B_DECAY_RANK)
    gup = rows_at(b_g_up, B_DECAY_RANK + B_AAA_RANK)
    yb = _rwkv(p, mur, muk, muv, mus, _row(b_w0), wup, _row(b_a0), aup, gup, _row(b_k_k), _row(b_k_a),
               _row(b_r_k), _row(b_ln_w), _row(b_ln_b), bsz, seq)

    wo = w_out.astype(BF16)
    return _resid_matmul(x, [ya, yb], [wo[:A_WIDTH], wo[A_WIDTH:]], EVEN_OUT_TM, wo.shape[1], "even_out")


def _odd_layer(x, bsz, seq, norm, w_in, conv_w, conv_b, w_q, w_k, w_v, w_if, b_if, ln_w, skip, w_out):
    w_in_b = w_in.astype(BF16)
    g = _row(norm)
    z_act = _norm_matmul(x, g, w_in_b, PROJ_TM, PROJ_TN_WIDE, "odd_in_z", col_start=C_WIDTH, silu_bf16=True)
    q, k, v, xc, gates = _mlstm_in(
        x, g, w_in_b, conv_w, _row(conv_b),
        _pack_block_diag(w_q, MXU_DIM).astype(BF16), _pack_block_diag(w_k, MXU_DIM).astype(BF16),
        _pack_block_diag(w_v, MXU_DIM).astype(BF16), w_if.astype(BF16), _row(b_if), seq,
        PROJ_TM, MLSTM_IN_COLS, MLSTM_IN_SUB)
    gt = jnp.transpose(gates.reshape(bsz, seq, 2, C_HEADS), (2, 0, 3, 1))
    i_pre = gt[0]
    b_cum = _mlstm_gates(gt[1].reshape(bsz * C_HEADS, seq), MLSTM_CHUNK).reshape(bsz, C_HEADS, seq)
    hs = _mlstm(q, k, v, xc, z_act, i_pre[:, :, None, :], b_cum[:, :, None, :], _row(ln_w), _row(skip),
                bsz, seq)
    return _resid_matmul(x, [hs], [w_out.astype(BF16)], PROJ_TM, PROJ_TN_WIDE, "odd_out")


def kernel(x, even_norm, even_w_in, a_conv_w, a_conv_b, a_w_r, a_b_r, a_w_i, a_b_i, a_lambda, b_mu, b_w0, b_w_up, b_a0, b_a_up, b_g_up, b_k_k, b_k_a, b_r_k, b_ln_w, b_ln_b, even_w_out, odd_norm, odd_w_in, c_conv_w, c_conv_b, c_w_q, c_w_k, c_w_v, c_w_if, c_b_if, c_ln_w, c_skip, odd_w_out, ffn_norm, ffn_w_gate, ffn_w_up, ffn_conv_w, ffn_conv_b, ffn_w_down, final_norm):
    bsz, seq, d = x.shape
    depth = ffn_norm.shape[0]
    h = x.reshape(bsz * seq, d)
    w_gate, w_up, w_down = ffn_w_gate.astype(BF16), ffn_w_up.astype(BF16), ffn_w_down.astype(BF16)
    for layer in range(depth):
        if layer % 2 == 0:
            e = layer // 2
            h = _even_layer(h, bsz, seq, even_norm[e], even_w_in[e], a_conv_w[e], a_conv_b[e], a_w_r[e],
                            a_b_r[e], a_w_i[e], a_b_i[e], a_lambda[e], b_mu[e], b_w0[e], b_w_up[e], b_a0[e],
                            b_a_up[e], b_g_up[e], b_k_k[e], b_k_a[e], b_r_k[e].reshape(-1), b_ln_w[e],
                            b_ln_b[e], even_w_out[e])
        else:
            o = layer // 2
            h = _odd_layer(h, bsz, seq, odd_norm[o], odd_w_in[o], c_conv_w[o], c_conv_b[o], c_w_q[o], c_w_k[o],
                           c_w_v[o], c_w_if[o], c_b_if[o], c_ln_w[o], c_skip[o], odd_w_out[o])
        h = _ffn(h, _row(ffn_norm[layer]), w_gate, w_up, ffn_conv_w[layer], _row(ffn_conv_b[layer]), w_down,
                 _row(final_norm), layer, seq, FFN_TM, FFN_TF, layer == depth - 1, "ffn%d" % layer)
    return h.reshape(bsz, seq, d)
```

```python
import functools
import math

import jax
import jax.numpy as jnp
from jax import lax
from jax.experimental import pallas as pl
from jax.experimental.pallas import tpu as pltpu

F32 = jnp.float32
BF16 = jnp.bfloat16

EPS = 1e-6
A_WIDTH = 1024
A_CONV = 4
LRU_C = 8.0
B_WIDTH = 1024
B_HEAD = 64
B_DECAY_RANK = 64
B_AAA_RANK = 64
B_GATE_RANK = 160
B_SMALL = 512
B_SMALL_K = 384
B_LN_EPS = 64e-5
B_KK_FLOOR = 1e-12
C_WIDTH = 4096
C_HEADS = 8
C_HEAD = 512
C_CONV = 4
FFN_CONV = 3

SUBLANES = 8
LANES = 128
MXU_DIM = 256
VMEM_LIMIT = 56 * 1024 * 1024

RWKV_CHUNK = 64
RWKV_GROUP = 4 * B_HEAD
RWKV_ROWS = 512
MLSTM_CHUNK = 256
MLSTM_HEADS_PER_STEP = 8
PROJ_TM, PROJ_TN = 1024, 512
NORM_SUB_ROWS = 256
EVEN_OUT_TM = 512
PROJ_TN_WIDE = 1024
FFN_TM, FFN_TF = 512, 512
RGLRU_GROUP = MXU_DIM
MLSTM_IN_COLS, MLSTM_IN_SUB = 512, 256


def _params(sem):
    return pltpu.CompilerParams(dimension_semantics=sem, vmem_limit_bytes=VMEM_LIMIT)


def _dot(a, b):
    return jnp.dot(a, b, preferred_element_type=F32)


def _dot_nt(a, b):
    return lax.dot_general(a, b, (((1,), (1,)), ((), ())), preferred_element_type=F32)


def _dot_tn(a, b):
    return lax.dot_general(a, b, (((0,), (0,)), ((), ())), preferred_element_type=F32)


def _split_bf16(x, terms):
    parts = []
    for _ in range(terms):
        p = x.astype(BF16)
        parts.append(p)
        x = x - p.astype(F32)
    return parts


def _sigmoid(x):
    return 1.0 / (1.0 + jnp.exp(-x))


def _silu(x):
    return x * _sigmoid(x)


def _rms(x, g):
    return x * lax.rsqrt(jnp.mean(x * x, axis=-1, keepdims=True) + EPS) * g


def _shift_rows(x, k, prev8):
    r = pltpu.roll(x, k, 0)
    fix = pltpu.roll(prev8, k, 0)
    row = lax.broadcasted_iota(jnp.int32, (SUBLANES, x.shape[1]), 0)
    head = jnp.where(row < k, fix, r[:SUBLANES])
    return jnp.concatenate([head, r[SUBLANES:]], axis=0)


def _norm_matmul_kernel(x_ref, g_ref, w_ref, o_ref, hn_ref, *, silu):
    j = pl.program_id(1)
    tm = x_ref.shape[0]

    def store(rows, y):
        o_ref[rows, :] = (_silu(y) if silu else y).astype(o_ref.dtype)

    @pl.when(j == 0)
    def _():
        for s in range(tm // NORM_SUB_ROWS):
            rows = pl.ds(s * NORM_SUB_ROWS, NORM_SUB_ROWS)
            hn = _rms(x_ref[rows, :], g_ref[...]).astype(BF16)
            hn_ref[rows, :] = hn
            store(rows, _dot(hn, w_ref[...]))

    @pl.when(j != 0)
    def _():
        store(pl.ds(0, tm), _dot(hn_ref[...], w_ref[...]))


def _norm_matmul(x, g, w, tm, tn, name, col_start=0, silu_bf16=False):
    n, d = x.shape
    nout = w.shape[1] - col_start
    off = col_start // tn
    assert col_start % tn == 0 and nout % tn == 0
    return pl.pallas_call(
        functools.partial(_norm_matmul_kernel, silu=silu_bf16),
        grid=(n // tm, nout // tn),
        in_specs=[pl.BlockSpec((tm, d), lambda i, j: (i, 0)),
                  pl.BlockSpec((1, d), lambda i, j: (0, 0)),
                  pl.BlockSpec((d, tn), lambda i, j: (0, j + off))],
        out_specs=pl.BlockSpec((tm, tn), lambda i, j: (i, j)),
        out_shape=jax.ShapeDtypeStruct((n, nout), BF16 if silu_bf16 else F32),
        scratch_shapes=[pltpu.VMEM((tm, d), BF16)],
        compiler_params=_params(("arbitrary", "arbitrary")),
        name=name,
    )(x, g, w)


def _resid_matmul_kernel(*refs, n_in):
    x_ref = refs[0]
    a_refs = refs[1:1 + n_in]
    w_refs = refs[1 + n_in:1 + 2 * n_in]
    o_ref = refs[1 + 2 * n_in]
    acc = x_ref[...]
    for a_ref, w_ref in zip(a_refs, w_refs):
        acc = acc + _dot(a_ref[...], w_ref[...])
    o_ref[...] = acc


def _resid_matmul(x, acts, ws, tm, tn, name):
    n, d = x.shape
    n_in = len(acts)
    in_specs = [pl.BlockSpec((tm, tn), lambda i, j: (i, j))]
    in_specs += [pl.BlockSpec((tm, a.shape[1]), lambda i, j: (i, 0)) for a in acts]
    in_specs += [pl.BlockSpec((w.shape[0], tn), lambda i, j: (0, j)) for w in ws]
    return pl.pallas_call(
        functools.partial(_resid_matmul_kernel, n_in=n_in),
        grid=(n // tm, d // tn),
        in_specs=in_specs,
        out_specs=pl.BlockSpec((tm, tn), lambda i, j: (i, j)),
        out_shape=jax.ShapeDtypeStruct((n, d), F32),
        compiler_params=_params(("arbitrary", "arbitrary")),
        name=name,
    )(x, *acts, *ws)


def _ffn_kernel(x_ref, g_ref, wg_ref, wu_ref, cw_ref, cb_ref, wd_ref, fg_ref, o_ref, hn_ref, carry_ref,
                *, tiles_per_seq, final_norm):
    i = pl.program_id(0)
    j = pl.program_id(1)
    tm = x_ref.shape[0]

    def activate_and_project(gate, up):
        seq_start = (i % tiles_per_seq) == 0
        prev8 = jnp.where(seq_start, 0.0, carry_ref[j])
        carry_ref[j] = gate[tm - SUBLANES:]
        cw = cw_ref[...]
        conv = (cb_ref[...] + gate * cw[2:3] + _shift_rows(gate, 1, prev8) * cw[1:2]
                + _shift_rows(gate, 2, prev8) * cw[0:1])
        u = (_silu(conv) * up).astype(BF16)
        o_ref[...] += _dot(u, wd_ref[...])

    @pl.when(j == 0)
    def _():
        gates, ups = [], []
        for s in range(tm // NORM_SUB_ROWS):
            rows = pl.ds(s * NORM_SUB_ROWS, NORM_SUB_ROWS)
            x = x_ref[rows, :]
            hn = _rms(x, g_ref[...]).astype(BF16)
            hn_ref[rows, :] = hn
            o_ref[rows, :] = x
            gates.append(_dot(hn, wg_ref[...]))
            ups.append(_dot(hn, wu_ref[...]))
        activate_and_project(jnp.concatenate(gates, axis=0), jnp.concatenate(ups, axis=0))

    @pl.when(j != 0)
    def _():
        hn = hn_ref[...]
        activate_and_project(_dot(hn, wg_ref[...]), _dot(hn, wu_ref[...]))

    if final_norm:
        @pl.when(j == pl.num_programs(1) - 1)
        def _():
            o_ref[...] = _rms(o_ref[...], fg_ref[...])


def _ffn(x, g, wg, wu, cw, cb, wd, fg, layer, seq, tm, tf, final_norm, name):
    n, d = x.shape
    f = wg.shape[2]
    nf = f // tf
    kern = functools.partial(_ffn_kernel, tiles_per_seq=seq // tm, final_norm=final_norm)
    return pl.pallas_call(
        kern,
        grid=(n // tm, nf),
        in_specs=[pl.BlockSpec((tm, d), lambda i, j: (i, 0)),
                  pl.BlockSpec((1, d), lambda i, j: (0, 0)),
                  pl.BlockSpec((None, d, tf), lambda i, j: (layer, 0, j)),
                  pl.BlockSpec((None, d, tf), lambda i, j: (layer, 0, j)),
                  pl.BlockSpec((FFN_CONV, tf), lambda i, j: (0, j)),
                  pl.BlockSpec((1, tf), lambda i, j: (0, j)),
                  pl.BlockSpec((None, tf, d), lambda i, j: (layer, j, 0)),
                  pl.BlockSpec((1, d), lambda i, j: (0, 0))],
        out_specs=pl.BlockSpec((tm, d), lambda i, j: (i, 0)),
        out_shape=jax.ShapeDtypeStruct((n, d), F32),
        scratch_shapes=[pltpu.VMEM((tm, d), BF16), pltpu.VMEM((nf, SUBLANES, tf), F32)],
        compiler_params=_params(("arbitrary", "arbitrary")),
        name=name,
    )(x, g, wg, wu, cw, cb, wd, fg)


def _rglru_gates(x, prev8, cw, cb, wr, wi):
    xc = cb + x * cw[A_CONV - 1:A_CONV]
    for k in range(1, A_CONV):
        xc = xc + _shift_rows(x, k, prev8) * cw[A_CONV - 1 - k:A_CONV - k]
    xb = xc.astype(BF16)
    return xc, _dot(xb, wr), _dot(xb, wi)


def _rglru_scan(xc, r_pre, i_pre, ga, h0, br, bi, lam):
    rows, width = xc.shape
    r = _sigmoid(r_pre + br)
    ig = _sigmoid(i_pre + bi)
    neg_lam = -lam
    softplus = jnp.maximum(neg_lam, 0.0) + jnp.log1p(jnp.exp(-jnp.abs(neg_lam)))
    log_a = (-LRU_C) * r * softplus
    a = jnp.exp(log_a)
    u = jnp.sqrt(jnp.maximum(1.0 - a * a, 0.0)) * (ig * xc)

    n_sub = rows // SUBLANES
    a3 = a.reshape(n_sub, SUBLANES, width)
    u3 = u.reshape(n_sub, SUBLANES, width)
    sub = lax.broadcasted_iota(jnp.int32, (n_sub, SUBLANES, width), 1)
    s = 1
    while s < SUBLANES:
        keep = sub >= s
        a_prev = pltpu.roll(a3, s, 1)
        u_prev = pltpu.roll(u3, s, 1)
        u3 = jnp.where(keep, a3 * u_prev + u3, u3)
        a3 = jnp.where(keep, a3 * a_prev, a3)
        s *= 2
    carry = h0
    groups = []
    for gi in range(n_sub):
        hg = u3[gi] + a3[gi] * carry
        groups.append(hg)
        carry = hg[SUBLANES - 1:]
    h = jnp.concatenate(groups, axis=0)
    gelu = 0.5 * ga * (1.0 + jnp.tanh(math.sqrt(2.0 / math.pi) * (ga + 0.044715 * (ga * ga * ga))))
    return (h * gelu).astype(BF16), carry


def _even_in_kernel(x_ref, g_ref, w_ref, cw_ref, cb_ref, wr_ref, br_ref, wi_ref, bi_ref, lam_ref,
                    p_ref, ya_ref, hn_ref, act_ref, prev_ref, h_ref, *, tiles_per_seq):
    i = pl.program_id(0)
    j = pl.program_id(1)
    tm = x_ref.shape[0]
    tn = w_ref.shape[1]
    grp = RGLRU_GROUP
    n_units = A_WIDTH // grp
    n_act = 2 * A_WIDTH // tn
    per_tile = tn // grp

    def stash(rows, y, tile):
        for part in range(per_tile):
            act_ref[per_tile * tile + part, rows, :] = y[:, part * grp:(part + 1) * grp]

    @pl.when(j == 0)
    def _():
        for s in range(tm // NORM_SUB_ROWS):
            rows = pl.ds(s * NORM_SUB_ROWS, NORM_SUB_ROWS)
            hn = _rms(x_ref[rows, :], g_ref[...]).astype(BF16)
            hn_ref[rows, :] = hn
            stash(rows, _dot(hn, w_ref[...]), 0)

    @pl.when((j > 0) & (j < n_act))
    def _():
        stash(pl.ds(0, tm), _dot(hn_ref[...], w_ref[...]), j)

    @pl.when((j >= n_act) & (j < n_act + n_units))
    def _():
        unit = j - n_act
        seq_start = (i % tiles_per_seq) == 0
        prev8 = jnp.where(seq_start, 0.0, prev_ref[unit])
        h_last = jnp.where(seq_start, 0.0, h_ref[unit])
        gate_w = (cw_ref[unit], cb_ref[unit], wr_ref[unit], wi_ref[unit])
        scan_w = (br_ref[unit], bi_ref[unit], lam_ref[unit])
        sub = lambda s: pl.ds(s * NORM_SUB_ROWS, NORM_SUB_ROWS)
        for s in range(tm // NORM_SUB_ROWS):
            x = act_ref[unit, sub(s), :]
            gated = _rglru_gates(x, prev8, *gate_w)
            prev8 = x[NORM_SUB_ROWS - SUBLANES:]
            projected = _dot(hn_ref[sub(s), :], w_ref[...])
            y, h_last = _rglru_scan(*gated, act_ref[n_units + unit, sub(s), :], h_last, *scan_w)
            p_ref[sub(s), :] = projected
            ya_ref[sub(s), :] = y
        prev_ref[unit] = prev8
        h_ref[unit] = h_last

    @pl.when(j >= n_act + n_units)
    def _():
        p_ref[...] = _dot(hn_ref[...], w_ref[...])


def _even_in(x, g, w_all, cw, cb, wr, br, wi, bi, lam, seq, tm, tn):
    n, d = x.shape
    grp = RGLRU_GROUP
    n_units = A_WIDTH // grp
    n_tiles = w_all.shape[1] // tn
    n_act = 2 * A_WIDTH // tn
    assert (2 * A_WIDTH) % tn == 0 and tn % grp == 0 and n_tiles >= n_act + n_units
    group_vec = lambda v: v.reshape(n_units, 1, grp)
    whole = lambda a: pl.BlockSpec(a.shape, lambda i, j: (0,) * a.ndim)
    params = [jnp.transpose(cw.reshape(A_CONV, n_units, grp), (1, 0, 2)), group_vec(cb), wr, group_vec(br),
              wi, group_vec(bi), group_vec(lam)]
    return pl.pallas_call(
        functools.partial(_even_in_kernel, tiles_per_seq=seq // tm),
        grid=(n // tm, n_tiles),
        in_specs=[pl.BlockSpec((tm, d), lambda i, j: (i, 0)),
                  pl.BlockSpec((1, d), lambda i, j: (0, 0)),
                  pl.BlockSpec((d, tn), lambda i, j: (0, j))] + [whole(a) for a in params],
        out_specs=[pl.BlockSpec((tm, tn), lambda i, j: (i, jnp.maximum(j - n_act, 0))),
                   pl.BlockSpec((tm, grp), lambda i, j: (i, jnp.clip(j - n_act, 0, n_units - 1)))],
        out_shape=[jax.ShapeDtypeStruct((n, w_all.shape[1] - 2 * A_WIDTH), F32),
                   jax.ShapeDtypeStruct((n, A_WIDTH), BF16)],
        scratch_shapes=[pltpu.VMEM((tm, d), BF16), pltpu.VMEM((2 * n_units, tm, grp), F32),
                        pltpu.VMEM((n_units, SUBLANES, grp), F32), pltpu.VMEM((n_units, 1, grp), F32)],
        compiler_params=_params(("arbitrary", "arbitrary")),
        name="even_in",
    )(x, g, w_all, *params)


def _alternate(*stage_streams):
    results = [None] * len(stage_streams)
    live = list(range(len(stage_streams)))
    while live:
        for idx in list(live):
            try:
                next(stage_streams[idx])
            except StopIteration as stop:
                results[idx] = stop.value
                live.remove(idx)
    return results


def _rwkv_kernel(r_ref, k_ref, v_ref, sm_ref, mur_ref, muk_ref, muv_ref, mus_ref, w0_ref, wup_ref, a0_ref,
                 aup_ref, gup_ref, kkw_ref, kaw_ref, rkw_ref, lnw_ref, lnb_ref, o_ref,
                 s_ref, pr_ref, pk_ref, pv_ref, ps_ref):
    tt = r_ref.shape[0]
    L = RWKV_CHUNK
    gw = RWKV_GROUP
    n_grp = B_WIDTH // gw
    hr = tt // 2

    @pl.when(pl.program_id(1) == 0)
    def _():
        s_ref[...] = jnp.zeros_like(s_ref)
        pr_ref[...] = jnp.zeros_like(pr_ref)
        pk_ref[...] = jnp.zeros_like(pk_ref)
        pv_ref[...] = jnp.zeros_like(pv_ref)
        ps_ref[...] = jnp.zeros_like(ps_ref)

    def lerp(x_ref, p_ref, mu_ref):
        x = x_ref[...]
        xs = _shift_rows(x, 1, p_ref[...])
        p_ref[...] = x[tt - SUBLANES:]
        return x + (xs - x) * mu_ref[...]

    r_all = lerp(r_ref, pr_ref, mur_ref)
    k_all = lerp(k_ref, pk_ref, muk_ref)
    v_all = lerp(v_ref, pv_ref, muv_ref)
    sm_all = lerp(sm_ref, ps_ref, mus_ref)[:, :B_SMALL_K]

    row_g = lax.broadcasted_iota(jnp.int32, (gw, gw), 0)
    col_g = lax.broadcasted_iota(jnp.int32, (gw, gw), 1)
    same_head = (row_g // B_HEAD) == (col_g // B_HEAD)
    head_mask = jnp.where(same_head, 1.0, 0.0)
    head_mask_bf = head_mask.astype(BF16)

    def head_sum(x, terms):
        n = x.shape[0]
        xs = jnp.concatenate([x[:, gi * gw:(gi + 1) * gw] for gi in range(n_grp)], axis=0)
        s = _dot(jnp.concatenate(_split_bf16(xs, terms), axis=0), head_mask_bf)
        s = sum(s[t * n_grp * n:(t + 1) * n_grp * n] for t in range(terms))
        return jnp.concatenate([s[gi * n:(gi + 1) * n] for gi in range(n_grp)], axis=1)

    def bd(x):
        return jnp.concatenate([x.astype(BF16)] * (gw // B_HEAD), axis=0) * head_mask_bf

    row_t = lax.broadcasted_iota(jnp.int32, (hr, hr), 0)
    col_t = lax.broadcasted_iota(jnp.int32, (hr, hr), 1)
    tril = jnp.where((row_t >= col_t) & ((row_t // L) == (col_t // L)), 1.0, 0.0).astype(BF16)
    row_p = lax.broadcasted_iota(jnp.int32, (L, gw), 0)
    src_p = lax.broadcasted_iota(jnp.int32, (L, gw), 1) % B_HEAD
    strict_lower = row_p > src_p
    lower = row_p >= src_p
    n_doublings = int(math.log2(L)) - 1
    groups = range(n_grp)
    half_units = [(c, gi) for c in range(hr // L) for gi in groups]

    def tile(arr, c, gi):
        return arr[c * L:(c + 1) * L, gi * gw:(gi + 1) * gw]

    def prepare(h):
        rows = slice(h * hr, (h + 1) * hr)
        r, k, v, sm = r_all[rows], k_all[rows], v_all[rows], sm_all[rows]
        z = w0_ref[...] + _dot(jnp.tanh(sm).astype(BF16), wup_ref[...])
        log_w = (-math.exp(-0.5)) * _sigmoid(z)
        yield
        a = _sigmoid(a0_ref[...] + _dot(sm.astype(BF16), aup_ref[...]))
        yield
        g = _dot(_sigmoid(sm).astype(BF16), gup_ref[...])
        kk = k * kkw_ref[...]
        yield
        kk = kk * lax.rsqrt(jnp.maximum(head_sum(kk * kk, 2), B_KK_FLOOR))
        yield
        k2 = k * (1.0 + (a - 1.0) * kaw_ref[...])
        yield
        cum = sum(_dot(tril, part) for part in _split_bf16(log_w, 3))
        yield
        p_in = jnp.exp(cum)
        p_inv = jnp.exp(-cum)
        yield
        a_bar = (-kk) * jnp.exp(cum - log_w)
        r_bar = r * p_in
        yield
        b_bar = kk * a * p_inv
        k_bar = k2 * p_inv
        return dict(r=r, v=v, g=g, k2=k2, p_in=p_in, a_bar=a_bar, r_bar=r_bar, b_bar=b_bar, k_bar=k_bar)

    def products(pre, states):
        units = half_units
        ar = {u: jnp.concatenate([tile(pre["a_bar"], *u), tile(pre["r_bar"], *u)], axis=0).astype(BF16)
              for u in units}
        m_b = {u: _dot_nt(ar[u], bd(tile(pre["b_bar"], *u))) for u in units}
        yield
        m_k = {u: _dot_nt(ar[u], bd(tile(pre["k_bar"], *u))) for u in units}
        yield
        x = {u: jnp.where(strict_lower, m_b[u][:L], 0.0) for u in units}
        a_rb = {u: jnp.where(lower, m_b[u][L:], 0.0).astype(BF16) for u in units}
        akrk = {u: jnp.concatenate([jnp.where(strict_lower, m_k[u][:L], 0.0),
                                    jnp.where(lower, m_k[u][L:], 0.0)], axis=0).astype(BF16) for u in units}
        cy = {u: _dot(akrk[u], bd(tile(pre["v"], *u))) for u in units}
        yield
        n_inv = dict(x)
        x_pow = {u: _dot(x[u].astype(BF16), bd(x[u])) for u in units}
        yield
        for step in range(n_doublings):
            if step + 1 < n_doublings:
                both = {u: _dot(jnp.concatenate([x_pow[u], n_inv[u]], axis=0).astype(BF16), bd(x_pow[u]))
                        for u in units}
                n_inv = {u: n_inv[u] + x_pow[u] + both[u][L:] for u in units}
                x_pow = {u: both[u][:L] for u in units}
            else:
                n_inv = {u: n_inv[u] + x_pow[u] + _dot(n_inv[u].astype(BF16), bd(x_pow[u])) for u in units}
            yield
        n_inv = {u: n_inv[u].astype(BF16) for u in units}
        y_rows = []
        for c in range(hr // L):
            p_last = pre["p_in"][(c + 1) * L - 1:(c + 1) * L]
            pl_g = [p_last[:, gi * gw:(gi + 1) * gw] for gi in groups]
            bk = [jnp.concatenate([tile(pre["b_bar"], c, gi) * pl_g[gi], tile(pre["k_bar"], c, gi) * pl_g[gi]],
                                  axis=0).astype(BF16) for gi in groups]
            ah = [_dot_nt(ar[c, gi], states[gi].astype(BF16)) for gi in groups]
            yield
            rhs = [ah[gi][:L] + cy[c, gi][:L] for gi in groups]
            u_c = [rhs[gi] + _dot(n_inv[c, gi], bd(rhs[gi])) for gi in groups]
            yield
            ds = [_dot_tn(jnp.concatenate([u_c[gi], tile(pre["v"], c, gi)], axis=0).astype(BF16), bk[gi])
                  for gi in groups]
            states = [states[gi] * pl_g[gi] + head_mask * ds[gi] for gi in groups]
            y_rows.append(jnp.concatenate(
                [ah[gi][L:] + cy[c, gi][L:] + _dot(a_rb[c, gi], bd(u_c[gi])) for gi in groups], axis=1))
            yield
        return jnp.concatenate(y_rows, axis=0), states

    def finish(h, pre, y):
        inv_n = 1.0 / B_HEAD
        yc = y - head_sum(y, 1) * inv_n
        yield
        var = head_sum(yc * yc, 1) * inv_n
        yield
        yn = yc * lax.rsqrt(var + B_LN_EPS) * lnw_ref[...] + lnb_ref[...]
        yield
        bonus = head_sum(pre["r"] * pre["k2"] * rkw_ref[...], 1) * pre["v"]
        yield
        o_ref[pl.ds(h * hr, hr), :] = ((yn + bonus) * pre["g"]).astype(o_ref.dtype)

    states = [s_ref[gi] for gi in groups]
    (pre0,) = _alternate(prepare(0))
    (y0, states), pre1 = _alternate(products(pre0, states), prepare(1))
    (y1, states), _ = _alternate(products(pre1, states), finish(0, pre0, y0))
    _alternate(finish(1, pre1, y1))
    for gi in groups:
        s_ref[gi] = states[gi]


def _rwkv(p, mur, muk, muv, mus, w0, wup, a0, aup, gup, kkw, kaw, rkw, lnw, lnb, bsz, seq):
    n = p.shape[0]
    tt = RWKV_ROWS
    nt = seq // tt
    w = B_WIDTH
    vec = lambda: pl.BlockSpec((1, w), lambda b, t: (0, 0))
    mat = lambda: pl.BlockSpec((B_SMALL_K, w), lambda b, t: (0, 0))
    return pl.pallas_call(
        _rwkv_kernel,
        grid=(bsz, nt),
        in_specs=[pl.BlockSpec((tt, w), lambda b, t: (b * nt + t, 0)),
                  pl.BlockSpec((tt, w), lambda b, t: (b * nt + t, 1)),
                  pl.BlockSpec((tt, w), lambda b, t: (b * nt + t, 2)),
                  pl.BlockSpec((tt, B_SMALL), lambda b, t: (b * nt + t, 3 * w // B_SMALL)),
                  vec(), vec(), vec(), pl.BlockSpec((1, B_SMALL), lambda b, t: (0, 0)),
                  vec(), mat(), vec(), mat(), mat(), vec(), vec(), vec(), vec(), vec()],
        out_specs=pl.BlockSpec((tt, w), lambda b, t: (b * nt + t, 0)),
        out_shape=jax.ShapeDtypeStruct((n, w), BF16),
        scratch_shapes=[pltpu.VMEM((w // RWKV_GROUP, RWKV_GROUP, RWKV_GROUP), F32),
                        pltpu.VMEM((SUBLANES, w), F32), pltpu.VMEM((SUBLANES, w), F32),
                        pltpu.VMEM((SUBLANES, w), F32), pltpu.VMEM((SUBLANES, B_SMALL), F32)],
        compiler_params=_params(("arbitrary", "arbitrary")),
        name="rwkv7",
    )(p, p, p, p, mur, muk, muv, mus, w0, wup, a0, aup, gup, kkw, kaw, rkw, lnw, lnb)


def _mlstm_in_kernel(x_ref, g_ref, w_ref, cw_ref, cb_ref, wq_ref, wk_ref, wv_ref, wif_ref, bif_ref,
                     q_ref, k_ref, v_ref, xc_ref, gates_ref, hn_ref, prev_ref, *, tiles_per_seq, sub_rows):
    i = pl.program_id(0)
    j = pl.program_id(1)
    tm = x_ref.shape[0]
    cb = w_ref.shape[1]
    n_grp = cb // MXU_DIM

    def blockdiag(xb, wb_ref):
        return jnp.concatenate(
            [_dot(xb[:, g * MXU_DIM:(g + 1) * MXU_DIM], wb_ref[g]) for g in range(n_grp)], axis=1)

    def body(first_col_tile):
        def project(s):
            rows = pl.ds(s * sub_rows, sub_rows)
            if first_col_tile:
                hn = _rms(x_ref[rows, :], g_ref[...]).astype(BF16)
                hn_ref[rows, :] = hn
            else:
                hn = hn_ref[rows, :]
            return _dot(hn, w_ref[...])

        seq_start = (i % tiles_per_seq) == 0
        prev8 = jnp.where(seq_start, 0.0, prev_ref[j])
        cw = cw_ref[...]
        n_sub = tm // sub_rows
        xm_next = project(0)
        for s in range(n_sub):
            rows = pl.ds(s * sub_rows, sub_rows)
            xm = xm_next
            if s + 1 < n_sub:
                xm_next = project(s + 1)
            conv = cb_ref[...] + xm * cw[C_CONV - 1:C_CONV]
            for kk in range(1, C_CONV):
                conv = conv + _shift_rows(xm, kk, prev8) * cw[C_CONV - 1 - kk:C_CONV - kk]
            prev8 = xm[sub_rows - SUBLANES:]
            xc = _silu(conv)
            xcb = xc.astype(BF16)
            xc_ref[rows, :] = xcb
            q = blockdiag(xcb, wq_ref)
            k = blockdiag(xcb, wk_ref)
            v = blockdiag(xm.astype(BF16), wv_ref)
            qb = q.astype(BF16)
            kb = k.astype(BF16)
            vb = v.astype(BF16)
            q_ref[rows, :] = qb
            k_ref[rows, :] = (k * (C_HEAD ** -0.5)).astype(BF16)
            v_ref[rows, :] = vb
            gate_part = _dot(qb, wif_ref[0]) + _dot(kb, wif_ref[1]) + _dot(vb, wif_ref[2])
            if first_col_tile:
                gates_ref[rows, :] = bif_ref[...] + gate_part
            else:
                gates_ref[rows, :] += gate_part
        prev_ref[j] = prev8

    pl.when(j == 0)(functools.partial(body, True))
    pl.when(j != 0)(functools.partial(body, False))


def _mlstm_in(x, g, w_in, cw, cb, wq, wk, wv, wif, bif, seq, tm, cblk, sub_rows):
    n, d = x.shape
    ncb = C_WIDTH // cblk
    gpb = cblk // MXU_DIM
    blk = lambda: pl.BlockSpec((tm, cblk), lambda i, j: (i, j))
    wspec = lambda: pl.BlockSpec((gpb, MXU_DIM, MXU_DIM), lambda i, j: (j, 0, 0))
    act = lambda dt: jax.ShapeDtypeStruct((n, C_WIDTH), dt)
    kern = functools.partial(_mlstm_in_kernel, tiles_per_seq=seq // tm, sub_rows=sub_rows)
    return pl.pallas_call(
        kern,
        grid=(n // tm, ncb),
        in_specs=[pl.BlockSpec((tm, d), lambda i, j: (i, 0)),
                  pl.BlockSpec((1, d), lambda i, j: (0, 0)),
                  pl.BlockSpec((d, cblk), lambda i, j: (0, j)),
                  pl.BlockSpec((C_CONV, cblk), lambda i, j: (0, j)),
                  pl.BlockSpec((1, cblk), lambda i, j: (0, j)),
                  wspec(), wspec(), wspec(),
                  pl.BlockSpec((3, cblk, 2 * C_HEADS), lambda i, j: (0, j, 0)),
                  pl.BlockSpec((1, 2 * C_HEADS), lambda i, j: (0, 0))],
        out_specs=[blk(), blk(), blk(), blk(),
                   pl.BlockSpec((tm, 2 * C_HEADS), lambda i, j: (i, 0))],
        out_shape=[act(BF16), act(BF16), act(BF16), act(BF16),
                   jax.ShapeDtypeStruct((n, 2 * C_HEADS), F32)],
        scratch_shapes=[pltpu.VMEM((tm, d), BF16), pltpu.VMEM((ncb, SUBLANES, cblk), F32)],
        compiler_params=_params(("arbitrary", "arbitrary")),
        name="mlstm_in",
    )(x, g, w_in, cw, cb, wq, wk, wv, wif, bif)


def _mlstm_gate_kernel(f_ref, b_ref):
    L = f_ref.shape[1]
    f = f_ref[...]
    lf = jnp.minimum(f, 0.0) - jnp.log1p(jnp.exp(-jnp.abs(f)))
    row = lax.broadcasted_iota(jnp.int32, (L, L), 0)
    col = lax.broadcasted_iota(jnp.int32, (L, L), 1)
    triu = jnp.where(row <= col, 1.0, 0.0).astype(BF16)
    b_ref[...] = sum(_dot(part, triu) for part in _split_bf16(lf, 3))


def _mlstm_gates(f_pre, chunk):
    rows, seq = f_pre.shape
    return pl.pallas_call(
        _mlstm_gate_kernel,
        grid=(seq // chunk,),
        in_specs=[pl.BlockSpec((rows, chunk), lambda c: (0, c))],
        out_specs=pl.BlockSpec((rows, chunk), lambda c: (0, c)),
        out_shape=jax.ShapeDtypeStruct((rows, seq), F32),
        compiler_params=_params(("arbitrary",)),
        name="mlstm_gates",
    )(f_pre)


def _mlstm_kernel(q_ref, k_ref, v_ref, xc_ref, za_ref, ir_ref, br_ref, lnw_ref, skip_ref,
                  o_ref, ct_ref, m_ref):
    c = pl.program_id(2)
    L = q_ref.shape[0]
    n_heads, hd = ct_ref.shape[0], ct_ref.shape[1]

    @pl.when(c == 0)
    def _():
        ct_ref[...] = jnp.zeros_like(ct_ref)
        m_ref[...] = jnp.zeros_like(m_ref)

    def lanes(x, width):
        return jnp.concatenate([x] * (width // LANES), axis=1)

    row = lax.broadcasted_iota(jnp.int32, (L, L), 0)
    col = lax.broadcasted_iota(jnp.int32, (L, L), 1)
    causal = row >= col
    eye = jnp.where(row == col, 1.0, 0.0).astype(BF16)

    def to_col(x_row):
        return sum(_dot_nt(eye, jnp.broadcast_to(part, (LANES, L))) for part in _split_bf16(x_row, 3))

    heads = range(n_heads)
    cols = [slice(i * hd, (i + 1) * hd) for i in heads]
    q = [q_ref[:, cols[i]] for i in heads]
    k = [k_ref[:, cols[i]] for i in heads]
    v_aug = [jnp.concatenate([v_ref[:, cols[i]], jnp.ones((L, LANES), BF16)], axis=1) for i in heads]
    li_row = [ir_ref[i] for i in heads]
    b_row = [br_ref[i] for i in heads]
    m_prev = [m_ref[i] for i in heads]

    qk = [_dot_nt(q[i], k[i]) for i in heads]
    q_ct = [_dot(q[i], ct_ref[i].astype(BF16)) for i in heads]
    b_col = [to_col(b_row[i]) for i in heads]
    li_col = [to_col(li_row[i]) for i in heads]
    b_last = [b_col[i][L - 1:L] for i in heads]

    d_log = [jnp.where(causal, lanes(b_col[i], L) - b_row[i] + li_row[i], -jnp.inf) for i in heads]
    inter = [b_col[i] + m_prev[i] for i in heads]
    m_t = [jnp.maximum(inter[i], jnp.max(d_log[i], axis=-1, keepdims=True)) for i in heads]
    s = [(qk[i] * jnp.exp(d_log[i] - lanes(m_t[i], L))).astype(BF16) for i in heads]
    sc = [jnp.exp(inter[i] - m_t[i]) for i in heads]
    num_den = [_dot(s[i], v_aug[i]) + lanes(sc[i], hd + LANES) * q_ct[i] for i in heads]

    g_log = [b_last[i] - b_col[i] + li_col[i] for i in heads]
    m_new = [jnp.maximum(b_last[i] + m_prev[i], jnp.max(g_log[i], axis=0, keepdims=True)) for i in heads]
    ke = [k[i] * lanes(jnp.exp(g_log[i] - m_new[i]).astype(BF16), hd) for i in heads]
    decay = [jnp.exp(b_last[i] + m_prev[i] - m_new[i]) for i in heads]
    for i in heads:
        ct_ref[i] = lanes(decay[i], hd + LANES) * ct_ref[i] + _dot_tn(ke[i], v_aug[i])
        m_ref[i] = m_new[i]

    for i in heads:
        inv = 1.0 / jnp.maximum(jnp.abs(num_den[i][:, hd:]), jnp.exp(-m_t[i]))
        h = num_den[i][:, :hd] * lanes(inv, hd)
        hc = h - jnp.mean(h, axis=-1, keepdims=True)
        hn = hc * lax.rsqrt(jnp.mean(hc * hc, axis=-1, keepdims=True) + EPS) * lnw_ref[:, cols[i]]
        hs = hn + skip_ref[:, cols[i]] * xc_ref[:, cols[i]].astype(F32)
        o_ref[:, cols[i]] = (hs * za_ref[:, cols[i]].astype(F32)).astype(o_ref.dtype)


def _mlstm(q, k, v, xc, z_act, i_row, b_row, lnw, skip, bsz, seq):
    n = q.shape[0]
    L = MLSTM_CHUNK
    nc = seq // L
    hd = C_HEAD
    hp = MLSTM_HEADS_PER_STEP
    blk = lambda: pl.BlockSpec((L, hp * hd), lambda b, h, c: (b * nc + c, h))
    rowspec = lambda: pl.BlockSpec((None, hp, 1, L), lambda b, h, c: (b, h, 0, c))
    vec = lambda: pl.BlockSpec((1, hp * hd), lambda b, h, c: (0, h))
    return pl.pallas_call(
        _mlstm_kernel,
        grid=(bsz, C_HEADS // hp, nc),
        in_specs=[blk(), blk(), blk(), blk(), blk(), rowspec(), rowspec(), vec(), vec()],
        out_specs=blk(),
        out_shape=jax.ShapeDtypeStruct((n, C_WIDTH), BF16),
        scratch_shapes=[pltpu.VMEM((hp, hd, hd + LANES), F32), pltpu.VMEM((hp, 1, LANES), F32)],
        compiler_params=_params(("arbitrary", "arbitrary", "arbitrary")),
        name="mlstm",
    )(q, k, v, xc, z_act, i_row, b_row, lnw, skip)


def _pack_block_diag(w, tile):
    g, bs, _ = w.shape
    per = tile // bs
    rows = jnp.tile(w.reshape(g // per, tile, bs), (1, 1, per))
    r = lax.broadcasted_iota(jnp.int32, (tile, tile), 0) // bs
    c = lax.broadcasted_iota(jnp.int32, (tile, tile), 1) // bs
    return jnp.where(r == c, rows, 0.0)


def _row(v):
    return v.reshape(1, -1)


def _even_layer(x, bsz, seq, norm, w_in, a_conv_w, a_conv_b, a_w_r, a_b_r, a_w_i, a_b_i, a_lambda,
                b_mu, b_w0, b_w_up, b_a0, b_a_up, b_g_up, b_k_k, b_k_a, b_r_k, b_ln_w, b_ln_b, w_out):
    main_w = 2 * A_WIDTH + 3 * B_WIDTH
    n_small = B_DECAY_RANK + B_AAA_RANK + B_GATE_RANK
    pad = B_SMALL - n_small
    assert main_w % B_SMALL == 0
    w_all = jnp.pad(w_in, ((0, 0), (0, pad))).astype(BF16)
    p, ya = _even_in(x, _row(norm), w_all, a_conv_w, a_conv_b,
                     _pack_block_diag(a_w_r, RGLRU_GROUP).astype(BF16), a_b_r,
                     _pack_block_diag(a_w_i, RGLRU_GROUP).astype(BF16), a_b_i, a_lambda, seq, PROJ_TM, PROJ_TN)

    mur, muk, muv = (_row(b_mu[i * B_WIDTH:(i + 1) * B_WIDTH]) for i in range(3))
    mus = _row(jnp.pad(b_mu[3 * B_WIDTH:], (0, pad)))

    def rows_at(w, start):
        out = jnp.zeros((B_SMALL_K, B_WIDTH), F32)
        return lax.dynamic_update_slice(out, w, (start, 0)).astype(BF16)

    wup = rows_at(b_w_up, 0)
    aup = rows_at(b_a_up, B_DECAY_RANK)
    gup = rows_at(b_g_up, B_DECAY_RANK + B_AAA_RANK)
    yb = _rwkv(p, mur, muk, muv, mus, _row(b_w0), wup, _row(b_a0), aup, gup, _row(b_k_k), _row(b_k_a),
               _row(b_r_k), _row(b_ln_w), _row(b_ln_b), bsz, seq)

    wo = w_out.astype(BF16)
    return _resid_matmul(x, [ya, yb], [wo[:A_WIDTH], wo[A_WIDTH:]], EVEN_OUT_TM, wo.shape[1], "even_out")


def _odd_layer(x, bsz, seq, norm, w_in, conv_w, conv_b, w_q, w_k, w_v, w_if, b_if, ln_w, skip, w_out):
    w_in_b = w_in.astype(BF16)
    g = _row(norm)
    z_act = _norm_matmul(x, g, w_in_b, PROJ_TM, PROJ_TN_WIDE, "odd_in_z", col_start=C_WIDTH, silu_bf16=True)
    q, k, v, xc, gates = _mlstm_in(
        x, g, w_in_b, conv_w, _row(conv_b),
        _pack_block_diag(w_q, MXU_DIM).astype(BF16), _pack_block_diag(w_k, MXU_DIM).astype(BF16),
        _pack_block_diag(w_v, MXU_DIM).astype(BF16), w_if.astype(BF16), _row(b_if), seq,
        PROJ_TM, MLSTM_IN_COLS, MLSTM_IN_SUB)
    gt = jnp.transpose(gates.reshape(bsz, seq, 2, C_HEADS), (2, 0, 3, 1))
    i_pre = gt[0]
    b_cum = _mlstm_gates(gt[1].reshape(bsz * C_HEADS, seq), MLSTM_CHUNK).reshape(bsz, C_HEADS, seq)
    hs = _mlstm(q, k, v, xc, z_act, i_pre[:, :, None, :], b_cum[:, :, None, :], _row(ln_w), _row(skip),
                bsz, seq)
    return _resid_matmul(x, [hs], [w_out.astype(BF16)], PROJ_TM, PROJ_TN_WIDE, "odd_out")


def kernel(x, even_norm, even_w_in, a_conv_w, a_conv_b, a_w_r, a_b_r, a_w_i, a_b_i, a_lambda, b_mu, b_w0, b_w_up, b_a0, b_a_up, b_g_up, b_k_k, b_k_a, b_r_k, b_ln_w, b_ln_b, even_w_out, odd_norm, odd_w_in, c_conv_w, c_conv_b, c_w_q, c_w_k, c_w_v, c_w_if, c_b_if, c_ln_w, c_skip, odd_w_out, ffn_norm, ffn_w_gate, ffn_w_up, ffn_conv_w, ffn_conv_b, ffn_w_down, final_norm):
    bsz, seq, d = x.shape
    depth = ffn_norm.shape[0]
    h = x.reshape(bsz * seq, d)
    w_gate, w_up, w_down = ffn_w_gate.astype(BF16), ffn_w_up.astype(BF16), ffn_w_down.astype(BF16)
    for layer in range(depth):
        if layer % 2 == 0:
            e = layer // 2
            h = _even_layer(h, bsz, seq, even_norm[e], even_w_in[e], a_conv_w[e], a_conv_b[e], a_w_r[e],
                            a_b_r[e], a_w_i[e], a_b_i[e], a_lambda[e], b_mu[e], b_w0[e], b_w_up[e], b_a0[e],
                            b_a_up[e], b_g_up[e], b_k_k[e], b_k_a[e], b_r_k[e].reshape(-1), b_ln_w[e],
                            b_ln_b[e], even_w_out[e])
        else:
            o = layer // 2
            h = _odd_layer(h, bsz, seq, odd_norm[o], odd_w_in[o], c_conv_w[o], c_conv_b[o], c_w_q[o], c_w_k[o],
                           c_w_v[o], c_w_if[o], c_b_if[o], c_ln_w[o], c_skip[o], odd_w_out[o])
        h = _ffn(h, _row(ffn_norm[layer]), w_gate, w_up, ffn_conv_w[layer], _row(ffn_conv_b[layer]), w_down,
                 _row(final_norm), layer, seq, FFN_TM, FFN_TF, layer == depth - 1, "ffn%d" % layer)
    return h.reshape(bsz, seq, d)
```
